```python
import jax
import jax.numpy as jnp
from jax import lax
import numpy as np

D_MODEL = 1024
BATCH = 8
SEQ = 2048
DEPTH = 4

GRID_W = 64
CTX_LEN = 256
QBLK = 128
ROPE_THETA = 10000.0
NORM_EPS = 1e-6
N_BRANCH = 4
BRANCH_W = D_MODEL // 4
SGU_CHUNK = 128
SGU_GROUPS = 4
MLA_HEADS = 4
MLA_Q_RANK = D_MODEL // 4
MLA_KV_RANK = D_MODEL // 8
MLA_NOPE = 64
MLA_ROPE = 32
MLA_V = BRANCH_W // MLA_HEADS
CONV_W = 3
GQA_HEADS = 4
GQA_KV_HEADS = 2
GQA_HEAD_DIM = BRANCH_W // GQA_HEADS
N_GROUPS = 4
EXPERTS_PER_GROUP = 8
N_EXPERTS = N_GROUPS * EXPERTS_PER_GROUP
TOP_K = 2
D_EXPERT = D_MODEL // 4
MOE_BLK = 128
SPLIT_SIZES = (BRANCH_W, BRANCH_W,
               MLA_Q_RANK, MLA_KV_RANK, MLA_ROPE,
               BRANCH_W, BRANCH_W, BRANCH_W,
               GQA_HEADS * GQA_HEAD_DIM, GQA_KV_HEADS * GQA_HEAD_DIM, GQA_KV_HEADS * GQA_HEAD_DIM)
D_IN = sum(SPLIT_SIZES)

kernel_name = 'hybrid_gated_mixers_hmoe_dit'


def _rmsnorm(x, gain=None):
    xf = x.astype(jnp.float32)
    y = xf * lax.rsqrt(jnp.mean(xf * xf, axis=-1, keepdims=True) + NORM_EPS)
    if gain is not None:
        y = y * gain.astype(jnp.float32)
    return y.astype(x.dtype)


def _split_cols(z, sizes):
    out, o = [], 0
    for s in sizes:
        out.append(z[..., o:o + s])
        o += s
    return out


def _axial_rope_tables(n_ctx, rows, rot_dim):
    n_freq = rot_dim // 4
    freqs = ROPE_THETA ** (-jnp.arange(n_freq, dtype=jnp.float32) / n_freq)
    row = jnp.repeat(jnp.arange(rows, dtype=jnp.float32), GRID_W)
    col = (jnp.arange(rows * GRID_W) % GRID_W).astype(jnp.float32)
    ang = jnp.concatenate([row[:, None] * freqs, col[:, None] * freqs], axis=-1)
    ang = jnp.concatenate([jnp.zeros((n_ctx, rot_dim // 2), jnp.float32), ang], axis=0)
    return jnp.cos(ang), jnp.sin(ang)


def _rope(x, cos, sin):
    half = x.shape[-1] // 2
    xf = x.astype(jnp.float32)
    x1, x2 = xf[..., :half], xf[..., half:]
    cc, ss = cos[None, :, None, :], sin[None, :, None, :]
    return jnp.concatenate([x1 * cc - x2 * ss, x1 * ss + x2 * cc], axis=-1).astype(x.dtype)


def _sdpa(q, k, v):
    b, tq, h, dk = q.shape
    hk = k.shape[2]
    qg = q.reshape(b, tq, hk, h // hk, dk)
    s = jnp.einsum('bqgrd,bkgd->bgrqk', qg, k).astype(jnp.float32) * (dk ** -0.5)
    p = jax.nn.softmax(s, axis=-1).astype(v.dtype)
    o = jnp.einsum('bgrqk,bkgd->bqgrd', p, v)
    return o.reshape(b, tq, h, v.shape[-1])


def _attention(q, k, v, n_ctx):
    b, n, h, dk = q.shape
    n_lat = n - n_ctx
    o_ctx = _sdpa(q[:, :n_ctx], k[:, :n_ctx], v[:, :n_ctx])
    qb = q[:, n_ctx:].reshape(b, n_lat // QBLK, QBLK, h, dk).transpose(1, 0, 2, 3, 4)
    o_lat = lax.map(lambda qq: _sdpa(qq, k, v), qb)
    o_lat = o_lat.transpose(1, 0, 2, 3, 4).reshape(b, n_lat, h, v.shape[-1])
    return jnp.concatenate([o_ctx, o_lat], axis=1)


def _per_seq(fn, z, n_ctx):
    return jnp.concatenate([fn(z[:, :n_ctx]), fn(z[:, n_ctx:])], axis=1)


def _chunk_mix(v, w_s, b_s):
    b, t, w = v.shape
    vc = v.reshape(b, t // SGU_CHUNK, SGU_CHUNK, SGU_GROUPS, w // SGU_GROUPS)
    m = jnp.einsum('gpq,bnqgc->bnpgc', w_s, vc) + b_s.T[None, None, :, :, None]
    return m.reshape(b, t, w)


def _dwconv(z, w):
    return lax.conv_general_dilated(z, w[:, None, :], (1,), ((CONV_W // 2, CONV_W // 2),),
                                    dimension_numbers=('NWC', 'WIO', 'NWC'),
                                    feature_group_count=z.shape[-1])


def _modulate(xn, shift_c, scale_c, shift_l, scale_l, n_ctx):
    hc = xn[:, :n_ctx] * (1 + scale_c) + shift_c
    hl = xn[:, n_ctx:] * (1 + scale_l[:, None]) + shift_l[:, None]
    return jnp.concatenate([hc, hl], axis=1)


def _gated(y, gate_c, gate_l, n_ctx):
    return jnp.concatenate([y[:, :n_ctx] * gate_c, y[:, n_ctx:] * gate_l[:, None]], axis=1)


def _mixer(h, n_ctx, rope_mla, rope_gqa, w_in, sgu_gain, w_sgu, b_sgu, mla_q_gain, w_uq,
           mla_kv_gain, w_ukv, w_conv, gqa_q_gain, gqa_k_gain, w_gate, b_gate, w_branch, w_out):
    b, n, _ = h.shape
    (a_u, a_v, m_q, m_kv, m_kpe, c_b, c_c, c_x, g_q, g_k, g_v) = _split_cols(h @ w_in, SPLIT_SIZES)

    a_v = _rmsnorm(jax.nn.gelu(a_v), sgu_gain)
    y_a = jax.nn.gelu(a_u) * _per_seq(lambda s: _chunk_mix(s, w_sgu, b_sgu), a_v, n_ctx)

    cos_m, sin_m = rope_mla
    q = (_rmsnorm(m_q, mla_q_gain) @ w_uq).reshape(b, n, MLA_HEADS, MLA_NOPE + MLA_ROPE)
    q = jnp.concatenate([q[..., :MLA_NOPE], _rope(q[..., MLA_NOPE:], cos_m, sin_m)], axis=-1)
    kv = (_rmsnorm(m_kv, mla_kv_gain) @ w_ukv).reshape(b, n, MLA_HEADS, MLA_NOPE + MLA_V)
    k_pe = _rope(m_kpe[:, :, None, :], cos_m, sin_m)
    k = jnp.concatenate([kv[..., :MLA_NOPE],
                         jnp.broadcast_to(k_pe, (b, n, MLA_HEADS, MLA_ROPE))], axis=-1)
    y_b = _attention(q, k, kv[..., MLA_NOPE:], n_ctx).reshape(b, n, BRANCH_W)

    y_c = c_b * _per_seq(lambda s: _dwconv(s, w_conv), c_c * c_x, n_ctx)

    cos_g, sin_g = rope_gqa
    qd = _rope(_rmsnorm(g_q.reshape(b, n, GQA_HEADS, GQA_HEAD_DIM), gqa_q_gain), cos_g, sin_g)
    kd = _rope(_rmsnorm(g_k.reshape(b, n, GQA_KV_HEADS, GQA_HEAD_DIM), gqa_k_gain), cos_g, sin_g)
    vd = g_v.reshape(b, n, GQA_KV_HEADS, GQA_HEAD_DIM)
    y_d = _attention(qd, kd, vd, n_ctx).reshape(b, n, BRANCH_W)

    merged = None
    for i, y in enumerate((y_a, y_b, y_c, y_d)):
        term = jax.nn.sigmoid(h @ w_gate[i] + b_gate[i]) * (y @ w_branch[i])
        merged = term if merged is None else merged + term
    return merged @ w_out


def _moe(h, w_gr, b_gr, w_er, b_er, w_gu, w_dn):
    shp = h.shape
    t = h.reshape(-1, shp[-1])
    n = t.shape[0]
    pg = jax.nn.softmax((t @ w_gr).astype(jnp.float32) + b_gr.astype(jnp.float32), axis=-1)
    pg_top, g_sel = lax.top_k(pg, 1)
    le = ((t @ w_er).astype(jnp.float32) + b_er.astype(jnp.float32)).reshape(n, N_GROUPS, EXPERTS_PER_GROUP)
    le = le[jnp.arange(n), g_sel[:, 0]]
    pe_top, e_sel = lax.top_k(jax.nn.softmax(le, axis=-1), TOP_K)
    wts = pg_top * pe_top / jnp.sum(pe_top, axis=-1, keepdims=True)
    eid = g_sel * EXPERTS_PER_GROUP + e_sel

    a = n * TOP_K
    flat_e = eid.reshape(-1)
    flat_t = jnp.repeat(jnp.arange(n, dtype=jnp.int32), TOP_K)
    flat_w = wts.reshape(-1)
    order = jnp.argsort(flat_e)
    e_s, t_s, w_s = flat_e[order], flat_t[order], flat_w[order]
    counts = jnp.bincount(flat_e, length=N_EXPERTS)
    start = jnp.cumsum(counts) - counts
    padded = (counts + MOE_BLK - 1) // MOE_BLK * MOE_BLK
    pend = jnp.cumsum(padded)
    pstart = pend - padded
    dest = pstart[e_s] + jnp.arange(a, dtype=jnp.int32) - start[e_s]
    cap = -(-a // MOE_BLK) * MOE_BLK + N_EXPERTS * MOE_BLK
    nblk = cap // MOE_BLK
    tok_buf = jnp.zeros((cap,), jnp.int32).at[dest].set(t_s)
    w_buf = jnp.zeros((cap,), t.dtype).at[dest].set(w_s.astype(t.dtype))
    blk_e = jnp.minimum(jnp.searchsorted(pend, jnp.arange(nblk, dtype=jnp.int32) * MOE_BLK, side='right'),
                        N_EXPERTS - 1)
    xb = t[tok_buf].reshape(nblk, MOE_BLK, shp[-1])

    def expert_block(args):
        xe, e = args
        gu = xe @ w_gu[e]
        return (jax.nn.silu(gu[:, :D_EXPERT]) * gu[:, D_EXPERT:]) @ w_dn[e]

    yb = lax.map(expert_block, (xb, blk_e)).reshape(cap, shp[-1])
    out = jnp.zeros_like(t).at[tok_buf].add(yb * w_buf[:, None])
    return out.reshape(shp)


def setup_inputs(seed: int = 0) -> dict:
    key = jax.random.key(seed)
    ks = iter(jax.random.split(key, 40))
    D = D_MODEL
    L = DEPTH

    def nrm(shape, scale):
        return jax.random.normal(next(ks), shape, jnp.float32) * scale

    def gain(shape):
        return 1.0 + nrm(shape, 0.02)

    return {
        'x': nrm((BATCH, SEQ, D), 1.0),
        'c': nrm((BATCH, D), 1.0),
        'ctx': nrm((BATCH, CTX_LEN, D), 1.0),
        'c_ctx': nrm((D,), 1.0),
        'w_ada': nrm((L, D, 6 * D), 0.5 * D ** -0.5),
        'b_ada': nrm((L, 6 * D), 0.02),
        'w_in': nrm((L, D, D_IN), D ** -0.5),
        'sgu_gain': gain((L, BRANCH_W)),
        'w_sgu': nrm((L, SGU_GROUPS, SGU_CHUNK, SGU_CHUNK), SGU_CHUNK ** -0.5),
        'b_sgu': gain((L, SGU_GROUPS, SGU_CHUNK)),
        'mla_q_gain': gain((L, MLA_Q_RANK)),
        'w_uq': nrm((L, MLA_Q_RANK, MLA_HEADS * (MLA_NOPE + MLA_ROPE)), MLA_Q_RANK ** -0.5),
        'mla_kv_gain': gain((L, MLA_KV_RANK)),
        'w_ukv': nrm((L, MLA_KV_RANK, MLA_HEADS * (MLA_NOPE + MLA_V)), MLA_KV_RANK ** -0.5),
        'w_conv': nrm((L, CONV_W, BRANCH_W), CONV_W ** -0.5),
        'gqa_q_gain': gain((L, GQA_HEAD_DIM)),
        'gqa_k_gain': gain((L, GQA_HEAD_DIM)),
        'w_gate': nrm((L, N_BRANCH, D, D), D ** -0.5),
        'b_gate': nrm((L, N_BRANCH, D), 0.02),
        'w_branch': nrm((L, N_BRANCH, BRANCH_W, D), BRANCH_W ** -0.5),
        'w_out': nrm((L, D, D), D ** -0.5),
        'w_group_router': nrm((L, D, N_GROUPS), D ** -0.5),
        'b_group_router': nrm((L, N_GROUPS), 0.01),
        'w_expert_router': nrm((L, D, N_EXPERTS), D ** -0.5),
        'b_expert_router': nrm((L, N_EXPERTS), 0.01),
        'w_expert_gate_up': nrm((L, N_EXPERTS, D, 2 * D_EXPERT), D ** -0.5),
        'w_expert_down': nrm((L, N_EXPERTS, D_EXPERT, D), D_EXPERT ** -0.5),
        'final_gain': gain((D,)),
    }


def reference(x, c, ctx, c_ctx, w_ada, b_ada, w_in, sgu_gain, w_sgu, b_sgu, mla_q_gain, w_uq,
              mla_kv_gain, w_ukv, w_conv, gqa_q_gain, gqa_k_gain, w_gate, b_gate, w_branch, w_out,
              w_group_router, b_group_router, w_expert_router, b_expert_router,
              w_expert_gate_up, w_expert_down, final_gain):
    seq = x.shape[1]
    n_ctx = ctx.shape[1]
    rows = seq // GRID_W
    rope_mla = _axial_rope_tables(n_ctx, rows, MLA_ROPE)
    rope_gqa = _axial_rope_tables(n_ctx, rows, GQA_HEAD_DIM)
    s_lat = jax.nn.silu(c)
    s_ctx = jax.nn.silu(c_ctx)
    hs = jnp.concatenate([ctx, x], axis=1)
    for l in range(DEPTH):
        ml = jnp.split(s_lat @ w_ada[l] + b_ada[l], 6, axis=-1)
        mc = jnp.split(s_ctx @ w_ada[l] + b_ada[l], 6, axis=-1)
        hn = _modulate(_rmsnorm(hs), mc[0], mc[1], ml[0], ml[1], n_ctx)
        y = _mixer(hn, n_ctx, rope_mla, rope_gqa, w_in[l], sgu_gain[l], w_sgu[l], b_sgu[l],
                   mla_q_gain[l], w_uq[l], mla_kv_gain[l], w_ukv[l], w_conv[l], gqa_q_gain[l],
                   gqa_k_gain[l], w_gate[l], b_gate[l], w_branch[l], w_out[l])
        hs = hs + _gated(y, mc[2], ml[2], n_ctx)
        if l == DEPTH - 1:
            hs = hs[:, n_ctx:]
            n_keep = 0
        else:
            n_keep = n_ctx
        hn = _modulate(_rmsnorm(hs), mc[3], mc[4], ml[3], ml[4], n_keep)
        f = _moe(hn, w_group_router[l], b_group_router[l], w_expert_router[l], b_expert_router[l],
                 w_expert_gate_up[l], w_expert_down[l])
        hs = hs + _gated(f, mc[5], ml[5], n_keep)
    return _rmsnorm(hs, final_gain)
```

```python
import functools
import math

import jax
import jax.numpy as jnp
from jax import lax
from jax.experimental import pallas as pl
from jax.experimental.pallas import tpu as pltpu

F32 = jnp.float32
BF16 = jnp.bfloat16

GRID_W = 64
ROPE_THETA = 10000.0
NORM_EPS = 1e-6
BRANCH_W = 256
SGU_CHUNK = 128
SGU_GROUPS = 4
MLA_HEADS = 4
MLA_NOPE = 64
MLA_ROPE = 32
MLA_V = 64
GQA_HEADS = 4
GQA_HEAD_DIM = 64
N_GROUPS = 4
EXPERTS_PER_GROUP = 8
N_EXPERTS = N_GROUPS * EXPERTS_PER_GROUP
D_EXPERT = 256
LANES = 128
TOKEN_TILE = 256
EXPERT_BLK = 256
VMEM_LIMIT = 56 * 1024 * 1024
LOG2E = 1.4426950408889634

C_AU, C_AV, C_MQ, C_MKV, C_KPA, C_KPB, C_CB, C_CC, C_CX, C_GQ, C_GK, C_GV, C_END = (
    0, 256, 512, 768, 896, 1024, 1152, 1408, 1664, 1920, 2176, 2304, 2432)


def _cparams(sem):
    return pltpu.CompilerParams(dimension_semantics=sem, vmem_limit_bytes=VMEM_LIMIT)


def _dot(a, b):
    return jnp.dot(a, b, preferred_element_type=F32)


def _dot_nt(a, b):
    return lax.dot_general(a, b, (((1,), (1,)), ((), ())), preferred_element_type=F32)


def _split_bf16(v):
    hi = v.astype(BF16)
    lo = (v - hi.astype(F32)).astype(BF16)
    return hi, lo


def _gelu(v):
    return 0.5 * v * (1.0 + jnp.tanh(math.sqrt(2.0 / math.pi) * (v + 0.044715 * (v * v * v))))


def _rms(v):
    return v * lax.rsqrt(jnp.mean(v * v, axis=-1, keepdims=True) + NORM_EPS)


def _const_spec(shape):
    nd = len(shape)
    return pl.BlockSpec(shape, lambda *_: (0,) * nd)


def _ada_kernel(c_ref, w_ref, b_ref, o_ref):
    cv = c_ref[...]
    s = cv * (1.0 / (1.0 + jnp.exp(-cv)))
    s_hi, s_lo = _split_bf16(s)
    w_hi, w_lo = _split_bf16(w_ref[...])
    o_ref[...] = _dot(s_hi, w_hi) + _dot(s_lo, w_hi) + _dot(s_hi, w_lo) + b_ref[...]


def _ada_call(cvec, w_ada, b_ada):
    n_layers, d, d6 = w_ada.shape
    rows = cvec.shape[0]
    tn = 1536
    return pl.pallas_call(
        _ada_kernel,
        grid=(n_layers, d6 // tn),
        in_specs=[pl.BlockSpec((rows, d), lambda l, j: (0, 0)),
                  pl.BlockSpec((None, d, tn), lambda l, j: (l, 0, j)),
                  pl.BlockSpec((None, 1, tn), lambda l, j: (l, 0, j))],
        out_specs=pl.BlockSpec((None, rows, tn), lambda l, j: (l, 0, j)),
        out_shape=jax.ShapeDtypeStruct((n_layers, rows, d6), F32),
        compiler_params=_cparams(("arbitrary", "arbitrary")),
        name="ada_mod",
    )(cvec, w_ada, b_ada.reshape(n_layers, 1, d6))


def _in_kernel(hs_ref, mod_ref, cosm_ref, sinm_ref, cosg_ref, sing_ref, w_in_ref, sgu_gain_ref,
               w_sgu_ref, b_sgu_ref, qgain_ref, wq_ref, kvgain_ref, wkv_ref, gqgain_ref, gkgain_ref,
               gmat_ref,
               ya_ref, qm_ref, km_ref, vm_ref, p_ref, cb_ref, qg_ref, kg_ref, vg_ref):
    tm = hs_ref.shape[0]
    mod = mod_ref[...]
    hn = _rms(hs_ref[...]) * (1.0 + mod[1:2, :]) + mod[0:1, :]
    z = _dot(hn.astype(BF16), w_in_ref[...])

    av = _rms(_gelu(z[:, C_AV:C_MQ])) * sgu_gain_ref[...]
    av = av.astype(BF16)
    grp = lax.broadcasted_iota(jnp.int32, (SGU_CHUNK, BRANCH_W), 1) // (BRANCH_W // SGU_GROUPS)
    chunks = []
    for ci in range(tm // SGU_CHUNK):
        vc = av[ci * SGU_CHUNK:(ci + 1) * SGU_CHUNK, :]
        m = b_sgu_ref[...]
        for g in range(SGU_GROUPS):
            m = m + jnp.where(grp == g, _dot(w_sgu_ref[g], vc), 0.0)
        chunks.append(m)
    mixed = jnp.concatenate(chunks, axis=0)
    ya_ref[...] = (_gelu(z[:, C_AU:C_AV]) * mixed).astype(BF16)

    cosm = cosm_ref[...]
    sinm = sinm_ref[...]
    cos4 = jnp.concatenate([cosm] * MLA_HEADS, axis=1)
    sin4 = jnp.concatenate([sinm] * MLA_HEADS, axis=1)
    nq = (_rms(z[:, C_MQ:C_MKV]) * qgain_ref[...]).astype(BF16)
    qab = _dot(nq, wq_ref[...])
    hw = MLA_HEADS * LANES
    q_scale = (MLA_NOPE + MLA_ROPE) ** -0.5 * LOG2E
    qm_ref[...] = ((qab[:, :hw] * cos4 + qab[:, hw:] * sin4) * q_scale).astype(BF16)
    nkv = (_rms(z[:, C_MKV:C_KPA]) * kvgain_ref[...]).astype(BF16)
    kv = _dot(nkv, wkv_ref[...])
    kpe = z[:, C_KPA:C_KPB] * cosm + z[:, C_KPB:C_CB] * sinm
    km_ref[...] = (kv[:, :hw] + jnp.concatenate([kpe] * MLA_HEADS, axis=1)).astype(BF16)
    lane = lax.broadcasted_iota(jnp.int32, (tm, hw), 1)
    ones_col = jnp.where(lane % (2 * LANES) == LANES, 1.0, 0.0)
    vm_ref[...] = (kv[:, hw:] + ones_col).astype(BF16)

    p_ref[...] = (z[:, C_CC:C_CX] * z[:, C_CX:C_GQ]).astype(BF16)
    cb_ref[...] = z[:, C_CB:C_CC].astype(BF16)

    cosg = cosg_ref[...]
    sing = sing_ref[...]
    gmat = gmat_ref[...]

    def head_norm_rope(v, gain, cos_t, sin_t, scale):
        w = v.shape[1]
        sq_hi, sq_lo = _split_bf16(v * v)
        ms = _dot(sq_hi, gmat[:w, :w]) + _dot(sq_lo, gmat[:w, :w])
        y = v * lax.rsqrt(ms + NORM_EPS) * gain
        half = GQA_HEAD_DIM // 2
        ln = lax.broadcasted_iota(jnp.int32, v.shape, 1) % GQA_HEAD_DIM
        partner = jnp.where(ln < half, pltpu.roll(y, w - half, axis=1), pltpu.roll(y, half, axis=1))
        return ((y * cos_t + partner * sin_t) * scale).astype(BF16)

    cos2 = jnp.concatenate([cosg, cosg], axis=1)
    sin2 = jnp.concatenate([sing, sing], axis=1)
    qg_ref[...] = head_norm_rope(z[:, C_GQ:C_GK], gqgain_ref[...], cos2, sin2,
                                 GQA_HEAD_DIM ** -0.5 * LOG2E)
    kg_ref[...] = head_norm_rope(z[:, C_GK:C_GV], gkgain_ref[...], cosg, sing, 1.0)
    lane2 = lax.broadcasted_iota(jnp.int32, (tm, LANES), 1)
    vg_ref[...] = jnp.concatenate([z[:, C_GV:C_END], jnp.where(lane2 == 0, 1.0, 0.0)], axis=1).astype(BF16)


def _mod_row_map(tm, n_tok, n_ctx, n_batch):
    def index_map(i):
        start = i * tm
        return (jnp.where(start % n_tok < n_ctx, n_batch, start // n_tok), 0, 0)
    return index_map


def _in_call(hs, mod, tabs, lw, dims):
    n_batch, n_tok, n_ctx = dims
    t, d = hs.shape
    tm = TOKEN_TILE
    tiles_per_seq = n_tok // tm
    row = lambda w: pl.BlockSpec((tm, w), lambda i: (i, 0))
    tab = pl.BlockSpec((tm, LANES), lambda i: (i % tiles_per_seq, 0))
    hw = MLA_HEADS * LANES
    outs = [(BRANCH_W, "ya"), (hw, "qm"), (hw, "km"), (hw, "vm"), (BRANCH_W, "p"), (BRANCH_W, "cb"),
            (BRANCH_W, "qg"), (LANES, "kg"), (2 * LANES, "vg")]
    return pl.pallas_call(
        _in_kernel,
        grid=(t // tm,),
        in_specs=[row(d),
                  pl.BlockSpec((None, 6, d), _mod_row_map(tm, n_tok, n_ctx, n_batch)),
                  tab, tab, tab, tab,
                  _const_spec((d, C_END)), _const_spec((1, BRANCH_W)),
                  _const_spec((SGU_GROUPS, SGU_CHUNK, SGU_CHUNK)), _const_spec((SGU_CHUNK, BRANCH_W)),
                  _const_spec((1, BRANCH_W)), _const_spec((BRANCH_W, 2 * hw)),
                  _const_spec((1, LANES)), _const_spec((LANES, 2 * hw)),
                  _const_spec((1, BRANCH_W)), _const_spec((1, LANES)),
                  _const_spec((BRANCH_W, BRANCH_W))],
        out_specs=[row(w) for w, _ in outs],
        out_shape=[jax.ShapeDtypeStruct((t, w), BF16) for w, _ in outs],
        compiler_params=_cparams(("arbitrary",)),
        name="in_proj",
    )(hs, mod, tabs["cosm"], tabs["sinm"], tabs["cosg"], tabs["sing"], lw["w_in"], lw["sgu_gain"],
      lw["w_sgu"], lw["b_sgu"], lw["q_gain"], lw["wq"], lw["kv_gain"], lw["wkv"], lw["gq_gain"],
      lw["gk_gain"], tabs["gmat"])


def _softmax_pv(s, v):
    m = jnp.max(s, axis=-1, keepdims=True)
    p = jnp.exp2(s - m).astype(BF16)
    r = _dot(p, v)
    return r[:, :LANES] * (1.0 / r[:, LANES:LANES + 1])


def _mla_attn_kernel(q_ref, k_ref, v_ref, o_ref, *, ctx_tiles, n_ctx):
    def run(nk):
        lane = lax.broadcasted_iota(jnp.int32, (q_ref.shape[0], LANES), 1)
        outs = []
        for pair in range(MLA_HEADS // 2):
            v = v_ref[0:nk, pair * 2 * LANES:(pair + 1) * 2 * LANES]
            res = []
            for hh in range(2):
                h = pair * 2 + hh
                s = _dot_nt(q_ref[:, h * LANES:(h + 1) * LANES], k_ref[0:nk, h * LANES:(h + 1) * LANES])
                res.append(_softmax_pv(s, v))
            outs.append(jnp.where(lane < MLA_V, res[0], res[1]))
        o_ref[...] = jnp.concatenate(outs, axis=1).astype(BF16)

    j = pl.program_id(1)
    pl.when(j < ctx_tiles)(lambda: run(n_ctx))
    pl.when(j >= ctx_tiles)(lambda: run(k_ref.shape[0]))


def _gqa_attn_kernel(q_ref, k_ref, v_ref, o_ref, *, ctx_tiles, n_ctx):
    def run(nk):
        lane = lax.broadcasted_iota(jnp.int32, (q_ref.shape[0], LANES), 1)
        k = k_ref[0:nk, :]
        v = v_ref[0:nk, :]
        outs = []
        for grp in range(GQA_HEADS // 2):
            q = q_ref[:, grp * LANES:(grp + 1) * LANES]
            zero = jnp.zeros_like(q)
            r_lo = _softmax_pv(_dot_nt(jnp.where(lane < GQA_HEAD_DIM, q, zero), k), v)
            r_hi = _softmax_pv(_dot_nt(jnp.where(lane >= GQA_HEAD_DIM, q, zero), k), v)
            outs.append(jnp.where(lane < GQA_HEAD_DIM, r_lo, r_hi))
        o_ref[...] = jnp.concatenate(outs, axis=1).astype(BF16)

    j = pl.program_id(1)
    pl.when(j < ctx_tiles)(lambda: run(n_ctx))
    pl.when(j >= ctx_tiles)(lambda: run(k_ref.shape[0]))


def _attn_call(body, name, q, k, v, dims):
    n_batch, n_tok, n_ctx = dims
    tq = TOKEN_TILE
    tiles = n_tok // tq
    return pl.pallas_call(
        functools.partial(body, ctx_tiles=n_ctx // tq, n_ctx=n_ctx),
        grid=(n_batch, tiles),
        in_specs=[pl.BlockSpec((tq, q.shape[1]), lambda b, j: (b * tiles + j, 0)),
                  pl.BlockSpec((n_tok, k.shape[1]), lambda b, j: (b, 0)),
                  pl.BlockSpec((n_tok, v.shape[1]), lambda b, j: (b, 0))],
        out_specs=pl.BlockSpec((tq, BRANCH_W), lambda b, j: (b * tiles + j, 0)),
        out_shape=jax.ShapeDtypeStruct((q.shape[0], BRANCH_W), BF16),
        compiler_params=_cparams(("arbitrary", "arbitrary")),
        name=name,
    )(q, k, v)


def _merge_kernel(hs_ref, mod_ref, ya_ref, yb_ref, yd_ref, p_ref, pprev_ref, pnext_ref, cb_ref,
                  wconv_ref, wgate_ref, bgate_ref, wbranch_ref, wout_ref, wr_hi_ref, wr_lo_ref, br_ref,
                  hs1_ref, hn2_ref, eid_ref, wts_ref, *, n_tok, n_ctx):
    tm = hs_ref.shape[0]
    mod = mod_ref[...]
    hs = hs_ref[...]
    hn = (_rms(hs) * (1.0 + mod[1:2, :]) + mod[0:1, :]).astype(BF16)

    start = (pl.program_id(0) * tm) % n_tok
    prev_ok = jnp.logical_and(start != 0, start != n_ctx)
    next_ok = jnp.logical_and(start + tm != n_ctx, start + tm != n_tok)
    p = p_ref[...].astype(F32)
    halo_prev = jnp.where(prev_ok, pprev_ref[...].astype(F32)[15:16, :], 0.0)
    halo_next = jnp.where(next_ok, pnext_ref[...].astype(F32)[0:1, :], 0.0)
    rowi = lax.broadcasted_iota(jnp.int32, p.shape, 0)
    prev = jnp.where(rowi == 0, halo_prev, pltpu.roll(p, 1, axis=0))
    nxt = jnp.where(rowi == tm - 1, halo_next, pltpu.roll(p, tm - 1, axis=0))
    wc = wconv_ref[...]
    yc = cb_ref[...].astype(F32) * (wc[0:1, :] * prev + wc[1:2, :] * p + wc[2:3, :] * nxt)

    branches = (ya_ref[...], yb_ref[...], yc.astype(BF16), yd_ref[...])
    merged = None
    for i, y in enumerate(branches):
        gate = _dot(hn, wgate_ref[i]) + bgate_ref[i:i + 1, :]
        term = (1.0 / (1.0 + jnp.exp(-gate))) * _dot(y, wbranch_ref[i])
        merged = term if merged is None else merged + term
    hs1 = hs + _dot(merged.astype(BF16), wout_ref[...]) * mod[2:3, :]
    hs1_ref[...] = hs1

    hn2 = _rms(hs1) * (1.0 + mod[4:5, :]) + mod[3:4, :]
    hn2_ref[...] = hn2

    x_hi, x_lo = _split_bf16(hn2)
    logits = (_dot(x_hi, wr_hi_ref[...]) + _dot(x_lo, wr_hi_ref[...]) + _dot(x_hi, wr_lo_ref[...])
              + br_ref[...])
    lane = lax.broadcasted_iota(jnp.int32, logits.shape, 1)
    neg = -jnp.inf
    big = jnp.int32(1 << 20)
    gl = jnp.where(lane < N_GROUPS, logits, neg)
    gmax = jnp.max(gl, axis=-1, keepdims=True)
    ge = jnp.exp(gl - gmax)
    pg = ge / jnp.sum(ge, axis=-1, keepdims=True)
    pg_top = jnp.max(pg, axis=-1, keepdims=True)
    g_sel = jnp.min(jnp.where(pg == pg_top, lane, big), axis=-1, keepdims=True)
    eidx = lane - N_GROUPS
    in_group = jnp.logical_and(eidx >= g_sel * EXPERTS_PER_GROUP, eidx < (g_sel + 1) * EXPERTS_PER_GROUP)
    le = jnp.where(in_group, logits, neg)
    emax = jnp.max(le, axis=-1, keepdims=True)
    ee = jnp.exp(le - emax)
    pe = ee / jnp.sum(ee, axis=-1, keepdims=True)
    p1 = jnp.max(pe, axis=-1, keepdims=True)
    i1 = jnp.min(jnp.where(jnp.logical_and(in_group, pe == p1), lane, big), axis=-1, keepdims=True)
    rest = jnp.logical_and(in_group, lane != i1)
    pe2 = jnp.where(rest, pe, neg)
    p2 = jnp.max(pe2, axis=-1, keepdims=True)
    i2 = jnp.min(jnp.where(jnp.logical_and(rest, pe2 == p2), lane, big), axis=-1, keepdims=True)
    denom = p1 + p2
    eid_ref[...] = jnp.where(lane == 0, i1 - N_GROUPS, jnp.where(lane == 1, i2 - N_GROUPS, 0))
    wts_ref[...] = jnp.where(lane == 0, pg_top * p1 / denom, jnp.where(lane == 1, pg_top * p2 / denom, 0.0))


def _merge_call(hs, mod, ya, yb, yd, p, cb, lw, dims):
    n_batch, n_tok, n_ctx = dims
    t, d = hs.shape
    tm = TOKEN_TILE
    sub = 16
    n_sub = t // sub
    row = lambda w: pl.BlockSpec((tm, w), lambda i: (i, 0))
    return pl.pallas_call(
        functools.partial(_merge_kernel, n_tok=n_tok, n_ctx=n_ctx),
        grid=(t // tm,),
        in_specs=[row(d),
                  pl.BlockSpec((None, 6, d), _mod_row_map(tm, n_tok, n_ctx, n_batch)),
                  row(BRANCH_W), row(BRANCH_W), row(BRANCH_W), row(BRANCH_W),
                  pl.BlockSpec((sub, BRANCH_W), lambda i: (jnp.maximum(i * (tm // sub) - 1, 0), 0)),
                  pl.BlockSpec((sub, BRANCH_W), lambda i: (jnp.minimum((i + 1) * (tm // sub), n_sub - 1), 0)),
                  row(BRANCH_W),
                  _const_spec((3, BRANCH_W)), _const_spec((4, d, d)), _const_spec((4, d)),
                  _const_spec((4, BRANCH_W, d)), _const_spec((d, d)),
                  _const_spec((d, LANES)), _const_spec((d, LANES)), _const_spec((1, LANES))],
        out_specs=[row(d), row(d), row(LANES), row(LANES)],
        out_shape=[jax.ShapeDtypeStruct((t, d), F32), jax.ShapeDtypeStruct((t, d), F32),
                   jax.ShapeDtypeStruct((t, LANES), jnp.int32), jax.ShapeDtypeStruct((t, LANES), F32)],
        compiler_params=_cparams(("arbitrary",)),
        name="merge_route",
    )(hs, mod, ya, yb, yd, p, p, p, cb, lw["w_conv"], lw["w_gate"], lw["b_gate"], lw["w_branch"],
      lw["w_out"], lw["wr_hi"], lw["wr_lo"], lw["br"])


def _row_gather(src_hbm, idx_ref, base, dst, sem, n_rows):
    def body(r, carry):
        tok = idx_ref[base + r]
        pltpu.make_async_copy(src_hbm.at[pl.ds(tok, 1)], dst.at[pl.ds(r, 1)], sem).start()
        return carry
    lax.fori_loop(0, n_rows, body, 0, unroll=8)


def _gather_wait(dst, sem):
    pltpu.make_async_copy(dst, dst, sem).wait()


def _expert_kernel(tok_ref, blk_e_ref, n_used_ref, x_hbm, wgu_ref, wdn_ref, y_ref, xbuf, sems):
    i = pl.program_id(0)
    blk = y_ref.shape[0]
    n_used = n_used_ref[0]
    slot = i % 2

    @pl.when(i == 0)
    def _():
        _row_gather(x_hbm, tok_ref, 0, xbuf.at[0], sems.at[0], blk)

    @pl.when(i + 1 < n_used)
    def _():
        _row_gather(x_hbm, tok_ref, (i + 1) * blk, xbuf.at[1 - slot], sems.at[1 - slot], blk)

    @pl.when(i < n_used)
    def _():
        _gather_wait(xbuf.at[slot], sems.at[slot])
        xe = xbuf[slot].astype(BF16)
        gu = _dot(xe, wgu_ref[...])
        g = gu[:, :D_EXPERT]
        act = (g * (1.0 / (1.0 + jnp.exp(-g))) * gu[:, D_EXPERT:]).astype(BF16)
        y_ref[...] = _dot(act, wdn_ref[...])

    @pl.when(i >= n_used)
    def _():
        y_ref[...] = jnp.zeros_like(y_ref)


def _expert_call(tok_buf, blk_e, n_used, x, w_gu, w_dn):
    cap = tok_buf.shape[0]
    d = x.shape[1]
    blk = EXPERT_BLK
    grid_spec = pltpu.PrefetchScalarGridSpec(
        num_scalar_prefetch=3,
        grid=(cap // blk,),
        in_specs=[pl.BlockSpec(memory_space=pl.ANY),
                  pl.BlockSpec((None, d, 2 * D_EXPERT), lambda i, tok, be, nu: (be[i], 0, 0)),
                  pl.BlockSpec((None, D_EXPERT, d), lambda i, tok, be, nu: (be[i], 0, 0))],
        out_specs=pl.BlockSpec((blk, d), lambda i, tok, be, nu: (i, 0)),
        scratch_shapes=[pltpu.VMEM((2, blk, d), F32), pltpu.SemaphoreType.DMA((2,))],
    )
    return pl.pallas_call(
        _expert_kernel,
        grid_spec=grid_spec,
        out_shape=jax.ShapeDtypeStruct((cap, d), F32),
        compiler_params=_cparams(("arbitrary",)),
        name="experts",
    )(tok_buf, blk_e, n_used, x, w_gu, w_dn)


def _combine_kernel(pos_ref, y_hbm, hs1_ref, wts_ref, mod_ref, gain_ref, o_ref, gbuf, sems, *,
                    tile_map, final):
    i = pl.program_id(0)
    n = pl.num_programs(0)
    tm = hs1_ref.shape[0]
    slot = i % 2

    def start(step, s):
        _row_gather(y_hbm, pos_ref, tile_map(step) * (2 * tm), gbuf.at[s], sems.at[s], 2 * tm)

    @pl.when(i == 0)
    def _():
        start(0, 0)

    @pl.when(i + 1 < n)
    def _():
        start(i + 1, 1 - slot)

    _gather_wait(gbuf.at[slot], sems.at[slot])
    wts = wts_ref[...]
    f = wts[:, 0:1] * gbuf[slot, 0:tm, :] + wts[:, 1:2] * gbuf[slot, tm:2 * tm, :]
    out = hs1_ref[...] + f * mod_ref[...][5:6, :]
    if final:
        out = _rms(out) * gain_ref[...]
    o_ref[...] = out


def _combine_call(pos, y, hs1, wts, mod, gain, dims, final):
    n_batch, n_tok, n_ctx = dims
    t, d = hs1.shape
    tm = TOKEN_TILE
    tiles = n_tok // tm
    ctx_tiles = n_ctx // tm
    if final:
        lat = tiles - ctx_tiles
        tile_map = lambda i: (i // lat) * tiles + ctx_tiles + i % lat
        n_steps = n_batch * lat
    else:
        tile_map = lambda i: i
        n_steps = t // tm
    mod_map = _mod_row_map(tm, n_tok, n_ctx, n_batch)
    grid_spec = pltpu.PrefetchScalarGridSpec(
        num_scalar_prefetch=1,
        grid=(n_steps,),
        in_specs=[pl.BlockSpec(memory_space=pl.ANY),
                  pl.BlockSpec((tm, d), lambda i, pos: (tile_map(i), 0)),
                  pl.BlockSpec((tm, LANES), lambda i, pos: (tile_map(i), 0)),
                  pl.BlockSpec((None, 6, d), lambda i, pos: mod_map(tile_map(i))),
                  pl.BlockSpec((1, d), lambda i, pos: (0, 0))],
        out_specs=pl.BlockSpec((tm, d), lambda i, pos: (i, 0)),
        scratch_shapes=[pltpu.VMEM((2, 2 * tm, d), F32), pltpu.SemaphoreType.DMA((2,))],
    )
    return pl.pallas_call(
        functools.partial(_combine_kernel, tile_map=tile_map, final=final),
        grid_spec=grid_spec,
        out_shape=jax.ShapeDtypeStruct((n_steps * tm, d), F32),
        compiler_params=_cparams(("arbitrary",)),
        name="combine_final" if final else "combine",
    )(pos, y, hs1, wts, mod, gain)


def _rope_angles(n_ctx, rows, rot_dim):
    n_freq = rot_dim // 4
    freqs = ROPE_THETA ** (-jnp.arange(n_freq, dtype=F32) / n_freq)
    row = jnp.repeat(jnp.arange(rows, dtype=F32), GRID_W)
    col = (jnp.arange(rows * GRID_W) % GRID_W).astype(F32)
    ang = jnp.concatenate([row[:, None] * freqs, col[:, None] * freqs], axis=-1)
    ang = jnp.concatenate([jnp.zeros((n_ctx, rot_dim // 2), F32), ang], axis=0)
    return jnp.cos(ang), jnp.sin(ang)


def _tables(n_ctx, seq):
    n = n_ctx + seq
    cm, sm = _rope_angles(n_ctx, seq // GRID_W, MLA_ROPE)
    cg, sg = _rope_angles(n_ctx, seq // GRID_W, GQA_HEAD_DIM)
    one, zero = jnp.ones((n, MLA_NOPE), F32), jnp.zeros((n, MLA_NOPE), F32)
    pad = jnp.zeros((n, LANES - MLA_NOPE - MLA_ROPE), F32)
    gi = jnp.arange(BRANCH_W) // GQA_HEAD_DIM
    return {
        "cosm": jnp.concatenate([one, cm, cm, pad], axis=1),
        "sinm": jnp.concatenate([zero, -sm, sm, pad], axis=1),
        "cosg": jnp.concatenate([cg, cg, cg, cg], axis=1),
        "sing": jnp.concatenate([-sg, sg, -sg, sg], axis=1),
        "gmat": jnp.where(gi[:, None] == gi[None, :], 1.0 / GQA_HEAD_DIM, 0.0).astype(BF16),
    }


def _prep_weights(w_in, sgu_gain, w_sgu, b_sgu, mla_q_gain, w_uq, mla_kv_gain, w_ukv, w_conv,
                  gqa_q_gain, gqa_k_gain, w_gate, b_gate, w_branch, w_out, w_group_router,
                  b_group_router, w_expert_router, b_expert_router, w_expert_gate_up, w_expert_down):
    nl, d, _ = w_in.shape
    zc = lambda n: jnp.zeros((nl, d, n), F32)
    half = MLA_ROPE // 2
    kpe = w_in[..., 896:928]
    gq = w_in[..., 1696:1952].reshape(nl, d, GQA_HEADS, GQA_HEAD_DIM)[:, :, (0, 2, 1, 3), :].reshape(nl, d, BRANCH_W)
    w_in_ext = jnp.concatenate([
        w_in[..., 0:896],
        zc(MLA_NOPE), kpe, zc(LANES - MLA_NOPE - MLA_ROPE),
        zc(MLA_NOPE), kpe[..., half:], kpe[..., :half], zc(LANES - MLA_NOPE - MLA_ROPE),
        w_in[..., 928:1696], gq, w_in[..., 1952:2208]], axis=-1).astype(BF16)

    dq = MLA_NOPE + MLA_ROPE
    uq = w_uq.reshape(nl, -1, MLA_HEADS, dq)
    zq = jnp.zeros(uq.shape[:3] + (LANES - dq,), F32)
    wq_a = jnp.concatenate([uq, zq], axis=-1)
    wq_b = jnp.concatenate([jnp.zeros_like(uq[..., :MLA_NOPE]), uq[..., MLA_NOPE + half:],
                            uq[..., MLA_NOPE:MLA_NOPE + half], zq], axis=-1)
    wq = jnp.concatenate([wq_a.reshape(nl, -1, MLA_HEADS * LANES),
                          wq_b.reshape(nl, -1, MLA_HEADS * LANES)], axis=-1).astype(BF16)

    ukv = w_ukv.reshape(nl, -1, MLA_HEADS, MLA_NOPE + MLA_V)
    k_lay = jnp.concatenate([ukv[..., :MLA_NOPE], jnp.zeros_like(ukv[..., :LANES - MLA_NOPE])], axis=-1)
    v_h = ukv[..., MLA_NOPE:]
    zv = jnp.zeros(v_h.shape[:2] + (LANES,), F32)
    v_lay = jnp.concatenate([v_h[:, :, 0], v_h[:, :, 1], zv, v_h[:, :, 2], v_h[:, :, 3], zv], axis=-1)
    wkv = jnp.concatenate([k_lay.reshape(nl, -1, MLA_HEADS * LANES), v_lay], axis=-1).astype(BF16)

    wb = w_branch
    wb_d = wb[:, 3].reshape(nl, GQA_HEADS, GQA_HEAD_DIM, d)[:, (0, 2, 1, 3)].reshape(nl, BRANCH_W, d)
    w_branch_p = jnp.concatenate([wb[:, :3], wb_d[:, None]], axis=1).astype(BF16)

    wr = jnp.concatenate([w_group_router, w_expert_router,
                          jnp.zeros((nl, d, LANES - N_GROUPS - N_EXPERTS), F32)], axis=-1)
    wr_hi = wr.astype(BF16)
    wr_lo = (wr - wr_hi.astype(F32)).astype(BF16)
    br = jnp.concatenate([b_group_router, b_expert_router,
                          jnp.zeros((nl, LANES - N_GROUPS - N_EXPERTS), F32)], axis=-1)[:, None, :]

    tile = lambda g, reps: jnp.concatenate([g] * reps, axis=-1)[:, None, :]
    return {
        "w_in": w_in_ext,
        "sgu_gain": sgu_gain[:, None, :],
        "w_sgu": w_sgu.astype(BF16),
        "b_sgu": jnp.repeat(b_sgu.transpose(0, 2, 1), BRANCH_W // SGU_GROUPS, axis=-1),
        "q_gain": mla_q_gain[:, None, :], "wq": wq,
        "kv_gain": mla_kv_gain[:, None, :], "wkv": wkv,
        "gq_gain": tile(gqa_q_gain, 4), "gk_gain": tile(gqa_k_gain, 2),
        "w_conv": w_conv,
        "w_gate": w_gate.astype(BF16), "b_gate": b_gate,
        "w_branch": w_branch_p, "w_out": w_out.astype(BF16),
        "wr_hi": wr_hi, "wr_lo": wr_lo, "br": br,
        "w_gu": w_expert_gate_up.astype(BF16), "w_dn": w_expert_down.astype(BF16),
    }


def _dispatch_plan(eid, n_assign):
    blk = EXPERT_BLK
    flat_e = eid.reshape(-1)
    onehot = (flat_e[:, None] == jnp.arange(N_EXPERTS, dtype=jnp.int32)[None, :]).astype(jnp.int32)
    csum = jnp.cumsum(onehot, axis=0)
    rank = jnp.sum(onehot * csum, axis=1) - 1
    counts = csum[-1]
    padded = (counts + blk - 1) // blk * blk
    pend = jnp.cumsum(padded)
    pstart = pend - padded
    dest = (pstart[flat_e] + rank).astype(jnp.int32)
    cap = -(-n_assign // blk) * blk + N_EXPERTS * blk
    nblk = cap // blk
    tok_buf = jnp.zeros((cap,), jnp.int32).at[dest].set(jnp.arange(n_assign, dtype=jnp.int32) // 2)
    n_used = (pend[-1] // blk).astype(jnp.int32)
    blk_idx = jnp.minimum(jnp.arange(nblk, dtype=jnp.int32), n_used - 1)
    blk_e = jnp.minimum(jnp.searchsorted(pend, blk_idx * blk, side="right"), N_EXPERTS - 1).astype(jnp.int32)
    return dest, tok_buf, blk_e, n_used.reshape(1)


def kernel(x, c, ctx, c_ctx, w_ada, b_ada, w_in, sgu_gain, w_sgu, b_sgu, mla_q_gain, w_uq, mla_kv_gain, w_ukv, w_conv, gqa_q_gain, gqa_k_gain, w_gate, b_gate, w_branch, w_out, w_group_router, b_group_router, w_expert_router, b_expert_router, w_expert_gate_up, w_expert_down, final_gain):
    n_batch, seq, d = x.shape
    n_ctx = ctx.shape[1]
    n_tok = n_ctx + seq
    t = n_batch * n_tok
    depth = w_in.shape[0]
    tm = TOKEN_TILE
    assert n_ctx % tm == 0 and seq % tm == 0 and seq % GRID_W == 0 and tm % SGU_CHUNK == 0
    dims = (n_batch, n_tok, n_ctx)

    hs = jnp.concatenate([ctx, x], axis=1).reshape(t, d)
    tabs = _tables(n_ctx, seq)
    rows = -(-(n_batch + 1) // 8) * 8
    cvec = jnp.concatenate([c, c_ctx[None, :], jnp.zeros((rows - n_batch - 1, d), F32)], axis=0)
    mods = _ada_call(cvec, w_ada, b_ada).reshape(depth, rows, 6, d)
    weights = _prep_weights(w_in, sgu_gain, w_sgu, b_sgu, mla_q_gain, w_uq, mla_kv_gain, w_ukv, w_conv,
                            gqa_q_gain, gqa_k_gain, w_gate, b_gate, w_branch, w_out, w_group_router,
                            b_group_router, w_expert_router, b_expert_router, w_expert_gate_up,
                            w_expert_down)
    gain = final_gain[None, :]

    for l in range(depth):
        lw = {k: v[l] for k, v in weights.items()}
        mod = mods[l]
        ya, qm, km, vm, p, cb, qg, kg, vg = _in_call(hs, mod, tabs, lw, dims)
        yb = _attn_call(_mla_attn_kernel, "mla_attn", qm, km, vm, dims)
        yd = _attn_call(_gqa_attn_kernel, "gqa_attn", qg, kg, vg, dims)
        hs1, hn2, eid, wts = _merge_call(hs, mod, ya, yb, yd, p, cb, lw, dims)
        dest, tok_buf, blk_e, n_used = _dispatch_plan(eid[:, :2], 2 * t)
        y = _expert_call(tok_buf, blk_e, n_used, hn2, lw["w_gu"], lw["w_dn"])
        pos = dest.reshape(t // tm, tm, 2).transpose(0, 2, 1).reshape(-1)
        hs = _combine_call(pos, y, hs1, wts, mod, gain, dims, final=(l == depth - 1))
    return hs.reshape(n_batch, seq, d)
```

```python
import functools
import math

import jax
import jax.numpy as jnp
from jax import lax
from jax.experimental import pallas as pl
from jax.experimental.pallas import tpu as pltpu

F32 = jnp.float32
BF16 = jnp.bfloat16

GRID_W = 64
ROPE_THETA = 10000.0
NORM_EPS = 1e-6
BRANCH_W = 256
SGU_CHUNK = 128
SGU_GROUPS = 4
MLA_HEADS = 4
MLA_NOPE = 64
MLA_ROPE = 32
MLA_V = 64
GQA_HEADS = 4
GQA_HEAD_DIM = 64
N_GROUPS = 4
EXPERTS_PER_GROUP = 8
N_EXPERTS = N_GROUPS * EXPERTS_PER_GROUP
D_EXPERT = 256
LANES = 128
ROW_SUB = 8
TOKEN_TILE = 256
EXPERT_BLK = 256
VMEM_LIMIT = 56 * 1024 * 1024
LOG2E = 1.4426950408889634

C_AU, C_AV, C_MQ, C_MKV, C_KPA, C_KPB, C_CB, C_CC, C_CX, C_GQ, C_GK, C_GV, C_END = (
    0, 256, 512, 768, 896, 1024, 1152, 1408, 1664, 1920, 2176, 2304, 2432)


def _cparams(sem):
    return pltpu.CompilerParams(dimension_semantics=sem, vmem_limit_bytes=VMEM_LIMIT)


def _dot(a, b):
    return jnp.dot(a, b, preferred_element_type=F32)


def _dot_nt(a, b):
    return lax.dot_general(a, b, (((1,), (1,)), ((), ())), preferred_element_type=F32)


def _split_bf16(v):
    hi = v.astype(BF16)
    lo = (v - hi.astype(F32)).astype(BF16)
    return hi, lo


def _gelu(v):
    return 0.5 * v * (1.0 + jnp.tanh(math.sqrt(2.0 / math.pi) * (v + 0.044715 * (v * v * v))))


def _rms(v):
    return v * lax.rsqrt(jnp.mean(v * v, axis=-1, keepdims=True) + NORM_EPS)


def _const_spec(shape):
    nd = len(shape)
    return pl.BlockSpec(shape, lambda *_: (0,) * nd)


def _ada_kernel(c_ref, w_ref, b_ref, o_ref):
    cv = c_ref[...]
    s = cv * (1.0 / (1.0 + jnp.exp(-cv)))
    s_hi, s_lo = _split_bf16(s)
    w_hi, w_lo = _split_bf16(w_ref[...])
    o_ref[...] = _dot(s_hi, w_hi) + _dot(s_lo, w_hi) + _dot(s_hi, w_lo) + b_ref[...]


def _ada_call(cvec, w_ada, b_ada):
    n_layers, d, d6 = w_ada.shape
    rows = cvec.shape[0]
    tn = 1536
    return pl.pallas_call(
        _ada_kernel,
        grid=(n_layers, d6 // tn),
        in_specs=[pl.BlockSpec((rows, d), lambda l, j: (0, 0)),
                  pl.BlockSpec((None, d, tn), lambda l, j: (l, 0, j)),
                  pl.BlockSpec((None, 1, tn), lambda l, j: (l, 0, j))],
        out_specs=pl.BlockSpec((None, rows, tn), lambda l, j: (l, 0, j)),
        out_shape=jax.ShapeDtypeStruct((n_layers, rows, d6), F32),
        compiler_params=_cparams(("arbitrary", "arbitrary")),
        name="ada_mod",
    )(cvec, w_ada, b_ada.reshape(n_layers, 1, d6))


def _in_kernel(hs_ref, mod_ref, cosm_ref, sinm_ref, cosg_ref, sing_ref, w_in_ref, sgu_gain_ref,
               w_sgu_ref, b_sgu_ref, qgain_ref, wq_ref, kvgain_ref, wkv_ref, gqgain_ref, gkgain_ref,
               gmat_ref,
               ya_ref, qm_ref, km_ref, vm_ref, p_ref, cb_ref, qg_ref, kg_ref, vg_ref):
    tm = hs_ref.shape[0]
    mod = mod_ref[...]
    hn = _rms(hs_ref[...]) * (1.0 + mod[1:2, :]) + mod[0:1, :]
    z = _dot(hn.astype(BF16), w_in_ref[...])

    av = _rms(_gelu(z[:, C_AV:C_MQ])) * sgu_gain_ref[...]
    av = av.astype(BF16)
    grp = lax.broadcasted_iota(jnp.int32, (SGU_CHUNK, BRANCH_W), 1) // (BRANCH_W // SGU_GROUPS)
    chunks = []
    for ci in range(tm // SGU_CHUNK):
        vc = av[ci * SGU_CHUNK:(ci + 1) * SGU_CHUNK, :]
        m = b_sgu_ref[...]
        for g in range(SGU_GROUPS):
            m = m + jnp.where(grp == g, _dot(w_sgu_ref[g], vc), 0.0)
        chunks.append(m)
    mixed = jnp.concatenate(chunks, axis=0)
    ya_ref[...] = (_gelu(z[:, C_AU:C_AV]) * mixed).astype(BF16)

    cosm = cosm_ref[...]
    sinm = sinm_ref[...]
    cos4 = jnp.concatenate([cosm] * MLA_HEADS, axis=1)
    sin4 = jnp.concatenate([sinm] * MLA_HEADS, axis=1)
    nq = (_rms(z[:, C_MQ:C_MKV]) * qgain_ref[...]).astype(BF16)
    qab = _dot(nq, wq_ref[...])
    hw = MLA_HEADS * LANES
    q_scale = (MLA_NOPE + MLA_ROPE) ** -0.5 * LOG2E
    qm_ref[...] = ((qab[:, :hw] * cos4 + qab[:, hw:] * sin4) * q_scale).astype(BF16)
    nkv = (_rms(z[:, C_MKV:C_KPA]) * kvgain_ref[...]).astype(BF16)
    kv = _dot(nkv, wkv_ref[...])
    kpe = z[:, C_KPA:C_KPB] * cosm + z[:, C_KPB:C_CB] * sinm
    km_ref[...] = (kv[:, :hw] + jnp.concatenate([kpe] * MLA_HEADS, axis=1)).astype(BF16)
    lane = lax.broadcasted_iota(jnp.int32, (tm, hw), 1)
    ones_col = jnp.where(lane % (2 * LANES) == LANES, 1.0, 0.0)
    vm_ref[...] = (kv[:, hw:] + ones_col).astype(BF16)

    p_ref[...] = (z[:, C_CC:C_CX] * z[:, C_CX:C_GQ]).astype(BF16)
    cb_ref[...] = z[:, C_CB:C_CC].astype(BF16)

    cosg = cosg_ref[...]
    sing = sing_ref[...]
    gmat = gmat_ref[...]

    def head_norm_rope(v, gain, cos_t, sin_t, scale):
        w = v.shape[1]
        sq_hi, sq_lo = _split_bf16(v * v)
        ms = _dot(sq_hi, gmat[:w, :w]) + _dot(sq_lo, gmat[:w, :w])
        y = v * lax.rsqrt(ms + NORM_EPS) * gain
        half = GQA_HEAD_DIM // 2
        ln = lax.broadcasted_iota(jnp.int32, v.shape, 1) % GQA_HEAD_DIM
        partner = jnp.where(ln < half, pltpu.roll(y, w - half, axis=1), pltpu.roll(y, half, axis=1))
        return ((y * cos_t + partner * sin_t) * scale).astype(BF16)

    cos2 = jnp.concatenate([cosg, cosg], axis=1)
    sin2 = jnp.concatenate([sing, sing], axis=1)
    qg_ref[...] = head_norm_rope(z[:, C_GQ:C_GK], gqgain_ref[...], cos2, sin2,
                                 GQA_HEAD_DIM ** -0.5 * LOG2E)
    kg_ref[...] = head_norm_rope(z[:, C_GK:C_GV], gkgain_ref[...], cosg, sing, 1.0)
    lane2 = lax.broadcasted_iota(jnp.int32, (tm, LANES), 1)
    vg_ref[...] = jnp.concatenate([z[:, C_GV:C_END], jnp.where(lane2 == 0, 1.0, 0.0)], axis=1).astype(BF16)


def _mod_row_map(tm, n_tok, n_ctx, n_batch):
    def index_map(i):
        start = i * tm
        return (jnp.where(start % n_tok < n_ctx, n_batch, start // n_tok), 0, 0)
    return index_map


def _in_call(hs, mod, tabs, lw, dims):
    n_batch, n_tok, n_ctx = dims
    t, d = hs.shape
    tm = TOKEN_TILE
    tiles_per_seq = n_tok // tm
    row = lambda w: pl.BlockSpec((tm, w), lambda i: (i, 0))
    tab = pl.BlockSpec((tm, LANES), lambda i: (i % tiles_per_seq, 0))
    hw = MLA_HEADS * LANES
    outs = [(BRANCH_W, "ya"), (hw, "qm"), (hw, "km"), (hw, "vm"), (BRANCH_W, "p"), (BRANCH_W, "cb"),
            (BRANCH_W, "qg"), (LANES, "kg"), (2 * LANES, "vg")]
    return pl.pallas_call(
        _in_kernel,
        grid=(t // tm,),
        in_specs=[row(d),
                  pl.BlockSpec((None, 6, d), _mod_row_map(tm, n_tok, n_ctx, n_batch)),
                  tab, tab, tab, tab,
                  _const_spec((d, C_END)), _const_spec((1, BRANCH_W)),
                  _const_spec((SGU_GROUPS, SGU_CHUNK, SGU_CHUNK)), _const_spec((SGU_CHUNK, BRANCH_W)),
                  _const_spec((1, BRANCH_W)), _const_spec((BRANCH_W, 2 * hw)),
                  _const_spec((1, LANES)), _const_spec((LANES, 2 * hw)),
                  _const_spec((1, BRANCH_W)), _const_spec((1, LANES)),
                  _const_spec((BRANCH_W, BRANCH_W))],
        out_specs=[row(w) for w, _ in outs],
        out_shape=[jax.ShapeDtypeStruct((t, w), BF16) for w, _ in outs],
        compiler_params=_cparams(("arbitrary",)),
        name="in_proj",
    )(hs, mod, tabs["cosm"], tabs["sinm"], tabs["cosg"], tabs["sing"], lw["w_in"], lw["sgu_gain"],
      lw["w_sgu"], lw["b_sgu"], lw["q_gain"], lw["wq"], lw["kv_gain"], lw["wkv"], lw["gq_gain"],
      lw["gk_gain"], tabs["gmat"])


def _softmax_pv(s, v):
    m = jnp.max(s, axis=-1, keepdims=True)
    p = jnp.exp2(s - m).astype(BF16)
    r = _dot(p, v)
    return r[:, :LANES] * (1.0 / r[:, LANES:LANES + 1])


def _mla_attn_kernel(q_ref, k_ref, v_ref, o_ref, *, ctx_tiles, n_ctx):
    def run(nk):
        lane = lax.broadcasted_iota(jnp.int32, (q_ref.shape[0], LANES), 1)
        outs = []
        for pair in range(MLA_HEADS // 2):
            v = v_ref[0:nk, pair * 2 * LANES:(pair + 1) * 2 * LANES]
            res = []
            for hh in range(2):
                h = pair * 2 + hh
                s = _dot_nt(q_ref[:, h * LANES:(h + 1) * LANES], k_ref[0:nk, h * LANES:(h + 1) * LANES])
                res.append(_softmax_pv(s, v))
            outs.append(jnp.where(lane < MLA_V, res[0], res[1]))
        o_ref[...] = jnp.concatenate(outs, axis=1).astype(BF16)

    j = pl.program_id(1)
    pl.when(j < ctx_tiles)(lambda: run(n_ctx))
    pl.when(j >= ctx_tiles)(lambda: run(k_ref.shape[0]))


def _gqa_attn_kernel(q_ref, k_ref, v_ref, o_ref, *, ctx_tiles, n_ctx):
    def run(nk):
        lane = lax.broadcasted_iota(jnp.int32, (q_ref.shape[0], LANES), 1)
        k = k_ref[0:nk, :]
        v = v_ref[0:nk, :]
        outs = []
        for grp in range(GQA_HEADS // 2):
            q = q_ref[:, grp * LANES:(grp + 1) * LANES]
            zero = jnp.zeros_like(q)
            r_lo = _softmax_pv(_dot_nt(jnp.where(lane < GQA_HEAD_DIM, q, zero), k), v)
            r_hi = _softmax_pv(_dot_nt(jnp.where(lane >= GQA_HEAD_DIM, q, zero), k), v)
            outs.append(jnp.where(lane < GQA_HEAD_DIM, r_lo, r_hi))
        o_ref[...] = jnp.concatenate(outs, axis=1).astype(BF16)

    j = pl.program_id(1)
    pl.when(j < ctx_tiles)(lambda: run(n_ctx))
    pl.when(j >= ctx_tiles)(lambda: run(k_ref.shape[0]))


def _attn_call(body, name, q, k, v, dims):
    n_batch, n_tok, n_ctx = dims
    tq = TOKEN_TILE
    tiles = n_tok // tq
    return pl.pallas_call(
        functools.partial(body, ctx_tiles=n_ctx // tq, n_ctx=n_ctx),
        grid=(n_batch, tiles),
        in_specs=[pl.BlockSpec((tq, q.shape[1]), lambda b, j: (b * tiles + j, 0)),
                  pl.BlockSpec((n_tok, k.shape[1]), lambda b, j: (b, 0)),
                  pl.BlockSpec((n_tok, v.shape[1]), lambda b, j: (b, 0))],
        out_specs=pl.BlockSpec((tq, BRANCH_W), lambda b, j: (b * tiles + j, 0)),
        out_shape=jax.ShapeDtypeStruct((q.shape[0], BRANCH_W), BF16),
        compiler_params=_cparams(("arbitrary", "arbitrary")),
        name=name,
    )(q, k, v)


def _merge_kernel(hs_ref, mod_ref, ya_ref, yb_ref, yd_ref, p_ref, pprev_ref, pnext_ref, cb_ref,
                  wconv_ref, wgate_ref, bgate_ref, wbranch_ref, wout_ref, wr_hi_ref, wr_lo_ref, br_ref,
                  hs1_ref, hn2_ref, route_ref, wts_ref, cnt_ref, carry_ref, *, n_tok, n_ctx):
    tm = hs_ref.shape[0]
    mod = mod_ref[...]
    hs = hs_ref[...]
    hn = (_rms(hs) * (1.0 + mod[1:2, :]) + mod[0:1, :]).astype(BF16)

    start = (pl.program_id(0) * tm) % n_tok
    prev_ok = jnp.logical_and(start != 0, start != n_ctx)
    next_ok = jnp.logical_and(start + tm != n_ctx, start + tm != n_tok)
    p = p_ref[...].astype(F32)
    halo_prev = jnp.where(prev_ok, pprev_ref[...].astype(F32)[15:16, :], 0.0)
    halo_next = jnp.where(next_ok, pnext_ref[...].astype(F32)[0:1, :], 0.0)
    rowi = lax.broadcasted_iota(jnp.int32, p.shape, 0)
    prev = jnp.where(rowi == 0, halo_prev, pltpu.roll(p, 1, axis=0))
    nxt = jnp.where(rowi == tm - 1, halo_next, pltpu.roll(p, tm - 1, axis=0))
    wc = wconv_ref[...]
    yc = cb_ref[...].astype(F32) * (wc[0:1, :] * prev + wc[1:2, :] * p + wc[2:3, :] * nxt)

    branches = (ya_ref[...], yb_ref[...], yc.astype(BF16), yd_ref[...])
    merged = None
    for i, y in enumerate(branches):
        gate = _dot(hn, wgate_ref[i]) + bgate_ref[i:i + 1, :]
        term = (1.0 / (1.0 + jnp.exp(-gate))) * _dot(y, wbranch_ref[i])
        merged = term if merged is None else merged + term
    hs1 = hs + _dot(merged.astype(BF16), wout_ref[...]) * mod[2:3, :]
    hs1_ref[...] = hs1

    hn2 = _rms(hs1) * (1.0 + mod[4:5, :]) + mod[3:4, :]
    _rows_to_tiles(hn2_ref, hn2)

    x_hi, x_lo = _split_bf16(hn2)
    logits = (_dot(x_hi, wr_hi_ref[...]) + _dot(x_lo, wr_hi_ref[...]) + _dot(x_hi, wr_lo_ref[...])
              + br_ref[...])
    lane = lax.broadcasted_iota(jnp.int32, logits.shape, 1)
    neg = -jnp.inf
    big = jnp.int32(1 << 20)
    gl = jnp.where(lane < N_GROUPS, logits, neg)
    gmax = jnp.max(gl, axis=-1, keepdims=True)
    ge = jnp.exp(gl - gmax)
    pg = ge / jnp.sum(ge, axis=-1, keepdims=True)
    pg_top = jnp.max(pg, axis=-1, keepdims=True)
    g_sel = jnp.min(jnp.where(pg == pg_top, lane, big), axis=-1, keepdims=True)
    eidx = lane - N_GROUPS
    in_group = jnp.logical_and(eidx >= g_sel * EXPERTS_PER_GROUP, eidx < (g_sel + 1) * EXPERTS_PER_GROUP)
    le = jnp.where(in_group, logits, neg)
    emax = jnp.max(le, axis=-1, keepdims=True)
    ee = jnp.exp(le - emax)
    pe = ee / jnp.sum(ee, axis=-1, keepdims=True)
    p1 = jnp.max(pe, axis=-1, keepdims=True)
    i1 = jnp.min(jnp.where(jnp.logical_and(in_group, pe == p1), lane, big), axis=-1, keepdims=True)
    rest = jnp.logical_and(in_group, lane != i1)
    pe2 = jnp.where(rest, pe, neg)
    p2 = jnp.max(pe2, axis=-1, keepdims=True)
    i2 = jnp.min(jnp.where(jnp.logical_and(rest, pe2 == p2), lane, big), axis=-1, keepdims=True)
    denom = p1 + p2
    wts_ref[...] = jnp.where(lane == 0, pg_top * p1 / denom, jnp.where(lane == 1, pg_top * p2 / denom, 0.0))

    @pl.when(pl.program_id(0) == 0)
    def _():
        carry_ref[...] = jnp.zeros_like(carry_ref)

    oh1 = jnp.where(lane == i1, 1.0, 0.0)
    oh2 = jnp.where(lane == i2, 1.0, 0.0)
    ri = lax.broadcasted_iota(jnp.int32, (tm, tm), 0)
    ci = lax.broadcasted_iota(jnp.int32, (tm, tm), 1)
    below = jnp.where(ri > ci, 1.0, 0.0).astype(BF16)
    prefix = _dot(below, jnp.concatenate([oh1, oh2], axis=1).astype(BF16))
    carry = carry_ref[...]
    c1 = jnp.sum(oh1, axis=0, keepdims=True)
    c2 = jnp.sum(oh2, axis=0, keepdims=True)
    r1 = jnp.sum(oh1 * (carry + prefix[:, :LANES]), axis=-1, keepdims=True)
    r2 = jnp.sum(oh2 * (carry + c1 + prefix[:, LANES:]), axis=-1, keepdims=True)
    total = carry + c1 + c2
    carry_ref[...] = total
    cnt_ref[...] = jnp.broadcast_to(total, cnt_ref.shape).astype(jnp.int32)
    route_ref[...] = jnp.where(lane == 0, i1 - N_GROUPS,
                               jnp.where(lane == 1, i2 - N_GROUPS,
                                         jnp.where(lane == 2, r1.astype(jnp.int32),
                                                   jnp.where(lane == 3, r2.astype(jnp.int32), 0))))


def _merge_call(hs, mod, ya, yb, yd, p, cb, lw, dims):
    n_batch, n_tok, n_ctx = dims
    t, d = hs.shape
    tm = TOKEN_TILE
    sub = 16
    n_sub = t // sub
    row = lambda w: pl.BlockSpec((tm, w), lambda i: (i, 0))
    return pl.pallas_call(
        functools.partial(_merge_kernel, n_tok=n_tok, n_ctx=n_ctx),
        grid=(t // tm,),
        in_specs=[row(d),
                  pl.BlockSpec((None, 6, d), _mod_row_map(tm, n_tok, n_ctx, n_batch)),
                  row(BRANCH_W), row(BRANCH_W), row(BRANCH_W), row(BRANCH_W),
                  pl.BlockSpec((sub, BRANCH_W), lambda i: (jnp.maximum(i * (tm // sub) - 1, 0), 0)),
                  pl.BlockSpec((sub, BRANCH_W), lambda i: (jnp.minimum((i + 1) * (tm // sub), n_sub - 1), 0)),
                  row(BRANCH_W),
                  _const_spec((3, BRANCH_W)), _const_spec((4, d, d)), _const_spec((4, d)),
                  _const_spec((4, BRANCH_W, d)), _const_spec((d, d)),
                  _const_spec((d, LANES)), _const_spec((d, LANES)), _const_spec((1, LANES))],
        out_specs=[row(d), pl.BlockSpec((tm * ROW_SUB, LANES), lambda i: (i, 0)),
                   row(LANES), row(LANES), pl.BlockSpec((8, LANES), lambda i: (0, 0))],
        out_shape=[jax.ShapeDtypeStruct((t, d), F32), jax.ShapeDtypeStruct((t * ROW_SUB, LANES), F32),
                   jax.ShapeDtypeStruct((t, LANES), jnp.int32), jax.ShapeDtypeStruct((t, LANES), F32),
                   jax.ShapeDtypeStruct((8, LANES), jnp.int32)],
        scratch_shapes=[pltpu.VMEM((1, LANES), F32)],
        compiler_params=_cparams(("arbitrary",)),
        name="merge_route",
    )(hs, mod, ya, yb, yd, p, p, p, cb, lw["w_conv"], lw["w_gate"], lw["b_gate"], lw["w_branch"],
      lw["w_out"], lw["wr_hi"], lw["wr_lo"], lw["br"])


PAD_BITS = tuple(1 << b for b in reversed(range(EXPERT_BLK.bit_length() - 1)))


def _rows_to_tiles(ref, v):
    for ck in range(ROW_SUB):
        ref[pl.ds(ck, v.shape[0], stride=ROW_SUB), :] = v[:, ck * LANES:(ck + 1) * LANES]


def _tiles_to_rows(ref, first, n):
    return jnp.concatenate([ref[pl.ds(first * ROW_SUB + ck, n, stride=ROW_SUB), :] for ck in range(ROW_SUB)],
                           axis=1)


def _tile_of(ref, row, n=1):
    return ref.at[pl.ds(pl.multiple_of(row * ROW_SUB, ROW_SUB), n * ROW_SUB)]


def _tiles_wait(like, sem):
    pltpu.make_async_copy(like, like, sem).wait()


def _dispatch_kernel(pos_ref, pad_start_ref, pad_len_ref, tail_ref, x_ref, xs_hbm, stage, zeros, sems, pad_sem):
    i = pl.program_id(0)
    n = pl.num_programs(0)
    tm = x_ref.shape[0] // ROW_SUB
    slot = i % 2

    def pad_copies(e, fn):
        start = pad_start_ref[e]
        ln = pad_len_ref[e]
        for b in PAD_BITS:
            @pl.when((ln & b) != 0)
            def _():
                off = ln & ~(2 * b - 1)
                fn(pltpu.make_async_copy(_tile_of(zeros, 0, b), _tile_of(xs_hbm, start + off, b), pad_sem))

    def for_each_expert(fn):
        def body(e, carry):
            pad_copies(e, fn)
            return carry
        lax.fori_loop(0, N_EXPERTS, body, 0)

        def tail(j, carry):
            fn(pltpu.make_async_copy(zeros, _tile_of(xs_hbm, j * PAD_BITS[0], PAD_BITS[0]), pad_sem))
            return carry
        lax.fori_loop(tail_ref[0] // PAD_BITS[0], xs_hbm.shape[0] // (ROW_SUB * PAD_BITS[0]), tail, 0)

    @pl.when(i == 0)
    def _():
        zeros[...] = jnp.zeros_like(zeros)
        for_each_expert(lambda cp: cp.start())

    @pl.when(i >= 2)
    def _():
        _tiles_wait(stage, sems.at[slot])

    stage[slot] = x_ref[...]

    def push(r, carry):
        for k in range(2):
            dst = pos_ref[i * (2 * tm) + k * tm + r]
            pltpu.make_async_copy(_tile_of(stage.at[slot], r), _tile_of(xs_hbm, dst), sems.at[slot]).start()
        return carry
    lax.fori_loop(0, tm, push, 0, unroll=8)

    @pl.when(i == n - 1)
    def _():
        _tiles_wait(stage, sems.at[slot])

        @pl.when(n >= 2)
        def _():
            _tiles_wait(stage, sems.at[1 - slot])
        for_each_expert(lambda cp: cp.wait())


def _dispatch_call(pos, pad_start, pad_len, tail_start, x_tiles, cap):
    tm = TOKEN_TILE
    grid_spec = pltpu.PrefetchScalarGridSpec(
        num_scalar_prefetch=4,
        grid=(x_tiles.shape[0] // (tm * ROW_SUB),),
        in_specs=[pl.BlockSpec((tm * ROW_SUB, LANES), lambda i, *_: (i, 0))],
        out_specs=pl.BlockSpec(memory_space=pl.ANY),
        scratch_shapes=[pltpu.VMEM((2, tm * ROW_SUB, LANES), F32), pltpu.VMEM((PAD_BITS[0] * ROW_SUB, LANES), F32),
                        pltpu.SemaphoreType.DMA((2,)), pltpu.SemaphoreType.DMA(())],
    )
    return pl.pallas_call(
        _dispatch_kernel,
        grid_spec=grid_spec,
        out_shape=jax.ShapeDtypeStruct((cap * ROW_SUB, LANES), F32),
        compiler_params=_cparams(("arbitrary",)),
        name="dispatch",
    )(pos, pad_start, pad_len, tail_start, x_tiles)


def _expert_kernel(blk_e_ref, n_used_ref, x_ref, wgu_ref, wdn_ref, y_ref):
    @pl.when(pl.program_id(0) < n_used_ref[0])
    def _():
        xe = _tiles_to_rows(x_ref, 0, EXPERT_BLK).astype(BF16)
        gu = _dot(xe, wgu_ref[...])
        g = gu[:, :D_EXPERT]
        act = (g * (1.0 / (1.0 + jnp.exp(-g))) * gu[:, D_EXPERT:]).astype(BF16)
        _rows_to_tiles(y_ref, _dot(act, wdn_ref[...]))

    @pl.when(pl.program_id(0) >= n_used_ref[0])
    def _():
        y_ref[...] = jnp.zeros_like(y_ref)


def _expert_call(blk_e, n_used, xs, w_gu, w_dn):
    d = ROW_SUB * LANES
    blk = EXPERT_BLK
    used = lambda i, be, nu: (jnp.minimum(i, nu[0] - 1), 0)
    grid_spec = pltpu.PrefetchScalarGridSpec(
        num_scalar_prefetch=2,
        grid=(xs.shape[0] // (blk * ROW_SUB),),
        in_specs=[pl.BlockSpec((blk * ROW_SUB, LANES), used),
                  pl.BlockSpec((None, d, 2 * D_EXPERT), lambda i, be, nu: (be[i], 0, 0)),
                  pl.BlockSpec((None, D_EXPERT, d), lambda i, be, nu: (be[i], 0, 0))],
        out_specs=pl.BlockSpec((blk * ROW_SUB, LANES), lambda i, be, nu: (i, 0)),
    )
    return pl.pallas_call(
        _expert_kernel,
        grid_spec=grid_spec,
        out_shape=jax.ShapeDtypeStruct(xs.shape, F32),
        compiler_params=_cparams(("arbitrary",)),
        name="experts",
    )(blk_e, n_used, xs, w_gu, w_dn)


def _combine_kernel(pos_ref, y_hbm, hs1_ref, wts_ref, mod_ref, gain_ref, o_ref, gbuf, sems, *,
                    tile_map, final):
    i = pl.program_id(0)
    n = pl.num_programs(0)
    tm = hs1_ref.shape[0]
    slot = i % 2

    def start(step, s):
        base = tile_map(step) * (2 * tm)

        def body(r, carry):
            src = pos_ref[base + r]
            pltpu.make_async_copy(_tile_of(y_hbm, src), _tile_of(gbuf.at[s], r), sems.at[s]).start()
            return carry
        lax.fori_loop(0, 2 * tm, body, 0, unroll=8)

    @pl.when(i == 0)
    def _():
        start(0, 0)

    @pl.when(i + 1 < n)
    def _():
        start(i + 1, 1 - slot)

    _tiles_wait(gbuf.at[slot], sems.at[slot])
    wts = wts_ref[...]
    g = gbuf.at[slot]
    f = wts[:, 0:1] * _tiles_to_rows(g, 0, tm) + wts[:, 1:2] * _tiles_to_rows(g, tm, tm)
    out = hs1_ref[...] + f * mod_ref[...][5:6, :]
    if final:
        out = _rms(out) * gain_ref[...]
    o_ref[...] = out


def _combine_call(pos, y, hs1, wts, mod, gain, dims, final):
    n_batch, n_tok, n_ctx = dims
    t, d = hs1.shape
    tm = TOKEN_TILE
    tiles = n_tok // tm
    ctx_tiles = n_ctx // tm
    if final:
        lat = tiles - ctx_tiles
        tile_map = lambda i: (i // lat) * tiles + ctx_tiles + i % lat
        n_steps = n_batch * lat
    else:
        tile_map = lambda i: i
        n_steps = t // tm
    mod_map = _mod_row_map(tm, n_tok, n_ctx, n_batch)
    grid_spec = pltpu.PrefetchScalarGridSpec(
        num_scalar_prefetch=1,
        grid=(n_steps,),
        in_specs=[pl.BlockSpec(memory_space=pl.ANY),
                  pl.BlockSpec((tm, d), lambda i, pos: (tile_map(i), 0)),
                  pl.BlockSpec((tm, LANES), lambda i, pos: (tile_map(i), 0)),
                  pl.BlockSpec((None, 6, d), lambda i, pos: mod_map(tile_map(i))),
                  pl.BlockSpec((1, d), lambda i, pos: (0, 0))],
        out_specs=pl.BlockSpec((tm, d), lambda i, pos: (i, 0)),
        scratch_shapes=[pltpu.VMEM((2, 2 * tm * ROW_SUB, LANES), F32), pltpu.SemaphoreType.DMA((2,))],
    )
    return pl.pallas_call(
        functools.partial(_combine_kernel, tile_map=tile_map, final=final),
        grid_spec=grid_spec,
        out_shape=jax.ShapeDtypeStruct((n_steps * tm, d), F32),
        compiler_params=_cparams(("arbitrary",)),
        name="combine_final" if final else "combine",
    )(pos, y, hs1, wts, mod, gain)


def _rope_angles(n_ctx, rows, rot_dim):
    n_freq = rot_dim // 4
    freqs = ROPE_THETA ** (-jnp.arange(n_freq, dtype=F32) / n_freq)
    row = jnp.repeat(jnp.arange(rows, dtype=F32), GRID_W)
    col = (jnp.arange(rows * GRID_W) % GRID_W).astype(F32)
    ang = jnp.concatenate([row[:, None] * freqs, col[:, None] * freqs], axis=-1)
    ang = jnp.concatenate([jnp.zeros((n_ctx, rot_dim // 2), F32), ang], axis=0)
    return jnp.cos(ang), jnp.sin(ang)


def _tables(n_ctx, seq):
    n = n_ctx + seq
    cm, sm = _rope_angles(n_ctx, seq // GRID_W, MLA_ROPE)
    cg, sg = _rope_angles(n_ctx, seq // GRID_W, GQA_HEAD_DIM)
    one, zero = jnp.ones((n, MLA_NOPE), F32), jnp.zeros((n, MLA_NOPE), F32)
    pad = jnp.zeros((n, LANES - MLA_NOPE - MLA_ROPE), F32)
    gi = jnp.arange(BRANCH_W) // GQA_HEAD_DIM
    return {
        "cosm": jnp.concatenate([one, cm, cm, pad], axis=1),
        "sinm": jnp.concatenate([zero, -sm, sm, pad], axis=1),
        "cosg": jnp.concatenate([cg, cg, cg, cg], axis=1),
        "sing": jnp.concatenate([-sg, sg, -sg, sg], axis=1),
        "gmat": jnp.where(gi[:, None] == gi[None, :], 1.0 / GQA_HEAD_DIM, 0.0).astype(BF16),
    }


def _prep_weights(w_in, sgu_gain, w_sgu, b_sgu, mla_q_gain, w_uq, mla_kv_gain, w_ukv, w_conv,
                  gqa_q_gain, gqa_k_gain, w_gate, b_gate, w_branch, w_out, w_group_router,
                  b_group_router, w_expert_router, b_expert_router, w_expert_gate_up, w_expert_down):
    nl, d, _ = w_in.shape
    zc = lambda n: jnp.zeros((nl, d, n), F32)
    half = MLA_ROPE // 2
    kpe = w_in[..., 896:928]
    gq = w_in[..., 1696:1952].reshape(nl, d, GQA_HEADS, GQA_HEAD_DIM)[:, :, (0, 2, 1, 3), :].reshape(nl, d, BRANCH_W)
    w_in_ext = jnp.concatenate([
        w_in[..., 0:896],
        zc(MLA_NOPE), kpe, zc(LANES - MLA_NOPE - MLA_ROPE),
        zc(MLA_NOPE), kpe[..., half:], kpe[..., :half], zc(LANES - MLA_NOPE - MLA_ROPE),
        w_in[..., 928:1696], gq, w_in[..., 1952:2208]], axis=-1).astype(BF16)

    dq = MLA_NOPE + MLA_ROPE
    uq = w_uq.reshape(nl, -1, MLA_HEADS, dq)
    zq = jnp.zeros(uq.shape[:3] + (LANES - dq,), F32)
    wq_a = jnp.concatenate([uq, zq], axis=-1)
    wq_b = jnp.concatenate([jnp.zeros_like(uq[..., :MLA_NOPE]), uq[..., MLA_NOPE + half:],
                            uq[..., MLA_NOPE:MLA_NOPE + half], zq], axis=-1)
    wq = jnp.concatenate([wq_a.reshape(nl, -1, MLA_HEADS * LANES),
                          wq_b.reshape(nl, -1, MLA_HEADS * LANES)], axis=-1).astype(BF16)

    ukv = w_ukv.reshape(nl, -1, MLA_HEADS, MLA_NOPE + MLA_V)
    k_lay = jnp.concatenate([ukv[..., :MLA_NOPE], jnp.zeros_like(ukv[..., :LANES - MLA_NOPE])], axis=-1)
    v_h = ukv[..., MLA_NOPE:]
    zv = jnp.zeros(v_h.shape[:2] + (LANES,), F32)
    v_lay = jnp.concatenate([v_h[:, :, 0], v_h[:, :, 1], zv, v_h[:, :, 2], v_h[:, :, 3], zv], axis=-1)
    wkv = jnp.concatenate([k_lay.reshape(nl, -1, MLA_HEADS * LANES), v_lay], axis=-1).astype(BF16)

    wb = w_branch
    wb_d = wb[:, 3].reshape(nl, GQA_HEADS, GQA_HEAD_DIM, d)[:, (0, 2, 1, 3)].reshape(nl, BRANCH_W, d)
    w_branch_p = jnp.concatenate([wb[:, :3], wb_d[:, None]], axis=1).astype(BF16)

    wr = jnp.concatenate([w_group_router, w_expert_router,
                          jnp.zeros((nl, d, LANES - N_GROUPS - N_EXPERTS), F32)], axis=-1)
    wr_hi = wr.astype(BF16)
    wr_lo = (wr - wr_hi.astype(F32)).astype(BF16)
    br = jnp.concatenate([b_group_router, b_expert_router,
                          jnp.zeros((nl, LANES - N_GROUPS - N_EXPERTS), F32)], axis=-1)[:, None, :]

    tile = lambda g, reps: jnp.concatenate([g] * reps, axis=-1)[:, None, :]
    return {
        "w_in": w_in_ext,
        "sgu_gain": sgu_gain[:, None, :],
        "w_sgu": w_sgu.astype(BF16),
        "b_sgu": jnp.repeat(b_sgu.transpose(0, 2, 1), BRANCH_W // SGU_GROUPS, axis=-1),
        "q_gain": mla_q_gain[:, None, :], "wq": wq,
        "kv_gain": mla_kv_gain[:, None, :], "wkv": wkv,
        "gq_gain": tile(gqa_q_gain, 4), "gk_gain": tile(gqa_k_gain, 2),
        "w_conv": w_conv,
        "w_gate": w_gate.astype(BF16), "b_gate": b_gate,
        "w_branch": w_branch_p, "w_out": w_out.astype(BF16),
        "wr_hi": wr_hi, "wr_lo": wr_lo, "br": br,
        "w_gu": w_expert_gate_up.astype(BF16), "w_dn": w_expert_down.astype(BF16),
    }


def _dispatch_plan(route, cnt, n_tiles, cap):
    blk = EXPERT_BLK
    tm = route.shape[0] // n_tiles
    counts = cnt[0, N_GROUPS:N_GROUPS + N_EXPERTS]
    padded = (counts + blk - 1) // blk * blk
    pend = jnp.cumsum(padded)
    pstart = pend - padded
    eid, rank = route[:, 0:2], route[:, 2:4]
    onehot = eid[:, :, None] == jnp.arange(N_EXPERTS, dtype=jnp.int32)
    dest = jnp.sum(jnp.where(onehot, pstart, 0), axis=-1) + rank
    pos = dest.reshape(n_tiles, tm, 2).transpose(0, 2, 1).reshape(-1)
    n_used = pend[-1] // blk
    blk_first = jnp.minimum(jnp.arange(cap // blk, dtype=jnp.int32), n_used - 1) * blk
    blk_e = jnp.sum((blk_first[:, None] >= pend[None, :]).astype(jnp.int32), axis=1)
    blk_e = jnp.minimum(blk_e, N_EXPERTS - 1)
    return pos, pstart + counts, padded - counts, blk_e, n_used.reshape(1)


def kernel(x, c, ctx, c_ctx, w_ada, b_ada, w_in, sgu_gain, w_sgu, b_sgu, mla_q_gain, w_uq, mla_kv_gain, w_ukv, w_conv, gqa_q_gain, gqa_k_gain, w_gate, b_gate, w_branch, w_out, w_group_router, b_group_router, w_expert_router, b_expert_router, w_expert_gate_up, w_expert_down, final_gain):
    n_batch, seq, d = x.shape
    n_ctx = ctx.shape[1]
    n_tok = n_ctx + seq
    t = n_batch * n_tok
    depth = w_in.shape[0]
    tm = TOKEN_TILE
    assert n_ctx % tm == 0 and seq % tm == 0 and seq % GRID_W == 0 and tm % SGU_CHUNK == 0
    dims = (n_batch, n_tok, n_ctx)

    hs = jnp.concatenate([ctx, x], axis=1).reshape(t, d)
    tabs = _tables(n_ctx, seq)
    rows = -(-(n_batch + 1) // 8) * 8
    cvec = jnp.concatenate([c, c_ctx[None, :], jnp.zeros((rows - n_batch - 1, d), F32)], axis=0)
    mods = _ada_call(cvec, w_ada, b_ada).reshape(depth, rows, 6, d)
    weights = _prep_weights(w_in, sgu_gain, w_sgu, b_sgu, mla_q_gain, w_uq, mla_kv_gain, w_ukv, w_conv,
                            gqa_q_gain, gqa_k_gain, w_gate, b_gate, w_branch, w_out, w_group_router,
                            b_group_router, w_expert_router, b_expert_router, w_expert_gate_up,
                            w_expert_down)
    gain = final_gain[None, :]
    cap = -(-2 * t // EXPERT_BLK) * EXPERT_BLK + N_EXPERTS * EXPERT_BLK

    for l in range(depth):
        lw = {k: v[l] for k, v in weights.items()}
        mod = mods[l]
        ya, qm, km, vm, p, cb, qg, kg, vg = _in_call(hs, mod, tabs, lw, dims)
        yb = _attn_call(_mla_attn_kernel, "mla_attn", qm, km, vm, dims)
        yd = _attn_call(_gqa_attn_kernel, "gqa_attn", qg, kg, vg, dims)
        hs1, hn2, route, wts, cnt = _merge_call(hs, mod, ya, yb, yd, p, cb, lw, dims)
        pos, pad_start, pad_len, blk_e, n_used = _dispatch_plan(route, cnt, t // tm, cap)
        xs = _dispatch_call(pos, pad_start, pad_len, n_used * EXPERT_BLK, hn2, cap)
        y = _expert_call(blk_e, n_used, xs, lw["w_gu"], lw["w_dn"])
        hs = _combine_call(pos, y, hs1, wts, mod, gain, dims, final=(l == depth - 1))
    return hs.reshape(n_batch, seq, d)
```

```python
import functools
import math

import jax
import jax.numpy as jnp
from jax import lax
from jax.experimental import pallas as pl
from jax.experimental.pallas import tpu as pltpu

F32 = jnp.float32
BF16 = jnp.bfloat16

GRID_W = 64
ROPE_THETA = 10000.0
NORM_EPS = 1e-6
BRANCH_W = 256
SGU_CHUNK = 128
SGU_GROUPS = 4
MLA_HEADS = 4
MLA_NOPE = 64
MLA_ROPE = 32
MLA_V = 64
GQA_HEADS = 4
GQA_HEAD_DIM = 64
N_GROUPS = 4
EXPERTS_PER_GROUP = 8
N_EXPERTS = N_GROUPS * EXPERTS_PER_GROUP
D_EXPERT = 256
LANES = 128
ROW_SUB = 8
TOKEN_TILE = 256
EXPERT_BLK = 256
VMEM_LIMIT = 56 * 1024 * 1024
LOG2E = 1.4426950408889634

C_AU, C_AV, C_MQ, C_MKV, C_KPA, C_KPB, C_CB, C_CC, C_CX, C_GQ, C_GK, C_GV, C_END = (
    0, 256, 512, 768, 896, 1024, 1152, 1408, 1664, 1920, 2176, 2304, 2432)


def _cparams(sem):
    return pltpu.CompilerParams(dimension_semantics=sem, vmem_limit_bytes=VMEM_LIMIT)


def _dot(a, b):
    return jnp.dot(a, b, preferred_element_type=F32)


def _dot_nt(a, b):
    return lax.dot_general(a, b, (((1,), (1,)), ((), ())), preferred_element_type=F32)


def _split_bf16(v):
    hi = v.astype(BF16)
    lo = (v - hi.astype(F32)).astype(BF16)
    return hi, lo


def _gelu(v):
    return 0.5 * v * (1.0 + jnp.tanh(math.sqrt(2.0 / math.pi) * (v + 0.044715 * (v * v * v))))


def _sigmoid(v):
    return 0.5 * jnp.tanh(0.5 * v) + 0.5


def _rms(v):
    return v * lax.rsqrt(jnp.mean(v * v, axis=-1, keepdims=True) + NORM_EPS)


def _const_spec(shape):
    nd = len(shape)
    return pl.BlockSpec(shape, lambda *_: (0,) * nd)


def _ada_kernel(c_ref, w_ref, b_ref, o_ref):
    cv = c_ref[...]
    s = cv * _sigmoid(cv)
    s_hi, s_lo = _split_bf16(s)
    w_hi, w_lo = _split_bf16(w_ref[...])
    o_ref[...] = _dot(s_hi, w_hi) + _dot(s_lo, w_hi) + _dot(s_hi, w_lo) + b_ref[...]


def _ada_call(cvec, w_ada, b_ada):
    n_layers, d, d6 = w_ada.shape
    rows = cvec.shape[0]
    tn = 1536
    return pl.pallas_call(
        _ada_kernel,
        grid=(n_layers, d6 // tn),
        in_specs=[pl.BlockSpec((rows, d), lambda l, j: (0, 0)),
                  pl.BlockSpec((None, d, tn), lambda l, j: (l, 0, j)),
                  pl.BlockSpec((None, 1, tn), lambda l, j: (l, 0, j))],
        out_specs=pl.BlockSpec((None, rows, tn), lambda l, j: (l, 0, j)),
        out_shape=jax.ShapeDtypeStruct((n_layers, rows, d6), F32),
        compiler_params=_cparams(("arbitrary", "arbitrary")),
        name="ada_mod",
    )(cvec, w_ada, b_ada.reshape(n_layers, 1, d6))


def _in_kernel(hs_ref, mod_ref, cosm_ref, sinm_ref, cosg_ref, sing_ref, w_in_ref, sgu_gain_ref,
               w_sgu_ref, b_sgu_ref, qgain_ref, wq_ref, kvgain_ref, wkv_ref, gqgain_ref, gkgain_ref,
               gmat_ref,
               ya_ref, qm_ref, km_ref, vm_ref, p_ref, cb_ref, qg_ref, kg_ref, vg_ref):
    tm = hs_ref.shape[0]
    mod = mod_ref[...]
    hn = _rms(hs_ref[...]) * (1.0 + mod[1:2, :]) + mod[0:1, :]
    z = _dot(hn.astype(BF16), w_in_ref[...])

    av = _rms(_gelu(z[:, C_AV:C_MQ])) * sgu_gain_ref[...]
    av = av.astype(BF16)
    grp = lax.broadcasted_iota(jnp.int32, (SGU_CHUNK, BRANCH_W), 1) // (BRANCH_W // SGU_GROUPS)
    chunks = []
    for ci in range(tm // SGU_CHUNK):
        vc = av[ci * SGU_CHUNK:(ci + 1) * SGU_CHUNK, :]
        m = b_sgu_ref[...]
        for g in range(SGU_GROUPS):
            m = m + jnp.where(grp == g, _dot(w_sgu_ref[g], vc), 0.0)
        chunks.append(m)
    mixed = jnp.concatenate(chunks, axis=0)
    ya_ref[...] = (_gelu(z[:, C_AU:C_AV]) * mixed).astype(BF16)

    cosm = cosm_ref[...]
    sinm = sinm_ref[...]
    cos4 = jnp.concatenate([cosm] * MLA_HEADS, axis=1)
    sin4 = jnp.concatenate([sinm] * MLA_HEADS, axis=1)
    nq = (_rms(z[:, C_MQ:C_MKV]) * qgain_ref[...]).astype(BF16)
    qab = _dot(nq, wq_ref[...])
    hw = MLA_HEADS * LANES
    q_scale = (MLA_NOPE + MLA_ROPE) ** -0.5 * LOG2E
    qm_ref[...] = ((qab[:, :hw] * cos4 + qab[:, hw:] * sin4) * q_scale).astype(BF16)
    nkv = (_rms(z[:, C_MKV:C_KPA]) * kvgain_ref[...]).astype(BF16)
    kv = _dot(nkv, wkv_ref[...])
    kpe = z[:, C_KPA:C_KPB] * cosm + z[:, C_KPB:C_CB] * sinm
    km_ref[...] = (kv[:, :hw] + jnp.concatenate([kpe] * MLA_HEADS, axis=1)).astype(BF16)
    lane = lax.broadcasted_iota(jnp.int32, (tm, hw), 1)
    ones_col = jnp.where(lane % (2 * LANES) == LANES, 1.0, 0.0)
    vm_ref[...] = (kv[:, hw:] + ones_col).astype(BF16)

    p_ref[...] = (z[:, C_CC:C_CX] * z[:, C_CX:C_GQ]).astype(BF16)
    cb_ref[...] = z[:, C_CB:C_CC].astype(BF16)

    cosg = cosg_ref[...]
    sing = sing_ref[...]
    gmat = gmat_ref[...]

    def head_norm_rope(v, gain, cos_t, sin_t, scale):
        w = v.shape[1]
        sq_hi, sq_lo = _split_bf16(v * v)
        ms = _dot(sq_hi, gmat[:w, :w]) + _dot(sq_lo, gmat[:w, :w])
        y = v * lax.rsqrt(ms + NORM_EPS) * gain
        half = GQA_HEAD_DIM // 2
        ln = lax.broadcasted_iota(jnp.int32, v.shape, 1) % GQA_HEAD_DIM
        partner = jnp.where(ln < half, pltpu.roll(y, w - half, axis=1), pltpu.roll(y, half, axis=1))
        return ((y * cos_t + partner * sin_t) * scale).astype(BF16)

    cos2 = jnp.concatenate([cosg, cosg], axis=1)
    sin2 = jnp.concatenate([sing, sing], axis=1)
    qg_ref[...] = head_norm_rope(z[:, C_GQ:C_GK], gqgain_ref[...], cos2, sin2,
                                 GQA_HEAD_DIM ** -0.5 * LOG2E)
    kg_ref[...] = head_norm_rope(z[:, C_GK:C_GV], gkgain_ref[...], cosg, sing, 1.0)
    lane2 = lax.broadcasted_iota(jnp.int32, (tm, LANES), 1)
    vg_ref[...] = jnp.concatenate([z[:, C_GV:C_END], jnp.where(lane2 == 0, 1.0, 0.0)], axis=1).astype(BF16)


def _mod_row_map(tm, n_tok, n_ctx, n_batch):
    def index_map(i):
        start = i * tm
        return (jnp.where(start % n_tok < n_ctx, n_batch, start // n_tok), 0, 0)
    return index_map


def _in_call(hs, mod, tabs, lw, dims):
    n_batch, n_tok, n_ctx = dims
    t, d = hs.shape
    tm = TOKEN_TILE
    tiles_per_seq = n_tok // tm
    row = lambda w: pl.BlockSpec((tm, w), lambda i: (i, 0))
    tab = pl.BlockSpec((tm, LANES), lambda i: (i % tiles_per_seq, 0))
    hw = MLA_HEADS * LANES
    outs = [(BRANCH_W, "ya"), (hw, "qm"), (hw, "km"), (hw, "vm"), (BRANCH_W, "p"), (BRANCH_W, "cb"),
            (BRANCH_W, "qg"), (LANES, "kg"), (2 * LANES, "vg")]
    return pl.pallas_call(
        _in_kernel,
        grid=(t // tm,),
        in_specs=[row(d),
                  pl.BlockSpec((None, 6, d), _mod_row_map(tm, n_tok, n_ctx, n_batch)),
                  tab, tab, tab, tab,
                  _const_spec((d, C_END)), _const_spec((1, BRANCH_W)),
                  _const_spec((SGU_GROUPS, SGU_CHUNK, SGU_CHUNK)), _const_spec((SGU_CHUNK, BRANCH_W)),
                  _const_spec((1, BRANCH_W)), _const_spec((BRANCH_W, 2 * hw)),
                  _const_spec((1, LANES)), _const_spec((LANES, 2 * hw)),
                  _const_spec((1, BRANCH_W)), _const_spec((1, LANES)),
                  _const_spec((BRANCH_W, BRANCH_W))],
        out_specs=[row(w) for w, _ in outs],
        out_shape=[jax.ShapeDtypeStruct((t, w), BF16) for w, _ in outs],
        compiler_params=_cparams(("arbitrary",)),
        name="in_proj",
    )(hs, mod, tabs["cosm"], tabs["sinm"], tabs["cosg"], tabs["sing"], lw["w_in"], lw["sgu_gain"],
      lw["w_sgu"], lw["b_sgu"], lw["q_gain"], lw["wq"], lw["kv_gain"], lw["wkv"], lw["gq_gain"],
      lw["gk_gain"], tabs["gmat"])


def _softmax_pv(s, v):
    m = jnp.max(s, axis=-1, keepdims=True)
    p = jnp.exp2(s - m).astype(BF16)
    r = _dot(p, v)
    return r[:, :LANES] * (1.0 / r[:, LANES:LANES + 1])


def _mla_attn_kernel(q_ref, k_ref, v_ref, o_ref, *, ctx_tiles, n_ctx):
    def run(nk):
        lane = lax.broadcasted_iota(jnp.int32, (q_ref.shape[0], LANES), 1)
        outs = []
        for pair in range(MLA_HEADS // 2):
            v = v_ref[0:nk, pair * 2 * LANES:(pair + 1) * 2 * LANES]
            res = []
            for hh in range(2):
                h = pair * 2 + hh
                s = _dot_nt(q_ref[:, h * LANES:(h + 1) * LANES], k_ref[0:nk, h * LANES:(h + 1) * LANES])
                res.append(_softmax_pv(s, v))
            outs.append(jnp.where(lane < MLA_V, res[0], res[1]))
        o_ref[...] = jnp.concatenate(outs, axis=1).astype(BF16)

    j = pl.program_id(1)
    pl.when(j < ctx_tiles)(lambda: run(n_ctx))
    pl.when(j >= ctx_tiles)(lambda: run(k_ref.shape[0]))


def _gqa_attn_kernel(q_ref, k_ref, v_ref, o_ref, *, ctx_tiles, n_ctx):
    def run(nk):
        lane = lax.broadcasted_iota(jnp.int32, (q_ref.shape[0], LANES), 1)
        k = k_ref[0:nk, :]
        v = v_ref[0:nk, :]
        outs = []
        for grp in range(GQA_HEADS // 2):
            q = q_ref[:, grp * LANES:(grp + 1) * LANES]
            zero = jnp.zeros_like(q)
            r_lo = _softmax_pv(_dot_nt(jnp.where(lane < GQA_HEAD_DIM, q, zero), k), v)
            r_hi = _softmax_pv(_dot_nt(jnp.where(lane >= GQA_HEAD_DIM, q, zero), k), v)
            outs.append(jnp.where(lane < GQA_HEAD_DIM, r_lo, r_hi))
        o_ref[...] = jnp.concatenate(outs, axis=1).astype(BF16)

    j = pl.program_id(1)
    pl.when(j < ctx_tiles)(lambda: run(n_ctx))
    pl.when(j >= ctx_tiles)(lambda: run(k_ref.shape[0]))


def _attn_call(body, name, q, k, v, dims):
    n_batch, n_tok, n_ctx = dims
    tq = TOKEN_TILE
    tiles = n_tok // tq
    return pl.pallas_call(
        functools.partial(body, ctx_tiles=n_ctx // tq, n_ctx=n_ctx),
        grid=(n_batch, tiles),
        in_specs=[pl.BlockSpec((tq, q.shape[1]), lambda b, j: (b * tiles + j, 0)),
                  pl.BlockSpec((n_tok, k.shape[1]), lambda b, j: (b, 0)),
                  pl.BlockSpec((n_tok, v.shape[1]), lambda b, j: (b, 0))],
        out_specs=pl.BlockSpec((tq, BRANCH_W), lambda b, j: (b * tiles + j, 0)),
        out_shape=jax.ShapeDtypeStruct((q.shape[0], BRANCH_W), BF16),
        compiler_params=_cparams(("arbitrary", "arbitrary")),
        name=name,
    )(q, k, v)


def _merge_kernel(hs_ref, mod_ref, ya_ref, yb_ref, yd_ref, p_ref, pprev_ref, pnext_ref, cb_ref,
                  wconv_ref, wgate_ref, bgate_ref, wbranch_ref, wout_ref, wr_hi_ref, wr_lo_ref, br_ref,
                  hs1_ref, hn2_ref, route_ref, wts_ref, cnt_ref, carry_ref, *, n_tok, n_ctx, tile_map):
    tm = hs_ref.shape[0]
    mod = mod_ref[...]
    hs = hs_ref[...]
    hn = (_rms(hs) * (1.0 + mod[1:2, :]) + mod[0:1, :]).astype(BF16)

    start = (tile_map(pl.program_id(0)) * tm) % n_tok
    prev_ok = jnp.logical_and(start != 0, start != n_ctx)
    next_ok = jnp.logical_and(start + tm != n_ctx, start + tm != n_tok)
    p = p_ref[...].astype(F32)
    halo_prev = jnp.where(prev_ok, pprev_ref[...].astype(F32)[15:16, :], 0.0)
    halo_next = jnp.where(next_ok, pnext_ref[...].astype(F32)[0:1, :], 0.0)
    rowi = lax.broadcasted_iota(jnp.int32, p.shape, 0)
    prev = jnp.where(rowi == 0, halo_prev, pltpu.roll(p, 1, axis=0))
    nxt = jnp.where(rowi == tm - 1, halo_next, pltpu.roll(p, tm - 1, axis=0))
    wc = wconv_ref[...]
    yc = cb_ref[...].astype(F32) * (wc[0:1, :] * prev + wc[1:2, :] * p + wc[2:3, :] * nxt)

    branches = (ya_ref[...], yb_ref[...], yc.astype(BF16), yd_ref[...])
    merged = None
    for i, y in enumerate(branches):
        gate = _dot(hn, wgate_ref[i]) + bgate_ref[i:i + 1, :]
        term = _sigmoid(gate) * _dot(y, wbranch_ref[i])
        merged = term if merged is None else merged + term
    hs1 = hs + _dot(merged.astype(BF16), wout_ref[...]) * mod[2:3, :]
    hs1_ref[...] = hs1

    hn2 = _rms(hs1) * (1.0 + mod[4:5, :]) + mod[3:4, :]
    _rows_to_tiles(hn2_ref, hn2)

    x_hi, x_lo = _split_bf16(hn2)
    logits = (_dot(x_hi, wr_hi_ref[...]) + _dot(x_lo, wr_hi_ref[...]) + _dot(x_hi, wr_lo_ref[...])
              + br_ref[...])
    lane = lax.broadcasted_iota(jnp.int32, logits.shape, 1)
    neg = -jnp.inf
    big = jnp.int32(1 << 20)
    gl = jnp.where(lane < N_GROUPS, logits, neg)
    gmax = jnp.max(gl, axis=-1, keepdims=True)
    ge = jnp.exp(gl - gmax)
    pg = ge / jnp.sum(ge, axis=-1, keepdims=True)
    pg_top = jnp.max(pg, axis=-1, keepdims=True)
    g_sel = jnp.min(jnp.where(pg == pg_top, lane, big), axis=-1, keepdims=True)
    eidx = lane - N_GROUPS
    in_group = jnp.logical_and(eidx >= g_sel * EXPERTS_PER_GROUP, eidx < (g_sel + 1) * EXPERTS_PER_GROUP)
    le = jnp.where(in_group, logits, neg)
    emax = jnp.max(le, axis=-1, keepdims=True)
    ee = jnp.exp(le - emax)
    pe = ee / jnp.sum(ee, axis=-1, keepdims=True)
    p1 = jnp.max(pe, axis=-1, keepdims=True)
    i1 = jnp.min(jnp.where(jnp.logical_and(in_group, pe == p1), lane, big), axis=-1, keepdims=True)
    rest = jnp.logical_and(in_group, lane != i1)
    pe2 = jnp.where(rest, pe, neg)
    p2 = jnp.max(pe2, axis=-1, keepdims=True)
    i2 = jnp.min(jnp.where(jnp.logical_and(rest, pe2 == p2), lane, big), axis=-1, keepdims=True)
    denom = p1 + p2
    wts_ref[...] = jnp.where(lane == 0, pg_top * p1 / denom, jnp.where(lane == 1, pg_top * p2 / denom, 0.0))

    @pl.when(pl.program_id(0) == 0)
    def _():
        carry_ref[...] = jnp.zeros_like(carry_ref)

    oh1 = jnp.where(lane == i1, 1.0, 0.0)
    oh2 = jnp.where(lane == i2, 1.0, 0.0)
    ri = lax.broadcasted_iota(jnp.int32, (tm, tm), 0)
    ci = lax.broadcasted_iota(jnp.int32, (tm, tm), 1)
    below = jnp.where(ri > ci, 1.0, 0.0).astype(BF16)
    prefix = _dot(below, jnp.concatenate([oh1, oh2], axis=1).astype(BF16))
    carry = carry_ref[...]
    c1 = jnp.sum(oh1, axis=0, keepdims=True)
    c2 = jnp.sum(oh2, axis=0, keepdims=True)
    r1 = jnp.sum(oh1 * (carry + prefix[:, :LANES]), axis=-1, keepdims=True)
    r2 = jnp.sum(oh2 * (carry + c1 + prefix[:, LANES:]), axis=-1, keepdims=True)
    total = carry + c1 + c2
    carry_ref[...] = total
    cnt_ref[...] = jnp.broadcast_to(total, cnt_ref.shape).astype(jnp.int32)
    route_ref[...] = jnp.where(lane == 0, i1 - N_GROUPS,
                               jnp.where(lane == 1, i2 - N_GROUPS,
                                         jnp.where(lane == 2, r1.astype(jnp.int32),
                                                   jnp.where(lane == 3, r2.astype(jnp.int32), 0))))


def _token_tiles(dims, latent_only):
    n_batch, n_tok, n_ctx = dims
    tiles, ctx_tiles = n_tok // TOKEN_TILE, n_ctx // TOKEN_TILE
    if not latent_only:
        return n_batch * tiles, lambda i: i
    lat = tiles - ctx_tiles
    return n_batch * lat, lambda i: (i // lat) * tiles + ctx_tiles + i % lat


def _merge_call(hs, mod, ya, yb, yd, p, cb, lw, dims, latent_only):
    n_batch, n_tok, n_ctx = dims
    t, d = hs.shape
    tm = TOKEN_TILE
    sub = 16
    n_sub = t // sub
    n_steps, tile_map = _token_tiles(dims, latent_only)
    t_out = n_steps * tm
    mod_map = _mod_row_map(tm, n_tok, n_ctx, n_batch)
    row_in = lambda w: pl.BlockSpec((tm, w), lambda i: (tile_map(i), 0))
    row = lambda w: pl.BlockSpec((tm, w), lambda i: (i, 0))
    return pl.pallas_call(
        functools.partial(_merge_kernel, n_tok=n_tok, n_ctx=n_ctx, tile_map=tile_map),
        grid=(n_steps,),
        in_specs=[row_in(d),
                  pl.BlockSpec((None, 6, d), lambda i: mod_map(tile_map(i))),
                  row_in(BRANCH_W), row_in(BRANCH_W), row_in(BRANCH_W), row_in(BRANCH_W),
                  pl.BlockSpec((sub, BRANCH_W),
                               lambda i: (jnp.maximum(tile_map(i) * (tm // sub) - 1, 0), 0)),
                  pl.BlockSpec((sub, BRANCH_W),
                               lambda i: (jnp.minimum((tile_map(i) + 1) * (tm // sub), n_sub - 1), 0)),
                  row_in(BRANCH_W),
                  _const_spec((3, BRANCH_W)), _const_spec((4, d, d)), _const_spec((4, d)),
                  _const_spec((4, BRANCH_W, d)), _const_spec((d, d)),
                  _const_spec((d, LANES)), _const_spec((d, LANES)), _const_spec((1, LANES))],
        out_specs=[row(d), pl.BlockSpec((tm * ROW_SUB, LANES), lambda i: (i, 0)),
                   row(LANES), row(LANES), pl.BlockSpec((8, LANES), lambda i: (0, 0))],
        out_shape=[jax.ShapeDtypeStruct((t_out, d), F32), jax.ShapeDtypeStruct((t_out * ROW_SUB, LANES), F32),
                   jax.ShapeDtypeStruct((t_out, LANES), jnp.int32), jax.ShapeDtypeStruct((t_out, LANES), F32),
                   jax.ShapeDtypeStruct((8, LANES), jnp.int32)],
        scratch_shapes=[pltpu.VMEM((1, LANES), F32)],
        compiler_params=_cparams(("arbitrary",)),
        name="merge_route",
    )(hs, mod, ya, yb, yd, p, p, p, cb, lw["w_conv"], lw["w_gate"], lw["b_gate"], lw["w_branch"],
      lw["w_out"], lw["wr_hi"], lw["wr_lo"], lw["br"])


PAD_BITS = tuple(1 << b for b in reversed(range(EXPERT_BLK.bit_length() - 1)))


def _rows_to_tiles(ref, v):
    for ck in range(ROW_SUB):
        ref[pl.ds(ck, v.shape[0], stride=ROW_SUB), :] = v[:, ck * LANES:(ck + 1) * LANES]


def _tiles_to_rows(ref, first, n):
    return jnp.concatenate([ref[pl.ds(first * ROW_SUB + ck, n, stride=ROW_SUB), :] for ck in range(ROW_SUB)],
                           axis=1)


def _tile_of(ref, row, n=1):
    return ref.at[pl.ds(pl.multiple_of(row * ROW_SUB, ROW_SUB), n * ROW_SUB)]


def _tiles_wait(like, sem):
    pltpu.make_async_copy(like, like, sem).wait()


def _dispatch_kernel(pos_ref, pad_start_ref, pad_len_ref, tail_ref, x_ref, xs_hbm, stage, zeros, sems, pad_sem):
    i = pl.program_id(0)
    n = pl.num_programs(0)
    tm = x_ref.shape[0] // ROW_SUB
    slot = i % 2

    def pad_copies(e, fn):
        start = pad_start_ref[e]
        ln = pad_len_ref[e]
        for b in PAD_BITS:
            @pl.when((ln & b) != 0)
            def _():
                off = ln & ~(2 * b - 1)
                fn(pltpu.make_async_copy(_tile_of(zeros, 0, b), _tile_of(xs_hbm, start + off, b), pad_sem))

    def for_each_expert(fn):
        def body(e, carry):
            pad_copies(e, fn)
            return carry
        lax.fori_loop(0, N_EXPERTS, body, 0)

        def tail(j, carry):
            fn(pltpu.make_async_copy(zeros, _tile_of(xs_hbm, j * PAD_BITS[0], PAD_BITS[0]), pad_sem))
            return carry
        lax.fori_loop(tail_ref[0] // PAD_BITS[0], xs_hbm.shape[0] // (ROW_SUB * PAD_BITS[0]), tail, 0)

    @pl.when(i == 0)
    def _():
        zeros[...] = jnp.zeros_like(zeros)
        for_each_expert(lambda cp: cp.start())

    @pl.when(i >= 2)
    def _():
        _tiles_wait(stage, sems.at[slot])

    stage[slot] = x_ref[...]

    def push(r, carry):
        for k in range(2):
            dst = pos_ref[i * (2 * tm) + k * tm + r]
            pltpu.make_async_copy(_tile_of(stage.at[slot], r), _tile_of(xs_hbm, dst),
                                  sems.at[slot]).start(priority=k)
        return carry
    lax.fori_loop(0, tm, push, 0, unroll=8)

    @pl.when(i == n - 1)
    def _():
        _tiles_wait(stage, sems.at[slot])

        @pl.when(n >= 2)
        def _():
            _tiles_wait(stage, sems.at[1 - slot])
        for_each_expert(lambda cp: cp.wait())


def _dispatch_call(pos, pad_start, pad_len, tail_start, x_tiles, cap):
    tm = TOKEN_TILE
    grid_spec = pltpu.PrefetchScalarGridSpec(
        num_scalar_prefetch=4,
        grid=(x_tiles.shape[0] // (tm * ROW_SUB),),
        in_specs=[pl.BlockSpec((tm * ROW_SUB, LANES), lambda i, *_: (i, 0))],
        out_specs=pl.BlockSpec(memory_space=pl.ANY),
        scratch_shapes=[pltpu.VMEM((2, tm * ROW_SUB, LANES), F32), pltpu.VMEM((PAD_BITS[0] * ROW_SUB, LANES), F32),
                        pltpu.SemaphoreType.DMA((2,)), pltpu.SemaphoreType.DMA(())],
    )
    return pl.pallas_call(
        _dispatch_kernel,
        grid_spec=grid_spec,
        out_shape=jax.ShapeDtypeStruct((cap * ROW_SUB, LANES), F32),
        compiler_params=_cparams(("arbitrary",)),
        name="dispatch",
    )(pos, pad_start, pad_len, tail_start, x_tiles)


def _expert_kernel(blk_e_ref, n_used_ref, x_ref, wgu_ref, wdn_ref, y_ref, wgu_bf, wdn_bf):
    i = pl.program_id(0)

    @pl.when(jnp.logical_or(i == 0, blk_e_ref[i] != blk_e_ref[jnp.maximum(i - 1, 0)]))
    def _():
        wgu_bf[...] = wgu_ref[...].astype(BF16)
        wdn_bf[...] = wdn_ref[...].astype(BF16)

    @pl.when(i < n_used_ref[0])
    def _():
        xe = _tiles_to_rows(x_ref, 0, EXPERT_BLK).astype(BF16)
        gu = _dot(xe, wgu_bf[...])
        g = gu[:, :D_EXPERT]
        act = (g * _sigmoid(g) * gu[:, D_EXPERT:]).astype(BF16)
        _rows_to_tiles(y_ref, _dot(act, wdn_bf[...]))

    @pl.when(i >= n_used_ref[0])
    def _():
        y_ref[...] = jnp.zeros_like(y_ref)


def _expert_call(blk_e, n_used, xs, w_gu, w_dn, layer):
    d = ROW_SUB * LANES
    blk = EXPERT_BLK
    used = lambda i, be, nu: (jnp.minimum(i, nu[0] - 1), 0)
    grid_spec = pltpu.PrefetchScalarGridSpec(
        num_scalar_prefetch=2,
        grid=(xs.shape[0] // (blk * ROW_SUB),),
        in_specs=[pl.BlockSpec((blk * ROW_SUB, LANES), used),
                  pl.BlockSpec((None, None, d, 2 * D_EXPERT), lambda i, be, nu: (layer, be[i], 0, 0)),
                  pl.BlockSpec((None, None, D_EXPERT, d), lambda i, be, nu: (layer, be[i], 0, 0))],
        out_specs=pl.BlockSpec((blk * ROW_SUB, LANES), lambda i, be, nu: (i, 0)),
        scratch_shapes=[pltpu.VMEM((d, 2 * D_EXPERT), BF16), pltpu.VMEM((D_EXPERT, d), BF16)],
    )
    return pl.pallas_call(
        _expert_kernel,
        grid_spec=grid_spec,
        out_shape=jax.ShapeDtypeStruct(xs.shape, F32),
        compiler_params=_cparams(("arbitrary",)),
        name="experts",
    )(blk_e, n_used, xs, w_gu, w_dn)


def _combine_kernel(pos_ref, y_hbm, hs1_ref, wts_ref, mod_ref, gain_ref, o_ref, gbuf, sems, *,
                    final):
    i = pl.program_id(0)
    n = pl.num_programs(0)
    tm = hs1_ref.shape[0]
    slot = i % 2

    def start(step, s):
        base = step * (2 * tm)

        def body(r, carry):
            for k in range(2):
                src = pos_ref[base + k * tm + r]
                pltpu.make_async_copy(_tile_of(y_hbm, src), _tile_of(gbuf.at[s], k * tm + r),
                                      sems.at[s]).start(priority=k)
            return carry
        lax.fori_loop(0, tm, body, 0, unroll=8)

    @pl.when(i == 0)
    def _():
        start(0, 0)

    @pl.when(i + 1 < n)
    def _():
        start(i + 1, 1 - slot)

    _tiles_wait(gbuf.at[slot], sems.at[slot])
    wts = wts_ref[...]
    g = gbuf.at[slot]
    f = wts[:, 0:1] * _tiles_to_rows(g, 0, tm) + wts[:, 1:2] * _tiles_to_rows(g, tm, tm)
    out = hs1_ref[...] + f * mod_ref[...][5:6, :]
    if final:
        out = _rms(out) * gain_ref[...]
    o_ref[...] = out


def _combine_call(pos, y, hs1, wts, mod, gain, dims, final):
    n_batch, n_tok, n_ctx = dims
    t, d = hs1.shape
    tm = TOKEN_TILE
    n_steps, tile_map = _token_tiles(dims, final)
    assert n_steps * tm == t
    mod_map = _mod_row_map(tm, n_tok, n_ctx, n_batch)
    grid_spec = pltpu.PrefetchScalarGridSpec(
        num_scalar_prefetch=1,
        grid=(n_steps,),
        in_specs=[pl.BlockSpec(memory_space=pl.ANY),
                  pl.BlockSpec((tm, d), lambda i, pos: (i, 0)),
                  pl.BlockSpec((tm, LANES), lambda i, pos: (i, 0)),
                  pl.BlockSpec((None, 6, d), lambda i, pos: mod_map(tile_map(i))),
                  pl.BlockSpec((1, d), lambda i, pos: (0, 0))],
        out_specs=pl.BlockSpec((tm, d), lambda i, pos: (i, 0)),
        scratch_shapes=[pltpu.VMEM((2, 2 * tm * ROW_SUB, LANES), F32), pltpu.SemaphoreType.DMA((2,))],
    )
    return pl.pallas_call(
        functools.partial(_combine_kernel, final=final),
        grid_spec=grid_spec,
        out_shape=jax.ShapeDtypeStruct((n_steps * tm, d), F32),
        compiler_params=_cparams(("arbitrary",)),
        name="combine_final" if final else "combine",
    )(pos, y, hs1, wts, mod, gain)


def _rope_angles(n_ctx, rows, rot_dim):
    n_freq = rot_dim // 4
    freqs = ROPE_THETA ** (-jnp.arange(n_freq, dtype=F32) / n_freq)
    row = jnp.repeat(jnp.arange(rows, dtype=F32), GRID_W)
    col = (jnp.arange(rows * GRID_W) % GRID_W).astype(F32)
    ang = jnp.concatenate([row[:, None] * freqs, col[:, None] * freqs], axis=-1)
    ang = jnp.concatenate([jnp.zeros((n_ctx, rot_dim // 2), F32), ang], axis=0)
    return jnp.cos(ang), jnp.sin(ang)


def _tables(n_ctx, seq):
    n = n_ctx + seq
    cm, sm = _rope_angles(n_ctx, seq // GRID_W, MLA_ROPE)
    cg, sg = _rope_angles(n_ctx, seq // GRID_W, GQA_HEAD_DIM)
    one, zero = jnp.ones((n, MLA_NOPE), F32), jnp.zeros((n, MLA_NOPE), F32)
    pad = jnp.zeros((n, LANES - MLA_NOPE - MLA_ROPE), F32)
    gi = jnp.arange(BRANCH_W) // GQA_HEAD_DIM
    return {
        "cosm": jnp.concatenate([one, cm, cm, pad], axis=1),
        "sinm": jnp.concatenate([zero, -sm, sm, pad], axis=1),
        "cosg": jnp.concatenate([cg, cg, cg, cg], axis=1),
        "sing": jnp.concatenate([-sg, sg, -sg, sg], axis=1),
        "gmat": jnp.where(gi[:, None] == gi[None, :], 1.0 / GQA_HEAD_DIM, 0.0).astype(BF16),
    }


def _prep_weights(w_in, sgu_gain, w_sgu, b_sgu, mla_q_gain, w_uq, mla_kv_gain, w_ukv, w_conv,
                  gqa_q_gain, gqa_k_gain, w_gate, b_gate, w_branch, w_out, w_group_router,
                  b_group_router, w_expert_router, b_expert_router):
    nl, d, _ = w_in.shape
    zc = lambda n: jnp.zeros((nl, d, n), F32)
    half = MLA_ROPE // 2
    kpe = w_in[..., 896:928]
    gq = w_in[..., 1696:1952].reshape(nl, d, GQA_HEADS, GQA_HEAD_DIM)[:, :, (0, 2, 1, 3), :].reshape(nl, d, BRANCH_W)
    w_in_ext = jnp.concatenate([
        w_in[..., 0:896],
        zc(MLA_NOPE), kpe, zc(LANES - MLA_NOPE - MLA_ROPE),
        zc(MLA_NOPE), kpe[..., half:], kpe[..., :half], zc(LANES - MLA_NOPE - MLA_ROPE),
        w_in[..., 928:1696], gq, w_in[..., 1952:2208]], axis=-1).astype(BF16)

    dq = MLA_NOPE + MLA_ROPE
    uq = w_uq.reshape(nl, -1, MLA_HEADS, dq)
    zq = jnp.zeros(uq.shape[:3] + (LANES - dq,), F32)
    wq_a = jnp.concatenate([uq, zq], axis=-1)
    wq_b = jnp.concatenate([jnp.zeros_like(uq[..., :MLA_NOPE]), uq[..., MLA_NOPE + half:],
                            uq[..., MLA_NOPE:MLA_NOPE + half], zq], axis=-1)
    wq = jnp.concatenate([wq_a.reshape(nl, -1, MLA_HEADS * LANES),
                          wq_b.reshape(nl, -1, MLA_HEADS * LANES)], axis=-1).astype(BF16)

    ukv = w_ukv.reshape(nl, -1, MLA_HEADS, MLA_NOPE + MLA_V)
    k_lay = jnp.concatenate([ukv[..., :MLA_NOPE], jnp.zeros_like(ukv[..., :LANES - MLA_NOPE])], axis=-1)
    v_h = ukv[..., MLA_NOPE:]
    zv = jnp.zeros(v_h.shape[:2] + (LANES,), F32)
    v_lay = jnp.concatenate([v_h[:, :, 0], v_h[:, :, 1], zv, v_h[:, :, 2], v_h[:, :, 3], zv], axis=-1)
    wkv = jnp.concatenate([k_lay.reshape(nl, -1, MLA_HEADS * LANES), v_lay], axis=-1).astype(BF16)

    wb = w_branch
    wb_d = wb[:, 3].reshape(nl, GQA_HEADS, GQA_HEAD_DIM, d)[:, (0, 2, 1, 3)].reshape(nl, BRANCH_W, d)
    w_branch_p = jnp.concatenate([wb[:, :3], wb_d[:, None]], axis=1).astype(BF16)

    wr = jnp.concatenate([w_group_router, w_expert_router,
                          jnp.zeros((nl, d, LANES - N_GROUPS - N_EXPERTS), F32)], axis=-1)
    wr_hi = wr.astype(BF16)
    wr_lo = (wr - wr_hi.astype(F32)).astype(BF16)
    br = jnp.concatenate([b_group_router, b_expert_router,
                          jnp.zeros((nl, LANES - N_GROUPS - N_EXPERTS), F32)], axis=-1)[:, None, :]

    tile = lambda g, reps: jnp.concatenate([g] * reps, axis=-1)[:, None, :]
    return {
        "w_in": w_in_ext,
        "sgu_gain": sgu_gain[:, None, :],
        "w_sgu": w_sgu.astype(BF16),
        "b_sgu": jnp.repeat(b_sgu.transpose(0, 2, 1), BRANCH_W // SGU_GROUPS, axis=-1),
        "q_gain": mla_q_gain[:, None, :], "wq": wq,
        "kv_gain": mla_kv_gain[:, None, :], "wkv": wkv,
        "gq_gain": tile(gqa_q_gain, 4), "gk_gain": tile(gqa_k_gain, 2),
        "w_conv": w_conv,
        "w_gate": w_gate.astype(BF16), "b_gate": b_gate,
        "w_branch": w_branch_p, "w_out": w_out.astype(BF16),
        "wr_hi": wr_hi, "wr_lo": wr_lo, "br": br,
    }


def _dispatch_plan(route, cnt, n_tiles, cap):
    blk = EXPERT_BLK
    tm = route.shape[0] // n_tiles
    counts = cnt[0, N_GROUPS:N_GROUPS + N_EXPERTS]
    padded = (counts + blk - 1) // blk * blk
    pend = jnp.cumsum(padded)
    pstart = pend - padded
    eid, rank = route[:, 0:2], route[:, 2:4]
    onehot = eid[:, :, None] == jnp.arange(N_EXPERTS, dtype=jnp.int32)
    dest = jnp.sum(jnp.where(onehot, pstart, 0), axis=-1) + rank
    pos = dest.reshape(n_tiles, tm, 2).transpose(0, 2, 1).reshape(-1)
    n_used = pend[-1] // blk
    blk_first = jnp.minimum(jnp.arange(cap // blk, dtype=jnp.int32), n_used - 1) * blk
    blk_e = jnp.sum((blk_first[:, None] >= pend[None, :]).astype(jnp.int32), axis=1)
    blk_e = jnp.minimum(blk_e, N_EXPERTS - 1)
    return pos, pstart + counts, padded - counts, blk_e, n_used.reshape(1)


def kernel(x, c, ctx, c_ctx, w_ada, b_ada, w_in, sgu_gain, w_sgu, b_sgu, mla_q_gain, w_uq, mla_kv_gain, w_ukv, w_conv, gqa_q_gain, gqa_k_gain, w_gate, b_gate, w_branch, w_out, w_group_router, b_group_router, w_expert_router, b_expert_router, w_expert_gate_up, w_expert_down, final_gain):
    n_batch, seq, d = x.shape
    n_ctx = ctx.shape[1]
    n_tok = n_ctx + seq
    t = n_batch * n_tok
    depth = w_in.shape[0]
    tm = TOKEN_TILE
    assert n_ctx % tm == 0 and seq % tm == 0 and seq % GRID_W == 0 and tm % SGU_CHUNK == 0
    dims = (n_batch, n_tok, n_ctx)

    hs = jnp.concatenate([ctx, x], axis=1).reshape(t, d)
    tabs = _tables(n_ctx, seq)
    rows = -(-(n_batch + 1) // 8) * 8
    cvec = jnp.concatenate([c, c_ctx[None, :], jnp.zeros((rows - n_batch - 1, d), F32)], axis=0)
    mods = _ada_call(cvec, w_ada, b_ada).reshape(depth, rows, 6, d)
    weights = _prep_weights(w_in, sgu_gain, w_sgu, b_sgu, mla_q_gain, w_uq, mla_kv_gain, w_ukv, w_conv,
                            gqa_q_gain, gqa_k_gain, w_gate, b_gate, w_branch, w_out, w_group_router,
                            b_group_router, w_expert_router, b_expert_router)
    gain = final_gain[None, :]

    for l in range(depth):
        last = l == depth - 1
        lw = {k: v[l] for k, v in weights.items()}
        mod = mods[l]
        ya, qm, km, vm, p, cb, qg, kg, vg = _in_call(hs, mod, tabs, lw, dims)
        yb = _attn_call(_mla_attn_kernel, "mla_attn", qm, km, vm, dims)
        yd = _attn_call(_gqa_attn_kernel, "gqa_attn", qg, kg, vg, dims)
        hs1, hn2, route, wts, cnt = _merge_call(hs, mod, ya, yb, yd, p, cb, lw, dims, latent_only=last)
        t_moe = hs1.shape[0]
        cap = -(-2 * t_moe // EXPERT_BLK) * EXPERT_BLK + N_EXPERTS * EXPERT_BLK
        pos, pad_start, pad_len, blk_e, n_used = _dispatch_plan(route, cnt, t_moe // tm, cap)
        xs = _dispatch_call(pos, pad_start, pad_len, n_used * EXPERT_BLK, hn2, cap)
        y = _expert_call(blk_e, n_used, xs, w_expert_gate_up, w_expert_down, l)
        hs = _combine_call(pos, y, hs1, wts, mod, gain, dims, final=last)
    return hs.reshape(n_batch, seq, d)
```

```python
import functools
import math

import jax
import jax.numpy as jnp
from jax import lax
from jax.experimental import pallas as pl
from jax.experimental.pallas import tpu as pltpu

F32 = jnp.float32
BF16 = jnp.bfloat16

GRID_W = 64
ROPE_THETA = 10000.0
NORM_EPS = 1e-6
BRANCH_W = 256
SGU_CHUNK = 128
SGU_GROUPS = 4
MLA_HEADS = 4
MLA_NOPE = 64
MLA_ROPE = 32
MLA_V = 64
GQA_HEADS = 4
GQA_HEAD_DIM = 64
N_GROUPS = 4
EXPERTS_PER_GROUP = 8
N_EXPERTS = N_GROUPS * EXPERTS_PER_GROUP
D_EXPERT = 256
LANES = 128
ROW_SUB = 8
TOKEN_TILE = 256
EXPERT_BLK = 256
VMEM_LIMIT = 56 * 1024 * 1024
LOG2E = 1.4426950408889634

C_AU, C_AV, C_MQ, C_MKV, C_KPA, C_KPB, C_CB, C_CC, C_CX, C_GQ, C_GK, C_GV, C_END = (
    0, 256, 512, 768, 896, 1024, 1152, 1408, 1664, 1920, 2176, 2304, 2432)


def _cparams(sem):
    return pltpu.CompilerParams(dimension_semantics=sem, vmem_limit_bytes=VMEM_LIMIT)


def _dot(a, b):
    return jnp.dot(a, b, preferred_element_type=F32)


def _dot_nt(a, b):
    return lax.dot_general(a, b, (((1,), (1,)), ((), ())), preferred_element_type=F32)


def _split_bf16(v):
    hi = v.astype(BF16)
    lo = (v - hi.astype(F32)).astype(BF16)
    return hi, lo


def _gelu(v):
    return 0.5 * v * (1.0 + jnp.tanh(math.sqrt(2.0 / math.pi) * (v + 0.044715 * (v * v * v))))


def _sigmoid(v):
    return 0.5 * jnp.tanh(0.5 * v) + 0.5


def _rms(v):
    return v * lax.rsqrt(jnp.mean(v * v, axis=-1, keepdims=True) + NORM_EPS)


def _const_spec(shape):
    nd = len(shape)
    return pl.BlockSpec(shape, lambda *_: (0,) * nd)


def _ada_kernel(c_ref, w_ref, b_ref, o_ref):
    cv = c_ref[...]
    s = cv * _sigmoid(cv)
    s_hi, s_lo = _split_bf16(s)
    w_hi, w_lo = _split_bf16(w_ref[...])
    o_ref[...] = _dot(s_hi, w_hi) + _dot(s_lo, w_hi) + _dot(s_hi, w_lo) + b_ref[...]


def _ada_call(cvec, w_ada, b_ada):
    n_layers, d, d6 = w_ada.shape
    rows = cvec.shape[0]
    tn = 1536
    return pl.pallas_call(
        _ada_kernel,
        grid=(n_layers, d6 // tn),
        in_specs=[pl.BlockSpec((rows, d), lambda l, j: (0, 0)),
                  pl.BlockSpec((None, d, tn), lambda l, j: (l, 0, j)),
                  pl.BlockSpec((None, 1, tn), lambda l, j: (l, 0, j))],
        out_specs=pl.BlockSpec((None, rows, tn), lambda l, j: (l, 0, j)),
        out_shape=jax.ShapeDtypeStruct((n_layers, rows, d6), F32),
        compiler_params=_cparams(("arbitrary", "arbitrary")),
        name="ada_mod",
    )(cvec, w_ada, b_ada.reshape(n_layers, 1, d6))


def _in_kernel(hs_ref, mod_ref, cosm_ref, sinm_ref, cosg_ref, sing_ref, w_in_ref, sgu_gain_ref,
               w_sgu_ref, b_sgu_ref, qgain_ref, wq_ref, kvgain_ref, wkv_ref, gqgain_ref, gkgain_ref,
               gmat_ref,
               ya_ref, qm_ref, km_ref, vm_ref, p_ref, cb_ref, qg_ref, kg_ref, vg_ref):
    tm = hs_ref.shape[0]
    mod = mod_ref[...]
    hn = _rms(hs_ref[...]) * (1.0 + mod[1:2, :]) + mod[0:1, :]
    z = _dot(hn.astype(BF16), w_in_ref[...])

    av = _rms(_gelu(z[:, C_AV:C_MQ])) * sgu_gain_ref[...]
    av = av.astype(BF16)
    grp = lax.broadcasted_iota(jnp.int32, (SGU_CHUNK, BRANCH_W), 1) // (BRANCH_W // SGU_GROUPS)
    chunks = []
    for ci in range(tm // SGU_CHUNK):
        vc = av[ci * SGU_CHUNK:(ci + 1) * SGU_CHUNK, :]
        m = b_sgu_ref[...]
        for g in range(SGU_GROUPS):
            m = m + jnp.where(grp == g, _dot(w_sgu_ref[g], vc), 0.0)
        chunks.append(m)
    mixed = jnp.concatenate(chunks, axis=0)
    ya_ref[...] = (_gelu(z[:, C_AU:C_AV]) * mixed).astype(BF16)

    cosm = cosm_ref[...]
    sinm = sinm_ref[...]
    cos4 = jnp.concatenate([cosm] * MLA_HEADS, axis=1)
    sin4 = jnp.concatenate([sinm] * MLA_HEADS, axis=1)
    nq = (_rms(z[:, C_MQ:C_MKV]) * qgain_ref[...]).astype(BF16)
    qab = _dot(nq, wq_ref[...])
    hw = MLA_HEADS * LANES
    q_scale = (MLA_NOPE + MLA_ROPE) ** -0.5 * LOG2E
    qm_ref[...] = ((qab[:, :hw] * cos4 + qab[:, hw:] * sin4) * q_scale).astype(BF16)
    nkv = (_rms(z[:, C_MKV:C_KPA]) * kvgain_ref[...]).astype(BF16)
    kv = _dot(nkv, wkv_ref[...])
    kpe = z[:, C_KPA:C_KPB] * cosm + z[:, C_KPB:C_CB] * sinm
    km_ref[...] = (kv[:, :hw] + jnp.concatenate([kpe] * MLA_HEADS, axis=1)).astype(BF16)
    lane = lax.broadcasted_iota(jnp.int32, (tm, hw), 1)
    ones_col = jnp.where(lane % (2 * LANES) == LANES, 1.0, 0.0)
    vm_ref[...] = (kv[:, hw:] + ones_col).astype(BF16)

    p_ref[...] = (z[:, C_CC:C_CX] * z[:, C_CX:C_GQ]).astype(BF16)
    cb_ref[...] = z[:, C_CB:C_CC].astype(BF16)

    cosg = cosg_ref[...]
    sing = sing_ref[...]
    gmat = gmat_ref[...]

    def head_norm_rope(v, gain, cos_t, sin_t, scale):
        w = v.shape[1]
        sq_hi, sq_lo = _split_bf16(v * v)
        ms = _dot(sq_hi, gmat[:w, :w]) + _dot(sq_lo, gmat[:w, :w])
        y = v * lax.rsqrt(ms + NORM_EPS) * gain
        half = GQA_HEAD_DIM // 2
        ln = lax.broadcasted_iota(jnp.int32, v.shape, 1) % GQA_HEAD_DIM
        partner = jnp.where(ln < half, pltpu.roll(y, w - half, axis=1), pltpu.roll(y, half, axis=1))
        return ((y * cos_t + partner * sin_t) * scale).astype(BF16)

    cos2 = jnp.concatenate([cosg, cosg], axis=1)
    sin2 = jnp.concatenate([sing, sing], axis=1)
    qg_ref[...] = head_norm_rope(z[:, C_GQ:C_GK], gqgain_ref[...], cos2, sin2,
                                 GQA_HEAD_DIM ** -0.5 * LOG2E)
    kg_ref[...] = head_norm_rope(z[:, C_GK:C_GV], gkgain_ref[...], cosg, sing, 1.0)
    lane2 = lax.broadcasted_iota(jnp.int32, (tm, LANES), 1)
    vg_ref[...] = jnp.concatenate([z[:, C_GV:C_END], jnp.where(lane2 == 0, 1.0, 0.0)], axis=1).astype(BF16)


def _mod_row_map(tm, n_tok, n_ctx, n_batch):
    def index_map(i):
        start = i * tm
        return (jnp.where(start % n_tok < n_ctx, n_batch, start // n_tok), 0, 0)
    return index_map


def _in_call(hs, mod, tabs, lw, dims):
    n_batch, n_tok, n_ctx = dims
    t, d = hs.shape
    tm = TOKEN_TILE
    tiles_per_seq = n_tok // tm
    row = lambda w: pl.BlockSpec((tm, w), lambda i: (i, 0))
    tab = pl.BlockSpec((tm, LANES), lambda i: (i % tiles_per_seq, 0))
    hw = MLA_HEADS * LANES
    outs = [(BRANCH_W, "ya"), (hw, "qm"), (hw, "km"), (hw, "vm"), (BRANCH_W, "p"), (BRANCH_W, "cb"),
            (BRANCH_W, "qg"), (LANES, "kg"), (2 * LANES, "vg")]
    return pl.pallas_call(
        _in_kernel,
        grid=(t // tm,),
        in_specs=[row(d),
                  pl.BlockSpec((None, 6, d), _mod_row_map(tm, n_tok, n_ctx, n_batch)),
                  tab, tab, tab, tab,
                  _const_spec((d, C_END)), _const_spec((1, BRANCH_W)),
                  _const_spec((SGU_GROUPS, SGU_CHUNK, SGU_CHUNK)), _const_spec((SGU_CHUNK, BRANCH_W)),
                  _const_spec((1, BRANCH_W)), _const_spec((BRANCH_W, 2 * hw)),
                  _const_spec((1, LANES)), _const_spec((LANES, 2 * hw)),
                  _const_spec((1, BRANCH_W)), _const_spec((1, LANES)),
                  _const_spec((BRANCH_W, BRANCH_W))],
        out_specs=[row(w) for w, _ in outs],
        out_shape=[jax.ShapeDtypeStruct((t, w), BF16) for w, _ in outs],
        compiler_params=_cparams(("arbitrary",)),
        name="in_proj",
    )(hs, mod, tabs["cosm"], tabs["sinm"], tabs["cosg"], tabs["sing"], lw["w_in"], lw["sgu_gain"],
      lw["w_sgu"], lw["b_sgu"], lw["q_gain"], lw["wq"], lw["kv_gain"], lw["wkv"], lw["gq_gain"],
      lw["gk_gain"], tabs["gmat"])


def _softmax_pv(s, v):
    m = jnp.max(s, axis=-1, keepdims=True)
    p = jnp.exp2(s - m).astype(BF16)
    r = _dot(p, v)
    return r[:, :LANES] * (1.0 / r[:, LANES:LANES + 1])


def _mla_attn_kernel(q_ref, k_ref, v_ref, o_ref, *, ctx_tiles, n_ctx):
    def run(nk):
        lane = lax.broadcasted_iota(jnp.int32, (q_ref.shape[0], LANES), 1)
        outs = []
        for pair in range(MLA_HEADS // 2):
            v = v_ref[0:nk, pair * 2 * LANES:(pair + 1) * 2 * LANES]
            res = []
            for hh in range(2):
                h = pair * 2 + hh
                s = _dot_nt(q_ref[:, h * LANES:(h + 1) * LANES], k_ref[0:nk, h * LANES:(h + 1) * LANES])
                res.append(_softmax_pv(s, v))
            outs.append(jnp.where(lane < MLA_V, res[0], res[1]))
        o_ref[...] = jnp.concatenate(outs, axis=1).astype(BF16)

    j = pl.program_id(1)
    pl.when(j < ctx_tiles)(lambda: run(n_ctx))
    pl.when(j >= ctx_tiles)(lambda: run(k_ref.shape[0]))


def _gqa_attn_kernel(q_ref, k_ref, v_ref, o_ref, *, ctx_tiles, n_ctx):
    def run(nk):
        lane = lax.broadcasted_iota(jnp.int32, (q_ref.shape[0], LANES), 1)
        k = k_ref[0:nk, :]
        v = v_ref[0:nk, :]
        outs = []
        for grp in range(GQA_HEADS // 2):
            q = q_ref[:, grp * LANES:(grp + 1) * LANES]
            zero = jnp.zeros_like(q)
            r_lo = _softmax_pv(_dot_nt(jnp.where(lane < GQA_HEAD_DIM, q, zero), k), v)
            r_hi = _softmax_pv(_dot_nt(jnp.where(lane >= GQA_HEAD_DIM, q, zero), k), v)
            outs.append(jnp.where(lane < GQA_HEAD_DIM, r_lo, r_hi))
        o_ref[...] = jnp.concatenate(outs, axis=1).astype(BF16)

    j = pl.program_id(1)
    pl.when(j < ctx_tiles)(lambda: run(n_ctx))
    pl.when(j >= ctx_tiles)(lambda: run(k_ref.shape[0]))


def _attn_call(body, name, q, k, v, dims):
    n_batch, n_tok, n_ctx = dims
    tq = TOKEN_TILE
    tiles = n_tok // tq
    return pl.pallas_call(
        functools.partial(body, ctx_tiles=n_ctx // tq, n_ctx=n_ctx),
        grid=(n_batch, tiles),
        in_specs=[pl.BlockSpec((tq, q.shape[1]), lambda b, j: (b * tiles + j, 0)),
                  pl.BlockSpec((n_tok, k.shape[1]), lambda b, j: (b, 0)),
                  pl.BlockSpec((n_tok, v.shape[1]), lambda b, j: (b, 0))],
        out_specs=pl.BlockSpec((tq, BRANCH_W), lambda b, j: (b * tiles + j, 0)),
        out_shape=jax.ShapeDtypeStruct((q.shape[0], BRANCH_W), BF16),
        compiler_params=_cparams(("arbitrary", "arbitrary")),
        name=name,
    )(q, k, v)


def _merge_kernel(hs_ref, mod_ref, ya_ref, yb_ref, yd_ref, p_ref, pprev_ref, pnext_ref, cb_ref,
                  wconv_ref, wgate_ref, bgate_ref, wbranch_ref, wout_ref, wr_hi_ref, wr_lo_ref, br_ref,
                  hs1_ref, hn2_ref, logit_ref, *, n_tok, n_ctx, tile_map):
    tm = hs_ref.shape[0]
    mod = mod_ref[...]
    hs = hs_ref[...]
    hn = (_rms(hs) * (1.0 + mod[1:2, :]) + mod[0:1, :]).astype(BF16)

    start = (tile_map(pl.program_id(0)) * tm) % n_tok
    prev_ok = jnp.logical_and(start != 0, start != n_ctx)
    next_ok = jnp.logical_and(start + tm != n_ctx, start + tm != n_tok)
    p = p_ref[...].astype(F32)
    halo_prev = jnp.where(prev_ok, pprev_ref[...].astype(F32)[15:16, :], 0.0)
    halo_next = jnp.where(next_ok, pnext_ref[...].astype(F32)[0:1, :], 0.0)
    rowi = lax.broadcasted_iota(jnp.int32, p.shape, 0)
    prev = jnp.where(rowi == 0, halo_prev, pltpu.roll(p, 1, axis=0))
    nxt = jnp.where(rowi == tm - 1, halo_next, pltpu.roll(p, tm - 1, axis=0))
    wc = wconv_ref[...]
    yc = cb_ref[...].astype(F32) * (wc[0:1, :] * prev + wc[1:2, :] * p + wc[2:3, :] * nxt)

    branches = (ya_ref[...], yb_ref[...], yc.astype(BF16), yd_ref[...])
    merged = None
    for i, y in enumerate(branches):
        half_gate = _dot(hn, wgate_ref[i]) + bgate_ref[i:i + 1, :]
        term = (jnp.tanh(half_gate) + 1.0) * _dot(y, wbranch_ref[i])
        merged = term if merged is None else merged + term
    hs1 = hs + _dot(merged.astype(BF16), wout_ref[...]) * mod[2:3, :]
    hs1_ref[...] = hs1

    hn2 = _rms(hs1) * (1.0 + mod[4:5, :]) + mod[3:4, :]
    _rows_to_tiles(hn2_ref, hn2)

    x_hi, x_lo = _split_bf16(hn2)
    logit_ref[...] = (_dot(x_hi, wr_hi_ref[...]) + _dot(x_lo, wr_hi_ref[...]) + _dot(x_hi, wr_lo_ref[...])
                      + br_ref[...])


ROUTE_SUB = 256


def _route_kernel(logit_ref, route_ref, wts_ref, cnt_ref, carry_ref):
    tm = logit_ref.shape[0]
    lt = logit_ref[...].T
    sub = lax.broadcasted_iota(jnp.int32, (EXPERTS_PER_GROUP, tm), 0)
    neg = -jnp.inf
    top = lambda v: jnp.max(v, axis=0, keepdims=True)
    first = lambda hit: jnp.min(jnp.where(hit, sub, EXPERTS_PER_GROUP), axis=0, keepdims=True)
    gl = jnp.where(sub < N_GROUPS, lt[0:EXPERTS_PER_GROUP, :], neg)
    ge = jnp.exp(gl - top(gl))
    pg = ge / jnp.sum(ge, axis=0, keepdims=True)
    pg_top = top(pg)
    g_sel = first(pg == pg_top)
    le = lt[EXPERTS_PER_GROUP:2 * EXPERTS_PER_GROUP, :]
    for g in range(1, N_GROUPS):
        le = jnp.where(g_sel == g, lt[(g + 1) * EXPERTS_PER_GROUP:(g + 2) * EXPERTS_PER_GROUP, :], le)
    ee = jnp.exp(le - top(le))
    pe = ee / jnp.sum(ee, axis=0, keepdims=True)
    p1 = top(pe)
    i1 = first(pe == p1)
    pe2 = jnp.where(sub != i1, pe, neg)
    p2 = top(pe2)
    i2 = first(jnp.logical_and(sub != i1, pe2 == p2))
    denom = p1 + p2
    e1 = g_sel * EXPERTS_PER_GROUP + i1
    e2 = g_sel * EXPERTS_PER_GROUP + i2
    wts_ref[...] = jnp.where(sub == 0, pg_top * p1 / denom, jnp.where(sub == 1, pg_top * p2 / denom, 0.0))

    @pl.when(pl.program_id(0) == 0)
    def _():
        carry_ref[...] = jnp.zeros_like(carry_ref)

    ts = ROUTE_SUB
    erow = lax.broadcasted_iota(jnp.int32, (N_EXPERTS, ts), 0)
    ri = lax.broadcasted_iota(jnp.int32, (ts, ts + LANES), 0)
    ci = lax.broadcasted_iota(jnp.int32, (ts, ts + LANES), 1)
    before = jnp.where(jnp.logical_or(ri < ci, ci >= ts), 1.0, 0.0).astype(BF16)
    carry = carry_ref[...]
    r1s, r2s = [], []
    for s in range(tm // ts):
        oh1 = jnp.where(erow == e1[:, s * ts:(s + 1) * ts], 1.0, 0.0)
        oh2 = jnp.where(erow == e2[:, s * ts:(s + 1) * ts], 1.0, 0.0)
        pref = _dot(jnp.concatenate([oh1, oh2], axis=0).astype(BF16), before)
        c1 = pref[:N_EXPERTS, ts:]
        c2 = pref[N_EXPERTS:, ts:]
        r1s.append(jnp.sum(oh1 * (carry[:, 0:1] + pref[:N_EXPERTS, :ts]), axis=0, keepdims=True))
        r2s.append(jnp.sum(oh2 * ((carry + c1)[:, 0:1] + pref[N_EXPERTS:, :ts]), axis=0, keepdims=True))
        carry = carry + c1 + c2
    r1 = jnp.concatenate(r1s, axis=1)
    r2 = jnp.concatenate(r2s, axis=1)
    carry_ref[...] = carry
    cnt_ref[...] = carry.astype(jnp.int32)
    route_ref[...] = jnp.where(sub == 0, e1, jnp.where(sub == 1, e2,
                               jnp.where(sub == 2, r1.astype(jnp.int32),
                                         jnp.where(sub == 3, r2.astype(jnp.int32), 0))))


def _route_call(logits):
    t = logits.shape[0]
    tr = 1024 if t % 1024 == 0 else ROUTE_SUB
    return pl.pallas_call(
        _route_kernel,
        grid=(t // tr,),
        in_specs=[pl.BlockSpec((tr, LANES), lambda i: (i, 0))],
        out_specs=[pl.BlockSpec((8, tr), lambda i: (0, i)), pl.BlockSpec((8, tr), lambda i: (0, i)),
                   pl.BlockSpec((N_EXPERTS, LANES), lambda i: (0, 0))],
        out_shape=[jax.ShapeDtypeStruct((8, t), jnp.int32), jax.ShapeDtypeStruct((8, t), F32),
                   jax.ShapeDtypeStruct((N_EXPERTS, LANES), jnp.int32)],
        scratch_shapes=[pltpu.VMEM((N_EXPERTS, LANES), F32)],
        compiler_params=_cparams(("arbitrary",)),
        name="route",
    )(logits)


def _token_tiles(dims, latent_only):
    n_batch, n_tok, n_ctx = dims
    tiles, ctx_tiles = n_tok // TOKEN_TILE, n_ctx // TOKEN_TILE
    if not latent_only:
        return n_batch * tiles, lambda i: i
    lat = tiles - ctx_tiles
    return n_batch * lat, lambda i: (i // lat) * tiles + ctx_tiles + i % lat


def _merge_call(hs, mod, ya, yb, yd, p, cb, lw, dims, latent_only):
    n_batch, n_tok, n_ctx = dims
    t, d = hs.shape
    tm = TOKEN_TILE
    sub = 16
    n_sub = t // sub
    n_steps, tile_map = _token_tiles(dims, latent_only)
    t_out = n_steps * tm
    mod_map = _mod_row_map(tm, n_tok, n_ctx, n_batch)
    row_in = lambda w: pl.BlockSpec((tm, w), lambda i: (tile_map(i), 0))
    row = lambda w: pl.BlockSpec((tm, w), lambda i: (i, 0))
    return pl.pallas_call(
        functools.partial(_merge_kernel, n_tok=n_tok, n_ctx=n_ctx, tile_map=tile_map),
        grid=(n_steps,),
        in_specs=[row_in(d),
                  pl.BlockSpec((None, 6, d), lambda i: mod_map(tile_map(i))),
                  row_in(BRANCH_W), row_in(BRANCH_W), row_in(BRANCH_W), row_in(BRANCH_W),
                  pl.BlockSpec((sub, BRANCH_W),
                               lambda i: (jnp.maximum(tile_map(i) * (tm // sub) - 1, 0), 0)),
                  pl.BlockSpec((sub, BRANCH_W),
                               lambda i: (jnp.minimum((tile_map(i) + 1) * (tm // sub), n_sub - 1), 0)),
                  row_in(BRANCH_W),
                  _const_spec((3, BRANCH_W)), _const_spec((4, d, d)), _const_spec((4, d)),
                  _const_spec((4, BRANCH_W, d)), _const_spec((d, d)),
                  _const_spec((d, LANES)), _const_spec((d, LANES)), _const_spec((1, LANES))],
        out_specs=[row(d), pl.BlockSpec((tm * ROW_SUB, LANES), lambda i: (i, 0)), row(LANES)],
        out_shape=[jax.ShapeDtypeStruct((t_out, d), F32), jax.ShapeDtypeStruct((t_out * ROW_SUB, LANES), F32),
                   jax.ShapeDtypeStruct((t_out, LANES), F32)],
        compiler_params=_cparams(("arbitrary",)),
        name="merge_route",
    )(hs, mod, ya, yb, yd, p, p, p, cb, lw["w_conv"], lw["w_gate"], lw["b_gate"], lw["w_branch"],
      lw["w_out"], lw["wr_hi"], lw["wr_lo"], lw["br"])


PAD_BITS = tuple(1 << b for b in reversed(range(EXPERT_BLK.bit_length() - 1)))


def _rows_to_tiles(ref, v):
    for ck in range(ROW_SUB):
        ref[pl.ds(ck, v.shape[0], stride=ROW_SUB), :] = v[:, ck * LANES:(ck + 1) * LANES]


def _tiles_to_rows(ref, first, n):
    return jnp.concatenate([ref[pl.ds(first * ROW_SUB + ck, n, stride=ROW_SUB), :] for ck in range(ROW_SUB)],
                           axis=1)


def _tile_of(ref, row, n=1):
    return ref.at[pl.ds(pl.multiple_of(row * ROW_SUB, ROW_SUB), n * ROW_SUB)]


def _tiles_wait(like, sem):
    pltpu.make_async_copy(like, like, sem).wait()


def _dispatch_kernel(pos_ref, pad_start_ref, pad_len_ref, tail_ref, x_ref, xs_hbm, stage, zeros, sems, pad_sem):
    i = pl.program_id(0)
    n = pl.num_programs(0)
    tm = x_ref.shape[0] // ROW_SUB
    slot = i % 2

    def pad_copies(e, fn):
        start = pad_start_ref[e]
        ln = pad_len_ref[e]
        for b in PAD_BITS:
            @pl.when((ln & b) != 0)
            def _():
                off = ln & ~(2 * b - 1)
                fn(pltpu.make_async_copy(_tile_of(zeros, 0, b), _tile_of(xs_hbm, start + off, b), pad_sem))

    def for_each_expert(fn):
        def body(e, carry):
            pad_copies(e, fn)
            return carry
        lax.fori_loop(0, N_EXPERTS, body, 0)

        def tail(j, carry):
            fn(pltpu.make_async_copy(zeros, _tile_of(xs_hbm, j * PAD_BITS[0], PAD_BITS[0]), pad_sem))
            return carry
        lax.fori_loop(tail_ref[0] // PAD_BITS[0], xs_hbm.shape[0] // (ROW_SUB * PAD_BITS[0]), tail, 0)

    @pl.when(i == 0)
    def _():
        zeros[...] = jnp.zeros_like(zeros)
        for_each_expert(lambda cp: cp.start())

    @pl.when(i >= 2)
    def _():
        _tiles_wait(stage, sems.at[slot])

    stage[slot] = x_ref[...]

    def push(r, carry):
        for k in range(2):
            dst = pos_ref[i * (2 * tm) + k * tm + r]
            pltpu.make_async_copy(_tile_of(stage.at[slot], r), _tile_of(xs_hbm, dst),
                                  sems.at[slot]).start(priority=k)
        return carry
    lax.fori_loop(0, tm, push, 0, unroll=8)

    @pl.when(i == n - 1)
    def _():
        _tiles_wait(stage, sems.at[slot])

        @pl.when(n >= 2)
        def _():
            _tiles_wait(stage, sems.at[1 - slot])
        for_each_expert(lambda cp: cp.wait())


def _dispatch_call(pos, pad_start, pad_len, tail_start, x_tiles, cap):
    tm = TOKEN_TILE
    grid_spec = pltpu.PrefetchScalarGridSpec(
        num_scalar_prefetch=4,
        grid=(x_tiles.shape[0] // (tm * ROW_SUB),),
        in_specs=[pl.BlockSpec((tm * ROW_SUB, LANES), lambda i, *_: (i, 0))],
        out_specs=pl.BlockSpec(memory_space=pl.ANY),
        scratch_shapes=[pltpu.VMEM((2, tm * ROW_SUB, LANES), F32), pltpu.VMEM((PAD_BITS[0] * ROW_SUB, LANES), F32),
                        pltpu.SemaphoreType.DMA((2,)), pltpu.SemaphoreType.DMA(())],
    )
    return pl.pallas_call(
        _dispatch_kernel,
        grid_spec=grid_spec,
        out_shape=jax.ShapeDtypeStruct((cap * ROW_SUB, LANES), F32),
        compiler_params=_cparams(("arbitrary",)),
        name="dispatch",
    )(pos, pad_start, pad_len, tail_start, x_tiles)


def _expert_kernel(blk_e_ref, n_used_ref, x_ref, wgu_ref, wdn_ref, y_ref, wgu_bf, wdn_bf):
    i = pl.program_id(0)

    @pl.when(jnp.logical_or(i == 0, blk_e_ref[i] != blk_e_ref[jnp.maximum(i - 1, 0)]))
    def _():
        wgu_bf[...] = wgu_ref[...].astype(BF16)
        wdn_bf[...] = wdn_ref[...].astype(BF16)

    @pl.when(i < n_used_ref[0])
    def _():
        xe = _tiles_to_rows(x_ref, 0, EXPERT_BLK).astype(BF16)
        gu = _dot(xe, wgu_bf[...])
        g = gu[:, :D_EXPERT]
        act = (g * _sigmoid(g) * gu[:, D_EXPERT:]).astype(BF16)
        _rows_to_tiles(y_ref, _dot(act, wdn_bf[...]))

    @pl.when(i >= n_used_ref[0])
    def _():
        y_ref[...] = jnp.zeros_like(y_ref)


def _expert_call(blk_e, n_used, xs, w_gu, w_dn, layer):
    d = ROW_SUB * LANES
    blk = EXPERT_BLK
    used = lambda i, be, nu: (jnp.minimum(i, nu[0] - 1), 0)
    grid_spec = pltpu.PrefetchScalarGridSpec(
        num_scalar_prefetch=2,
        grid=(xs.shape[0] // (blk * ROW_SUB),),
        in_specs=[pl.BlockSpec((blk * ROW_SUB, LANES), used),
                  pl.BlockSpec((None, None, d, 2 * D_EXPERT), lambda i, be, nu: (layer, be[i], 0, 0)),
                  pl.BlockSpec((None, None, D_EXPERT, d), lambda i, be, nu: (layer, be[i], 0, 0))],
        out_specs=pl.BlockSpec((blk * ROW_SUB, LANES), lambda i, be, nu: (i, 0)),
        scratch_shapes=[pltpu.VMEM((d, 2 * D_EXPERT), BF16), pltpu.VMEM((D_EXPERT, d), BF16)],
    )
    return pl.pallas_call(
        _expert_kernel,
        grid_spec=grid_spec,
        out_shape=jax.ShapeDtypeStruct(xs.shape, F32),
        compiler_params=_cparams(("arbitrary",)),
        name="experts",
    )(blk_e, n_used, xs, w_gu, w_dn)


def _combine_kernel(pos_ref, y_hbm, hs1_ref, wts_ref, mod_ref, gain_ref, o_ref, gbuf, sems, *,
                    final):
    i = pl.program_id(0)
    n = pl.num_programs(0)
    tm = hs1_ref.shape[0]
    slot = i % 2

    def start(step, s):
        base = step * (2 * tm)

        def body(r, carry):
            for k in range(2):
                src = pos_ref[base + k * tm + r]
                pltpu.make_async_copy(_tile_of(y_hbm, src), _tile_of(gbuf.at[s], k * tm + r),
                                      sems.at[s]).start(priority=k)
            return carry
        lax.fori_loop(0, tm, body, 0, unroll=8)

    @pl.when(i == 0)
    def _():
        start(0, 0)

    @pl.when(i + 1 < n)
    def _():
        start(i + 1, 1 - slot)

    _tiles_wait(gbuf.at[slot], sems.at[slot])
    wts = wts_ref[...]
    g = gbuf.at[slot]
    f = wts[:, 0:1] * _tiles_to_rows(g, 0, tm) + wts[:, 1:2] * _tiles_to_rows(g, tm, tm)
    out = hs1_ref[...] + f * mod_ref[...][5:6, :]
    if final:
        out = _rms(out) * gain_ref[...]
    o_ref[...] = out


def _combine_call(pos, y, hs1, wts, mod, gain, dims, final):
    n_batch, n_tok, n_ctx = dims
    t, d = hs1.shape
    tm = TOKEN_TILE
    n_steps, tile_map = _token_tiles(dims, final)
    assert n_steps * tm == t
    mod_map = _mod_row_map(tm, n_tok, n_ctx, n_batch)
    grid_spec = pltpu.PrefetchScalarGridSpec(
        num_scalar_prefetch=1,
        grid=(n_steps,),
        in_specs=[pl.BlockSpec(memory_space=pl.ANY),
                  pl.BlockSpec((tm, d), lambda i, pos: (i, 0)),
                  pl.BlockSpec((tm, wts.shape[1]), lambda i, pos: (i, 0)),
                  pl.BlockSpec((None, 6, d), lambda i, pos: mod_map(tile_map(i))),
                  pl.BlockSpec((1, d), lambda i, pos: (0, 0))],
        out_specs=pl.BlockSpec((tm, d), lambda i, pos: (i, 0)),
        scratch_shapes=[pltpu.VMEM((2, 2 * tm * ROW_SUB, LANES), F32), pltpu.SemaphoreType.DMA((2,))],
    )
    return pl.pallas_call(
        functools.partial(_combine_kernel, final=final),
        grid_spec=grid_spec,
        out_shape=jax.ShapeDtypeStruct((n_steps * tm, d), F32),
        compiler_params=_cparams(("arbitrary",)),
        name="combine_final" if final else "combine",
    )(pos, y, hs1, wts, mod, gain)


def _rope_angles(n_ctx, rows, rot_dim):
    n_freq = rot_dim // 4
    freqs = ROPE_THETA ** (-jnp.arange(n_freq, dtype=F32) / n_freq)
    row = jnp.repeat(jnp.arange(rows, dtype=F32), GRID_W)
    col = (jnp.arange(rows * GRID_W) % GRID_W).astype(F32)
    ang = jnp.concatenate([row[:, None] * freqs, col[:, None] * freqs], axis=-1)
    ang = jnp.concatenate([jnp.zeros((n_ctx, rot_dim // 2), F32), ang], axis=0)
    return jnp.cos(ang), jnp.sin(ang)


def _tables(n_ctx, seq):
    n = n_ctx + seq
    cm, sm = _rope_angles(n_ctx, seq // GRID_W, MLA_ROPE)
    cg, sg = _rope_angles(n_ctx, seq // GRID_W, GQA_HEAD_DIM)
    one, zero = jnp.ones((n, MLA_NOPE), F32), jnp.zeros((n, MLA_NOPE), F32)
    pad = jnp.zeros((n, LANES - MLA_NOPE - MLA_ROPE), F32)
    gi = jnp.arange(BRANCH_W) // GQA_HEAD_DIM
    return {
        "cosm": jnp.concatenate([one, cm, cm, pad], axis=1),
        "sinm": jnp.concatenate([zero, -sm, sm, pad], axis=1),
        "cosg": jnp.concatenate([cg, cg, cg, cg], axis=1),
        "sing": jnp.concatenate([-sg, sg, -sg, sg], axis=1),
        "gmat": jnp.where(gi[:, None] == gi[None, :], 1.0 / GQA_HEAD_DIM, 0.0).astype(BF16),
    }


def _prep_weights(w_in, sgu_gain, w_sgu, b_sgu, mla_q_gain, w_uq, mla_kv_gain, w_ukv, w_conv,
                  gqa_q_gain, gqa_k_gain, w_gate, b_gate, w_branch, w_out, w_group_router,
                  b_group_router, w_expert_router, b_expert_router):
    nl, d, _ = w_in.shape
    zc = lambda n: jnp.zeros((nl, d, n), F32)
    half = MLA_ROPE // 2
    kpe = w_in[..., 896:928]
    gq = w_in[..., 1696:1952].reshape(nl, d, GQA_HEADS, GQA_HEAD_DIM)[:, :, (0, 2, 1, 3), :].reshape(nl, d, BRANCH_W)
    w_in_ext = jnp.concatenate([
        w_in[..., 0:896],
        zc(MLA_NOPE), kpe, zc(LANES - MLA_NOPE - MLA_ROPE),
        zc(MLA_NOPE), kpe[..., half:], kpe[..., :half], zc(LANES - MLA_NOPE - MLA_ROPE),
        w_in[..., 928:1696], gq, w_in[..., 1952:2208]], axis=-1).astype(BF16)

    dq = MLA_NOPE + MLA_ROPE
    uq = w_uq.reshape(nl, -1, MLA_HEADS, dq)
    zq = jnp.zeros(uq.shape[:3] + (LANES - dq,), F32)
    wq_a = jnp.concatenate([uq, zq], axis=-1)
    wq_b = jnp.concatenate([jnp.zeros_like(uq[..., :MLA_NOPE]), uq[..., MLA_NOPE + half:],
                            uq[..., MLA_NOPE:MLA_NOPE + half], zq], axis=-1)
    wq = jnp.concatenate([wq_a.reshape(nl, -1, MLA_HEADS * LANES),
                          wq_b.reshape(nl, -1, MLA_HEADS * LANES)], axis=-1).astype(BF16)

    ukv = w_ukv.reshape(nl, -1, MLA_HEADS, MLA_NOPE + MLA_V)
    k_lay = jnp.concatenate([ukv[..., :MLA_NOPE], jnp.zeros_like(ukv[..., :LANES - MLA_NOPE])], axis=-1)
    v_h = ukv[..., MLA_NOPE:]
    zv = jnp.zeros(v_h.shape[:2] + (LANES,), F32)
    v_lay = jnp.concatenate([v_h[:, :, 0], v_h[:, :, 1], zv, v_h[:, :, 2], v_h[:, :, 3], zv], axis=-1)
    wkv = jnp.concatenate([k_lay.reshape(nl, -1, MLA_HEADS * LANES), v_lay], axis=-1).astype(BF16)

    wb = w_branch
    wb_d = wb[:, 3].reshape(nl, GQA_HEADS, GQA_HEAD_DIM, d)[:, (0, 2, 1, 3)].reshape(nl, BRANCH_W, d)
    w_branch_p = (0.5 * jnp.concatenate([wb[:, :3], wb_d[:, None]], axis=1)).astype(BF16)

    gap = EXPERTS_PER_GROUP - N_GROUPS
    rest = LANES - EXPERTS_PER_GROUP - N_EXPERTS
    wr = jnp.concatenate([w_group_router, jnp.zeros((nl, d, gap), F32), w_expert_router,
                          jnp.zeros((nl, d, rest), F32)], axis=-1)
    wr_hi = wr.astype(BF16)
    wr_lo = (wr - wr_hi.astype(F32)).astype(BF16)
    br = jnp.concatenate([b_group_router, jnp.zeros((nl, gap), F32), b_expert_router,
                          jnp.zeros((nl, rest), F32)], axis=-1)[:, None, :]

    tile = lambda g, reps: jnp.concatenate([g] * reps, axis=-1)[:, None, :]
    return {
        "w_in": w_in_ext,
        "sgu_gain": sgu_gain[:, None, :],
        "w_sgu": w_sgu.astype(BF16),
        "b_sgu": jnp.repeat(b_sgu.transpose(0, 2, 1), BRANCH_W // SGU_GROUPS, axis=-1),
        "q_gain": mla_q_gain[:, None, :], "wq": wq,
        "kv_gain": mla_kv_gain[:, None, :], "wkv": wkv,
        "gq_gain": tile(gqa_q_gain, 4), "gk_gain": tile(gqa_k_gain, 2),
        "w_conv": w_conv,
        "w_gate": (0.5 * w_gate).astype(BF16), "b_gate": 0.5 * b_gate,
        "w_branch": w_branch_p, "w_out": w_out.astype(BF16),
        "wr_hi": wr_hi, "wr_lo": wr_lo, "br": br,
    }


def _dispatch_plan(route, cnt, n_tiles, cap):
    blk = EXPERT_BLK
    tm = route.shape[1] // n_tiles
    counts = cnt[:, 0]
    padded = (counts + blk - 1) // blk * blk
    pend = jnp.cumsum(padded)
    pstart = pend - padded
    eid, rank = route[0:2], route[2:4]
    onehot = eid[:, :, None] == jnp.arange(N_EXPERTS, dtype=jnp.int32)
    dest = jnp.sum(jnp.where(onehot, pstart, 0), axis=-1) + rank
    pos = dest.reshape(2, n_tiles, tm).transpose(1, 0, 2).reshape(-1)
    n_used = pend[-1] // blk
    blk_first = jnp.minimum(jnp.arange(cap // blk, dtype=jnp.int32), n_used - 1) * blk
    blk_e = jnp.sum((blk_first[:, None] >= pend[None, :]).astype(jnp.int32), axis=1)
    blk_e = jnp.minimum(blk_e, N_EXPERTS - 1)
    return pos, pstart + counts, padded - counts, blk_e, n_used.reshape(1)


def kernel(x, c, ctx, c_ctx, w_ada, b_ada, w_in, sgu_gain, w_sgu, b_sgu, mla_q_gain, w_uq, mla_kv_gain, w_ukv, w_conv, gqa_q_gain, gqa_k_gain, w_gate, b_gate, w_branch, w_out, w_group_router, b_group_router, w_expert_router, b_expert_router, w_expert_gate_up, w_expert_down, final_gain):
    n_batch, seq, d = x.shape
    n_ctx = ctx.shape[1]
    n_tok = n_ctx + seq
    t = n_batch * n_tok
    depth = w_in.shape[0]
    tm = TOKEN_TILE
    assert n_ctx % tm == 0 and seq % tm == 0 and seq % GRID_W == 0 and tm % SGU_CHUNK == 0
    dims = (n_batch, n_tok, n_ctx)

    hs = jnp.concatenate([ctx, x], axis=1).reshape(t, d)
    tabs = _tables(n_ctx, seq)
    rows = -(-(n_batch + 1) // 8) * 8
    cvec = jnp.concatenate([c, c_ctx[None, :], jnp.zeros((rows - n_batch - 1, d), F32)], axis=0)
    mods = _ada_call(cvec, w_ada, b_ada).reshape(depth, rows, 6, d)
    weights = _prep_weights(w_in, sgu_gain, w_sgu, b_sgu, mla_q_gain, w_uq, mla_kv_gain, w_ukv, w_conv,
                            gqa_q_gain, gqa_k_gain, w_gate, b_gate, w_branch, w_out, w_group_router,
                            b_group_router, w_expert_router, b_expert_router)
    gain = final_gain[None, :]

    for l in range(depth):
        last = l == depth - 1
        lw = {k: v[l] for k, v in weights.items()}
        mod = mods[l]
        ya, qm, km, vm, p, cb, qg, kg, vg = _in_call(hs, mod, tabs, lw, dims)
        yb = _attn_call(_mla_attn_kernel, "mla_attn", qm, km, vm, dims)
        yd = _attn_call(_gqa_attn_kernel, "gqa_attn", qg, kg, vg, dims)
        hs1, hn2, logits = _merge_call(hs, mod, ya, yb, yd, p, cb, lw, dims, latent_only=last)
        route, wts, cnt = _route_call(logits)
        t_moe = hs1.shape[0]
        cap = -(-2 * t_moe // EXPERT_BLK) * EXPERT_BLK + N_EXPERTS * EXPERT_BLK
        pos, pad_start, pad_len, blk_e, n_used = _dispatch_plan(route, cnt, t_moe // tm, cap)
        xs = _dispatch_call(pos, pad_start, pad_len, n_used * EXPERT_BLK, hn2, cap)
        y = _expert_call(blk_e, n_used, xs, w_expert_gate_up, w_expert_down, l)
        hs = _combine_call(pos, y, hs1, wts.T, mod, gain, dims, final=last)
    return hs.reshape(n_batch, seq, d)
```

```python
import functools
import math

import jax
import jax.numpy as jnp
from jax import lax
from jax.experimental import pallas as pl
from jax.experimental.pallas import tpu as pltpu

F32 = jnp.float32
BF16 = jnp.bfloat16

GRID_W = 64
ROPE_THETA = 10000.0
NORM_EPS = 1e-6
BRANCH_W = 256
SGU_CHUNK = 128
SGU_GROUPS = 4
MLA_HEADS = 4
MLA_NOPE = 64
MLA_ROPE = 32
MLA_V = 64
GQA_HEADS = 4
GQA_HEAD_DIM = 64
N_GROUPS = 4
EXPERTS_PER_GROUP = 8
N_EXPERTS = N_GROUPS * EXPERTS_PER_GROUP
D_EXPERT = 256
LANES = 128
ROW_SUB = 8
TOKEN_TILE = 256
EXPERT_BLK = 256
VMEM_LIMIT = 56 * 1024 * 1024
LOG2E = 1.4426950408889634

C_AU, C_AV, C_MQ, C_MKV, C_KPA, C_KPB, C_CB, C_CC, C_CX, C_GQ, C_GK, C_GV, C_END = (
    0, 256, 512, 768, 896, 1024, 1152, 1408, 1664, 1920, 2176, 2304, 2432)


def _cparams(sem):
    return pltpu.CompilerParams(dimension_semantics=sem, vmem_limit_bytes=VMEM_LIMIT)


def _dot(a, b):
    return jnp.dot(a, b, preferred_element_type=F32)


def _dot_nt(a, b):
    return lax.dot_general(a, b, (((1,), (1,)), ((), ())), preferred_element_type=F32)


def _split_bf16(v):
    hi = v.astype(BF16)
    lo = (v - hi.astype(F32)).astype(BF16)
    return hi, lo


def _gelu(v):
    return 0.5 * v * (1.0 + jnp.tanh(math.sqrt(2.0 / math.pi) * (v + 0.044715 * (v * v * v))))


def _sigmoid(v):
    return 0.5 * jnp.tanh(0.5 * v) + 0.5


def _rms(v):
    return v * lax.rsqrt(jnp.mean(v * v, axis=-1, keepdims=True) + NORM_EPS)


def _const_spec(shape):
    nd = len(shape)
    return pl.BlockSpec(shape, lambda *_: (0,) * nd)


def _layer_spec(stacked, layer):
    nd = stacked.ndim - 1
    return pl.BlockSpec((None,) + stacked.shape[1:], lambda *_: (layer,) + (0,) * nd)


def _ada_kernel(c_ref, w_ref, b_ref, o_ref):
    cv = c_ref[...]
    s = cv * _sigmoid(cv)
    s_hi, s_lo = _split_bf16(s)
    w_hi, w_lo = _split_bf16(w_ref[...])
    o_ref[...] = _dot(s_hi, w_hi) + _dot(s_lo, w_hi) + _dot(s_hi, w_lo) + b_ref[...]


def _ada_call(cvec, w_ada, b_ada):
    n_layers, d, d6 = w_ada.shape
    rows = cvec.shape[0]
    tn = 1536
    return pl.pallas_call(
        _ada_kernel,
        grid=(n_layers, d6 // tn),
        in_specs=[pl.BlockSpec((rows, d), lambda l, j: (0, 0)),
                  pl.BlockSpec((None, d, tn), lambda l, j: (l, 0, j)),
                  pl.BlockSpec((None, 1, tn), lambda l, j: (l, 0, j))],
        out_specs=pl.BlockSpec((None, rows, tn), lambda l, j: (l, 0, j)),
        out_shape=jax.ShapeDtypeStruct((n_layers, rows, d6), F32),
        compiler_params=_cparams(("arbitrary", "arbitrary")),
        name="ada_mod",
    )(cvec, w_ada, b_ada.reshape(n_layers, 1, d6))


def _in_kernel(hs_ref, mod_ref, cosm_ref, sinm_ref, cosg_ref, sing_ref, w_in_ref, sgu_gain_ref,
               w_sgu_ref, b_sgu_ref, qgain_ref, wq_ref, kvgain_ref, wkv_ref, gqgain_ref, gkgain_ref,
               gmat_ref,
               ya_ref, qm_ref, km_ref, vm_ref, p_ref, cb_ref, qg_ref, kg_ref, vg_ref):
    tm = hs_ref.shape[0]
    mod = mod_ref[...]
    hn = _rms(hs_ref[...]) * (1.0 + mod[1:2, :]) + mod[0:1, :]
    z = _dot(hn.astype(BF16), w_in_ref[...])

    av = _rms(_gelu(z[:, C_AV:C_MQ])) * sgu_gain_ref[...]
    av = av.astype(BF16)
    grp = lax.broadcasted_iota(jnp.int32, (SGU_CHUNK, BRANCH_W), 1) // (BRANCH_W // SGU_GROUPS)
    chunks = []
    for ci in range(tm // SGU_CHUNK):
        vc = av[ci * SGU_CHUNK:(ci + 1) * SGU_CHUNK, :]
        m = b_sgu_ref[...]
        for g in range(SGU_GROUPS):
            m = m + jnp.where(grp == g, _dot(w_sgu_ref[g], vc), 0.0)
        chunks.append(m)
    mixed = jnp.concatenate(chunks, axis=0)
    ya_ref[...] = (_gelu(z[:, C_AU:C_AV]) * mixed).astype(BF16)

    cosm = cosm_ref[...]
    sinm = sinm_ref[...]
    cos4 = jnp.concatenate([cosm] * MLA_HEADS, axis=1)
    sin4 = jnp.concatenate([sinm] * MLA_HEADS, axis=1)
    nq = (_rms(z[:, C_MQ:C_MKV]) * qgain_ref[...]).astype(BF16)
    qab = _dot(nq, wq_ref[...])
    hw = MLA_HEADS * LANES
    q_scale = (MLA_NOPE + MLA_ROPE) ** -0.5 * LOG2E
    qm_ref[...] = ((qab[:, :hw] * cos4 + qab[:, hw:] * sin4) * q_scale).astype(BF16)
    nkv = (_rms(z[:, C_MKV:C_KPA]) * kvgain_ref[...]).astype(BF16)
    kv = _dot(nkv, wkv_ref[...])
    kpe = z[:, C_KPA:C_KPB] * cosm + z[:, C_KPB:C_CB] * sinm
    km_ref[...] = (kv[:, :hw] + jnp.concatenate([kpe] * MLA_HEADS, axis=1)).astype(BF16)
    lane = lax.broadcasted_iota(jnp.int32, (tm, hw), 1)
    ones_col = jnp.where(lane % (2 * LANES) == LANES, 1.0, 0.0)
    vm_ref[...] = (kv[:, hw:] + ones_col).astype(BF16)

    p_ref[...] = (z[:, C_CC:C_CX] * z[:, C_CX:C_GQ]).astype(BF16)
    cb_ref[...] = z[:, C_CB:C_CC].astype(BF16)

    cosg = cosg_ref[...]
    sing = sing_ref[...]
    gmat = gmat_ref[...]

    def head_norm_rope(v, gain, cos_t, sin_t, scale):
        w = v.shape[1]
        sq_hi, sq_lo = _split_bf16(v * v)
        ms = _dot(sq_hi, gmat[:w, :w]) + _dot(sq_lo, gmat[:w, :w])
        y = v * lax.rsqrt(ms + NORM_EPS) * gain
        half = GQA_HEAD_DIM // 2
        ln = lax.broadcasted_iota(jnp.int32, v.shape, 1) % GQA_HEAD_DIM
        partner = jnp.where(ln < half, pltpu.roll(y, w - half, axis=1), pltpu.roll(y, half, axis=1))
        return ((y * cos_t + partner * sin_t) * scale).astype(BF16)

    cos2 = jnp.concatenate([cosg, cosg], axis=1)
    sin2 = jnp.concatenate([sing, sing], axis=1)
    qg_ref[...] = head_norm_rope(z[:, C_GQ:C_GK], gqgain_ref[...], cos2, sin2,
                                 GQA_HEAD_DIM ** -0.5 * LOG2E)
    kg_ref[...] = head_norm_rope(z[:, C_GK:C_GV], gkgain_ref[...], cosg, sing, 1.0)
    lane2 = lax.broadcasted_iota(jnp.int32, (tm, LANES), 1)
    vg_ref[...] = jnp.concatenate([z[:, C_GV:C_END], jnp.where(lane2 == 0, 1.0, 0.0)], axis=1).astype(BF16)


def _mod_spec(mods, layer, dims, tile_map=lambda i: i):
    n_batch, n_tok, n_ctx = dims

    def index_map(i, *_):
        start = tile_map(i) * TOKEN_TILE
        return (layer, jnp.where(start % n_tok < n_ctx, n_batch, start // n_tok), 0, 0)
    return pl.BlockSpec((None, None) + mods.shape[2:], index_map)


def _in_call(hs, mods, tabs, weights, layer, dims):
    n_batch, n_tok, n_ctx = dims
    t, d = hs.shape
    tm = TOKEN_TILE
    tiles_per_seq = n_tok // tm
    row = lambda w: pl.BlockSpec((tm, w), lambda i: (i, 0))
    tab = pl.BlockSpec((tm, LANES), lambda i: (i % tiles_per_seq, 0))
    hw = MLA_HEADS * LANES
    outs = [(BRANCH_W, "ya"), (hw, "qm"), (hw, "km"), (hw, "vm"), (BRANCH_W, "p"), (BRANCH_W, "cb"),
            (BRANCH_W, "qg"), (LANES, "kg"), (2 * LANES, "vg")]
    names = ("w_in", "sgu_gain", "w_sgu", "b_sgu", "q_gain", "wq", "kv_gain", "wkv", "gq_gain", "gk_gain")
    params = [weights[k] for k in names]
    return pl.pallas_call(
        _in_kernel,
        grid=(t // tm,),
        in_specs=[row(d), _mod_spec(mods, layer, dims), tab, tab, tab, tab]
                 + [_layer_spec(a, layer) for a in params] + [_const_spec((BRANCH_W, BRANCH_W))],
        out_specs=[row(w) for w, _ in outs],
        out_shape=[jax.ShapeDtypeStruct((t, w), BF16) for w, _ in outs],
        compiler_params=_cparams(("arbitrary",)),
        name="in_proj",
    )(hs, mods, tabs["cosm"], tabs["sinm"], tabs["cosg"], tabs["sing"], *params, tabs["gmat"])


def _softmax_pv(s, v):
    m = jnp.max(s, axis=-1, keepdims=True)
    p = jnp.exp2(s - m).astype(BF16)
    r = _dot(p, v)
    return r[:, :LANES] * (1.0 / r[:, LANES:LANES + 1])


def _attn_kernel(qm_ref, km_ref, vm_ref, qg_ref, kg_ref, vg_ref, ob_ref, od_ref, *, ctx_tiles, n_ctx):
    def run(nk):
        lane = lax.broadcasted_iota(jnp.int32, (qm_ref.shape[0], LANES), 1)
        outs = []
        for pair in range(MLA_HEADS // 2):
            v = vm_ref[0:nk, pair * 2 * LANES:(pair + 1) * 2 * LANES]
            res = []
            for hh in range(2):
                h = pair * 2 + hh
                s = _dot_nt(qm_ref[:, h * LANES:(h + 1) * LANES], km_ref[0:nk, h * LANES:(h + 1) * LANES])
                res.append(_softmax_pv(s, v))
            outs.append(jnp.where(lane < MLA_V, res[0], res[1]))
        ob_ref[...] = jnp.concatenate(outs, axis=1).astype(BF16)

        k = kg_ref[0:nk, :]
        v = vg_ref[0:nk, :]
        outs = []
        for grp in range(GQA_HEADS // 2):
            q = qg_ref[:, grp * LANES:(grp + 1) * LANES]
            zero = jnp.zeros_like(q)
            r_lo = _softmax_pv(_dot_nt(jnp.where(lane < GQA_HEAD_DIM, q, zero), k), v)
            r_hi = _softmax_pv(_dot_nt(jnp.where(lane >= GQA_HEAD_DIM, q, zero), k), v)
            outs.append(jnp.where(lane < GQA_HEAD_DIM, r_lo, r_hi))
        od_ref[...] = jnp.concatenate(outs, axis=1).astype(BF16)

    j = pl.program_id(1)
    pl.when(j < ctx_tiles)(lambda: run(n_ctx))
    pl.when(j >= ctx_tiles)(lambda: run(km_ref.shape[0]))


def _attn_call(qm, km, vm, qg, kg, vg, dims):
    n_batch, n_tok, n_ctx = dims
    tq = TOKEN_TILE
    tiles = n_tok // tq
    qspec = lambda w: pl.BlockSpec((tq, w), lambda b, j: (b * tiles + j, 0))
    kspec = lambda w: pl.BlockSpec((n_tok, w), lambda b, j: (b, 0))
    return pl.pallas_call(
        functools.partial(_attn_kernel, ctx_tiles=n_ctx // tq, n_ctx=n_ctx),
        grid=(n_batch, tiles),
        in_specs=[qspec(qm.shape[1]), kspec(km.shape[1]), kspec(vm.shape[1]),
                  qspec(qg.shape[1]), kspec(kg.shape[1]), kspec(vg.shape[1])],
        out_specs=[qspec(BRANCH_W), qspec(BRANCH_W)],
        out_shape=[jax.ShapeDtypeStruct((qm.shape[0], BRANCH_W), BF16)] * 2,
        compiler_params=_cparams(("arbitrary", "arbitrary")),
        name="attn",
    )(qm, km, vm, qg, kg, vg)


def _merge_kernel(hs_ref, mod_ref, ya_ref, yb_ref, yd_ref, p_ref, pprev_ref, pnext_ref, cb_ref,
                  wconv_ref, wgate_ref, bgate_ref, wbranch_ref, wout_ref, wr_hi_ref, wr_lo_ref, br_ref,
                  hs1_ref, hn2_ref, logit_ref, *, n_tok, n_ctx, tile_map):
    tm = hs_ref.shape[0]
    mod = mod_ref[...]
    hs = hs_ref[...]
    hn = (_rms(hs) * (1.0 + mod[1:2, :]) + mod[0:1, :]).astype(BF16)

    start = (tile_map(pl.program_id(0)) * tm) % n_tok
    prev_ok = jnp.logical_and(start != 0, start != n_ctx)
    next_ok = jnp.logical_and(start + tm != n_ctx, start + tm != n_tok)
    p = p_ref[...].astype(F32)
    halo_prev = jnp.where(prev_ok, pprev_ref[...].astype(F32)[15:16, :], 0.0)
    halo_next = jnp.where(next_ok, pnext_ref[...].astype(F32)[0:1, :], 0.0)
    rowi = lax.broadcasted_iota(jnp.int32, p.shape, 0)
    prev = jnp.where(rowi == 0, halo_prev, pltpu.roll(p, 1, axis=0))
    nxt = jnp.where(rowi == tm - 1, halo_next, pltpu.roll(p, tm - 1, axis=0))
    wc = wconv_ref[...]
    yc = cb_ref[...].astype(F32) * (wc[0:1, :] * prev + wc[1:2, :] * p + wc[2:3, :] * nxt)

    branches = (ya_ref[...], yb_ref[...], yc.astype(BF16), yd_ref[...])
    merged = None
    for i, y in enumerate(branches):
        half_gate = _dot(hn, wgate_ref[i]) + bgate_ref[i:i + 1, :]
        term = (jnp.tanh(half_gate) + 1.0) * _dot(y, wbranch_ref[i])
        merged = term if merged is None else merged + term
    hs1 = hs + _dot(merged.astype(BF16), wout_ref[...]) * mod[2:3, :]
    hs1_ref[...] = hs1

    hn2 = _rms(hs1) * (1.0 + mod[4:5, :]) + mod[3:4, :]
    _rows_to_tiles(hn2_ref, hn2)

    x_hi, x_lo = _split_bf16(hn2)
    logit_ref[...] = (_dot(x_hi, wr_hi_ref[...]) + _dot(x_lo, wr_hi_ref[...]) + _dot(x_hi, wr_lo_ref[...])
                      + br_ref[...])


ROUTE_SUB = 256


def _route_kernel(logit_ref, route_ref, wts_ref, cnt_ref, carry_ref):
    tm = logit_ref.shape[0]
    lt = logit_ref[...].T
    sub = lax.broadcasted_iota(jnp.int32, (EXPERTS_PER_GROUP, tm), 0)
    neg = -jnp.inf
    top = lambda v: jnp.max(v, axis=0, keepdims=True)
    first = lambda hit: jnp.min(jnp.where(hit, sub, EXPERTS_PER_GROUP), axis=0, keepdims=True)
    gl = jnp.where(sub < N_GROUPS, lt[0:EXPERTS_PER_GROUP, :], neg)
    ge = jnp.exp(gl - top(gl))
    pg = ge / jnp.sum(ge, axis=0, keepdims=True)
    pg_top = top(pg)
    g_sel = first(pg == pg_top)
    le = lt[EXPERTS_PER_GROUP:2 * EXPERTS_PER_GROUP, :]
    for g in range(1, N_GROUPS):
        le = jnp.where(g_sel == g, lt[(g + 1) * EXPERTS_PER_GROUP:(g + 2) * EXPERTS_PER_GROUP, :], le)
    ee = jnp.exp(le - top(le))
    pe = ee / jnp.sum(ee, axis=0, keepdims=True)
    p1 = top(pe)
    i1 = first(pe == p1)
    pe2 = jnp.where(sub != i1, pe, neg)
    p2 = top(pe2)
    i2 = first(jnp.logical_and(sub != i1, pe2 == p2))
    denom = p1 + p2
    e1 = g_sel * EXPERTS_PER_GROUP + i1
    e2 = g_sel * EXPERTS_PER_GROUP + i2
    wts_ref[...] = jnp.where(sub == 0, pg_top * p1 / denom, jnp.where(sub == 1, pg_top * p2 / denom, 0.0))

    @pl.when(pl.program_id(0) == 0)
    def _():
        carry_ref[...] = jnp.zeros_like(carry_ref)

    ts = ROUTE_SUB
    erow = lax.broadcasted_iota(jnp.int32, (N_EXPERTS, ts), 0)
    ri = lax.broadcasted_iota(jnp.int32, (ts, ts + LANES), 0)
    ci = lax.broadcasted_iota(jnp.int32, (ts, ts + LANES), 1)
    before = jnp.where(jnp.logical_or(ri < ci, ci >= ts), 1.0, 0.0).astype(BF16)
    carry = carry_ref[...]
    r1s, r2s = [], []
    for s in range(tm // ts):
        oh1 = jnp.where(erow == e1[:, s * ts:(s + 1) * ts], 1.0, 0.0)
        oh2 = jnp.where(erow == e2[:, s * ts:(s + 1) * ts], 1.0, 0.0)
        pref = _dot(jnp.concatenate([oh1, oh2], axis=0).astype(BF16), before)
        c1 = pref[:N_EXPERTS, ts:]
        c2 = pref[N_EXPERTS:, ts:]
        r1s.append(jnp.sum(oh1 * (carry[:, 0:1] + pref[:N_EXPERTS, :ts]), axis=0, keepdims=True))
        r2s.append(jnp.sum(oh2 * ((carry + c1)[:, 0:1] + pref[N_EXPERTS:, :ts]), axis=0, keepdims=True))
        carry = carry + c1 + c2
    r1 = jnp.concatenate(r1s, axis=1)
    r2 = jnp.concatenate(r2s, axis=1)
    carry_ref[...] = carry
    cnt_ref[...] = carry.astype(jnp.int32)
    route_ref[...] = jnp.where(sub == 0, e1, jnp.where(sub == 1, e2,
                               jnp.where(sub == 2, r1.astype(jnp.int32),
                                         jnp.where(sub == 3, r2.astype(jnp.int32), 0))))


def _route_call(logits):
    t = logits.shape[0]
    tr = 1024 if t % 1024 == 0 else ROUTE_SUB
    return pl.pallas_call(
        _route_kernel,
        grid=(t // tr,),
        in_specs=[pl.BlockSpec((tr, LANES), lambda i: (i, 0))],
        out_specs=[pl.BlockSpec((8, tr), lambda i: (0, i)), pl.BlockSpec((8, tr), lambda i: (0, i)),
                   pl.BlockSpec((N_EXPERTS, LANES), lambda i: (0, 0))],
        out_shape=[jax.ShapeDtypeStruct((8, t), jnp.int32), jax.ShapeDtypeStruct((8, t), F32),
                   jax.ShapeDtypeStruct((N_EXPERTS, LANES), jnp.int32)],
        scratch_shapes=[pltpu.VMEM((N_EXPERTS, LANES), F32)],
        compiler_params=_cparams(("arbitrary",)),
        name="route",
    )(logits)


def _token_tiles(dims, latent_only):
    n_batch, n_tok, n_ctx = dims
    tiles, ctx_tiles = n_tok // TOKEN_TILE, n_ctx // TOKEN_TILE
    if not latent_only:
        return n_batch * tiles, lambda i: i
    lat = tiles - ctx_tiles
    return n_batch * lat, lambda i: (i // lat) * tiles + ctx_tiles + i % lat


def _merge_call(hs, mods, ya, yb, yd, p, cb, weights, layer, dims, latent_only):
    n_batch, n_tok, n_ctx = dims
    t, d = hs.shape
    tm = TOKEN_TILE
    sub = 16
    n_sub = t // sub
    n_steps, tile_map = _token_tiles(dims, latent_only)
    t_out = n_steps * tm
    row_in = lambda w: pl.BlockSpec((tm, w), lambda i: (tile_map(i), 0))
    row = lambda w: pl.BlockSpec((tm, w), lambda i: (i, 0))
    names = ("w_conv", "w_gate", "b_gate", "w_branch", "w_out", "wr_hi", "wr_lo", "br")
    params = [weights[k] for k in names]
    return pl.pallas_call(
        functools.partial(_merge_kernel, n_tok=n_tok, n_ctx=n_ctx, tile_map=tile_map),
        grid=(n_steps,),
        in_specs=[row_in(d), _mod_spec(mods, layer, dims, tile_map),
                  row_in(BRANCH_W), row_in(BRANCH_W), row_in(BRANCH_W), row_in(BRANCH_W),
                  pl.BlockSpec((sub, BRANCH_W),
                               lambda i: (jnp.maximum(tile_map(i) * (tm // sub) - 1, 0), 0)),
                  pl.BlockSpec((sub, BRANCH_W),
                               lambda i: (jnp.minimum((tile_map(i) + 1) * (tm // sub), n_sub - 1), 0)),
                  row_in(BRANCH_W)] + [_layer_spec(a, layer) for a in params],
        out_specs=[row(d), pl.BlockSpec((tm * ROW_SUB, LANES), lambda i: (i, 0)), row(LANES)],
        out_shape=[jax.ShapeDtypeStruct((t_out, d), F32), jax.ShapeDtypeStruct((t_out * ROW_SUB, LANES), F32),
                   jax.ShapeDtypeStruct((t_out, LANES), F32)],
        compiler_params=_cparams(("arbitrary",)),
        name="merge_route",
    )(hs, mods, ya, yb, yd, p, p, p, cb, *params)


PAD_BITS = tuple(1 << b for b in reversed(range(EXPERT_BLK.bit_length() - 1)))


def _rows_to_tiles(ref, v):
    for ck in range(ROW_SUB):
        ref[pl.ds(ck, v.shape[0], stride=ROW_SUB), :] = v[:, ck * LANES:(ck + 1) * LANES]


def _tiles_to_rows(ref, first, n):
    return jnp.concatenate([ref[pl.ds(first * ROW_SUB + ck, n, stride=ROW_SUB), :] for ck in range(ROW_SUB)],
                           axis=1)


def _tile_of(ref, row, n=1):
    return ref.at[pl.ds(pl.multiple_of(row * ROW_SUB, ROW_SUB), n * ROW_SUB)]


def _tiles_wait(like, sem):
    pltpu.make_async_copy(like, like, sem).wait()


def _dispatch_kernel(pos_ref, pad_start_ref, pad_len_ref, tail_ref, x_ref, xs_hbm, stage, zeros, sems, pad_sem):
    i = pl.program_id(0)
    n = pl.num_programs(0)
    tm = x_ref.shape[0] // ROW_SUB
    slot = i % 2

    def pad_copies(e, fn):
        start = pad_start_ref[e]
        ln = pad_len_ref[e]
        for b in PAD_BITS:
            @pl.when((ln & b) != 0)
            def _():
                off = ln & ~(2 * b - 1)
                fn(pltpu.make_async_copy(_tile_of(zeros, 0, b), _tile_of(xs_hbm, start + off, b), pad_sem))

    def for_each_expert(fn):
        def body(e, carry):
            pad_copies(e, fn)
            return carry
        lax.fori_loop(0, N_EXPERTS, body, 0)

        def tail(j, carry):
            fn(pltpu.make_async_copy(zeros, _tile_of(xs_hbm, j * PAD_BITS[0], PAD_BITS[0]), pad_sem))
            return carry
        lax.fori_loop(tail_ref[0] // PAD_BITS[0], xs_hbm.shape[0] // (ROW_SUB * PAD_BITS[0]), tail, 0)

    @pl.when(i == 0)
    def _():
        zeros[...] = jnp.zeros_like(zeros)
        for_each_expert(lambda cp: cp.start())

    @pl.when(i >= 2)
    def _():
        _tiles_wait(stage, sems.at[slot])

    stage[slot] = x_ref[...]

    def push(r, carry):
        for k in range(2):
            dst = pos_ref[i * (2 * tm) + k * tm + r]
            pltpu.make_async_copy(_tile_of(stage.at[slot], r), _tile_of(xs_hbm, dst),
                                  sems.at[slot]).start(priority=k)
        return carry
    lax.fori_loop(0, tm, push, 0, unroll=8)

    @pl.when(i == n - 1)
    def _():
        _tiles_wait(stage, sems.at[slot])

        @pl.when(n >= 2)
        def _():
            _tiles_wait(stage, sems.at[1 - slot])
        for_each_expert(lambda cp: cp.wait())


def _dispatch_call(pos, pad_start, pad_len, tail_start, x_tiles, cap):
    tm = TOKEN_TILE
    grid_spec = pltpu.PrefetchScalarGridSpec(
        num_scalar_prefetch=4,
        grid=(x_tiles.shape[0] // (tm * ROW_SUB),),
        in_specs=[pl.BlockSpec((tm * ROW_SUB, LANES), lambda i, *_: (i, 0))],
        out_specs=pl.BlockSpec(memory_space=pl.ANY),
        scratch_shapes=[pltpu.VMEM((2, tm * ROW_SUB, LANES), F32), pltpu.VMEM((PAD_BITS[0] * ROW_SUB, LANES), F32),
                        pltpu.SemaphoreType.DMA((2,)), pltpu.SemaphoreType.DMA(())],
    )
    return pl.pallas_call(
        _dispatch_kernel,
        grid_spec=grid_spec,
        out_shape=jax.ShapeDtypeStruct((cap * ROW_SUB, LANES), F32),
        compiler_params=_cparams(("arbitrary",)),
        name="dispatch",
    )(pos, pad_start, pad_len, tail_start, x_tiles)


def _expert_kernel(blk_e_ref, n_used_ref, x_ref, wgu_ref, wdn_ref, y_ref, wgu_bf, wdn_bf):
    i = pl.program_id(0)

    @pl.when(jnp.logical_or(i == 0, blk_e_ref[i] != blk_e_ref[jnp.maximum(i - 1, 0)]))
    def _():
        wgu_bf[...] = wgu_ref[...].astype(BF16)
        wdn_bf[...] = wdn_ref[...].astype(BF16)

    @pl.when(i < n_used_ref[0])
    def _():
        xe = _tiles_to_rows(x_ref, 0, EXPERT_BLK).astype(BF16)
        gu = _dot(xe, wgu_bf[...])
        g = gu[:, :D_EXPERT]
        act = (g * _sigmoid(g) * gu[:, D_EXPERT:]).astype(BF16)
        _rows_to_tiles(y_ref, _dot(act, wdn_bf[...]))

    @pl.when(i >= n_used_ref[0])
    def _():
        y_ref[...] = jnp.zeros_like(y_ref)


def _expert_call(blk_e, n_used, xs, w_gu, w_dn, layer):
    d = ROW_SUB * LANES
    blk = EXPERT_BLK
    used = lambda i, be, nu: (jnp.minimum(i, nu[0] - 1), 0)
    grid_spec = pltpu.PrefetchScalarGridSpec(
        num_scalar_prefetch=2,
        grid=(xs.shape[0] // (blk * ROW_SUB),),
        in_specs=[pl.BlockSpec((blk * ROW_SUB, LANES), used),
                  pl.BlockSpec((None, None, d, 2 * D_EXPERT), lambda i, be, nu: (layer, be[i], 0, 0)),
                  pl.BlockSpec((None, None, D_EXPERT, d), lambda i, be, nu: (layer, be[i], 0, 0))],
        out_specs=pl.BlockSpec((blk * ROW_SUB, LANES), lambda i, be, nu: (i, 0)),
        scratch_shapes=[pltpu.VMEM((d, 2 * D_EXPERT), BF16), pltpu.VMEM((D_EXPERT, d), BF16)],
    )
    return pl.pallas_call(
        _expert_kernel,
        grid_spec=grid_spec,
        out_shape=jax.ShapeDtypeStruct(xs.shape, F32),
        compiler_params=_cparams(("arbitrary",)),
        name="experts",
    )(blk_e, n_used, xs, w_gu, w_dn)


def _combine_kernel(pos_ref, y_hbm, hs1_ref, wts_ref, mod_ref, gain_ref, o_ref, gbuf, sems, *,
                    final):
    i = pl.program_id(0)
    n = pl.num_programs(0)
    tm = hs1_ref.shape[0]
    slot = i % 2

    def start(step, s):
        base = step * (2 * tm)

        def body(r, carry):
            for k in range(2):
                src = pos_ref[base + k * tm + r]
                pltpu.make_async_copy(_tile_of(y_hbm, src), _tile_of(gbuf.at[s], k * tm + r),
                                      sems.at[s]).start(priority=k)
            return carry
        lax.fori_loop(0, tm, body, 0, unroll=8)

    @pl.when(i == 0)
    def _():
        start(0, 0)

    @pl.when(i + 1 < n)
    def _():
        start(i + 1, 1 - slot)

    _tiles_wait(gbuf.at[slot], sems.at[slot])
    wts = wts_ref[...]
    g = gbuf.at[slot]
    f = wts[:, 0:1] * _tiles_to_rows(g, 0, tm) + wts[:, 1:2] * _tiles_to_rows(g, tm, tm)
    out = hs1_ref[...] + f * mod_ref[...][5:6, :]
    if final:
        out = _rms(out) * gain_ref[...]
    o_ref[...] = out


def _combine_call(pos, y, hs1, wts, mods, layer, gain, dims, final):
    t, d = hs1.shape
    tm = TOKEN_TILE
    n_steps, tile_map = _token_tiles(dims, final)
    assert n_steps * tm == t
    grid_spec = pltpu.PrefetchScalarGridSpec(
        num_scalar_prefetch=1,
        grid=(n_steps,),
        in_specs=[pl.BlockSpec(memory_space=pl.ANY),
                  pl.BlockSpec((tm, d), lambda i, pos: (i, 0)),
                  pl.BlockSpec((tm, wts.shape[1]), lambda i, pos: (i, 0)),
                  _mod_spec(mods, layer, dims, tile_map),
                  pl.BlockSpec((1, d), lambda i, pos: (0, 0))],
        out_specs=pl.BlockSpec((tm, d), lambda i, pos: (i, 0)),
        scratch_shapes=[pltpu.VMEM((2, 2 * tm * ROW_SUB, LANES), F32), pltpu.SemaphoreType.DMA((2,))],
    )
    return pl.pallas_call(
        functools.partial(_combine_kernel, final=final),
        grid_spec=grid_spec,
        out_shape=jax.ShapeDtypeStruct((n_steps * tm, d), F32),
        compiler_params=_cparams(("arbitrary",)),
        name="combine_final" if final else "combine",
    )(pos, y, hs1, wts, mods, gain)


def _rope_angles(n_ctx, rows, rot_dim):
    n_freq = rot_dim // 4
    freqs = ROPE_THETA ** (-jnp.arange(n_freq, dtype=F32) / n_freq)
    row = jnp.repeat(jnp.arange(rows, dtype=F32), GRID_W)
    col = (jnp.arange(rows * GRID_W) % GRID_W).astype(F32)
    ang = jnp.concatenate([row[:, None] * freqs, col[:, None] * freqs], axis=-1)
    ang = jnp.concatenate([jnp.zeros((n_ctx, rot_dim // 2), F32), ang], axis=0)
    return jnp.cos(ang), jnp.sin(ang)


def _tables(n_ctx, seq):
    n = n_ctx + seq
    cm, sm = _rope_angles(n_ctx, seq // GRID_W, MLA_ROPE)
    cg, sg = _rope_angles(n_ctx, seq // GRID_W, GQA_HEAD_DIM)
    one, zero = jnp.ones((n, MLA_NOPE), F32), jnp.zeros((n, MLA_NOPE), F32)
    pad = jnp.zeros((n, LANES - MLA_NOPE - MLA_ROPE), F32)
    gi = jnp.arange(BRANCH_W) // GQA_HEAD_DIM
    return {
        "cosm": jnp.concatenate([one, cm, cm, pad], axis=1),
        "sinm": jnp.concatenate([zero, -sm, sm, pad], axis=1),
        "cosg": jnp.concatenate([cg, cg, cg, cg], axis=1),
        "sing": jnp.concatenate([-sg, sg, -sg, sg], axis=1),
        "gmat": jnp.where(gi[:, None] == gi[None, :], 1.0 / GQA_HEAD_DIM, 0.0).astype(BF16),
    }


def _prep_weights(w_in, sgu_gain, w_sgu, b_sgu, mla_q_gain, w_uq, mla_kv_gain, w_ukv, w_conv,
                  gqa_q_gain, gqa_k_gain, w_gate, b_gate, w_branch, w_out, w_group_router,
                  b_group_router, w_expert_router, b_expert_router):
    nl, d, _ = w_in.shape
    zc = lambda n: jnp.zeros((nl, d, n), F32)
    half = MLA_ROPE // 2
    kpe = w_in[..., 896:928]
    gq = w_in[..., 1696:1952].reshape(nl, d, GQA_HEADS, GQA_HEAD_DIM)[:, :, (0, 2, 1, 3), :].reshape(nl, d, BRANCH_W)
    w_in_ext = jnp.concatenate([
        w_in[..., 0:896],
        zc(MLA_NOPE), kpe, zc(LANES - MLA_NOPE - MLA_ROPE),
        zc(MLA_NOPE), kpe[..., half:], kpe[..., :half], zc(LANES - MLA_NOPE - MLA_ROPE),
        w_in[..., 928:1696], gq, w_in[..., 1952:2208]], axis=-1).astype(BF16)

    dq = MLA_NOPE + MLA_ROPE
    uq = w_uq.reshape(nl, -1, MLA_HEADS, dq)
    zq = jnp.zeros(uq.shape[:3] + (LANES - dq,), F32)
    wq_a = jnp.concatenate([uq, zq], axis=-1)
    wq_b = jnp.concatenate([jnp.zeros_like(uq[..., :MLA_NOPE]), uq[..., MLA_NOPE + half:],
                            uq[..., MLA_NOPE:MLA_NOPE + half], zq], axis=-1)
    wq = jnp.concatenate([wq_a.reshape(nl, -1, MLA_HEADS * LANES),
                          wq_b.reshape(nl, -1, MLA_HEADS * LANES)], axis=-1).astype(BF16)

    ukv = w_ukv.reshape(nl, -1, MLA_HEADS, MLA_NOPE + MLA_V)
    k_lay = jnp.concatenate([ukv[..., :MLA_NOPE], jnp.zeros_like(ukv[..., :LANES - MLA_NOPE])], axis=-1)
    v_h = ukv[..., MLA_NOPE:]
    zv = jnp.zeros(v_h.shape[:2] + (LANES,), F32)
    v_lay = jnp.concatenate([v_h[:, :, 0], v_h[:, :, 1], zv, v_h[:, :, 2], v_h[:, :, 3], zv], axis=-1)
    wkv = jnp.concatenate([k_lay.reshape(nl, -1, MLA_HEADS * LANES), v_lay], axis=-1).astype(BF16)

    wb = w_branch
    wb_d = wb[:, 3].reshape(nl, GQA_HEADS, GQA_HEAD_DIM, d)[:, (0, 2, 1, 3)].reshape(nl, BRANCH_W, d)
    w_branch_p = (0.5 * jnp.concatenate([wb[:, :3], wb_d[:, None]], axis=1)).astype(BF16)

    gap = EXPERTS_PER_GROUP - N_GROUPS
    rest = LANES - EXPERTS_PER_GROUP - N_EXPERTS
    wr = jnp.concatenate([w_group_router, jnp.zeros((nl, d, gap), F32), w_expert_router,
                          jnp.zeros((nl, d, rest), F32)], axis=-1)
    wr_hi = wr.astype(BF16)
    wr_lo = (wr - wr_hi.astype(F32)).astype(BF16)
    br = jnp.concatenate([b_group_router, jnp.zeros((nl, gap), F32), b_expert_router,
                          jnp.zeros((nl, rest), F32)], axis=-1)[:, None, :]

    tile = lambda g, reps: jnp.concatenate([g] * reps, axis=-1)[:, None, :]
    return {
        "w_in": w_in_ext,
        "sgu_gain": sgu_gain[:, None, :],
        "w_sgu": w_sgu.astype(BF16),
        "b_sgu": jnp.repeat(b_sgu.transpose(0, 2, 1), BRANCH_W // SGU_GROUPS, axis=-1),
        "q_gain": mla_q_gain[:, None, :], "wq": wq,
        "kv_gain": mla_kv_gain[:, None, :], "wkv": wkv,
        "gq_gain": tile(gqa_q_gain, 4), "gk_gain": tile(gqa_k_gain, 2),
        "w_conv": w_conv,
        "w_gate": (0.5 * w_gate).astype(BF16), "b_gate": 0.5 * b_gate,
        "w_branch": w_branch_p, "w_out": w_out.astype(BF16),
        "wr_hi": wr_hi, "wr_lo": wr_lo, "br": br,
    }


def _dispatch_plan(route, cnt, n_tiles, cap):
    blk = EXPERT_BLK
    tm = route.shape[1] // n_tiles
    counts = cnt[:, 0]
    padded = (counts + blk - 1) // blk * blk
    pend = jnp.cumsum(padded)
    pstart = pend - padded
    eid, rank = route[0:2], route[2:4]
    onehot = eid[:, :, None] == jnp.arange(N_EXPERTS, dtype=jnp.int32)
    dest = jnp.sum(jnp.where(onehot, pstart, 0), axis=-1) + rank
    pos = dest.reshape(2, n_tiles, tm).transpose(1, 0, 2).reshape(-1)
    n_used = pend[-1] // blk
    blk_first = jnp.minimum(jnp.arange(cap // blk, dtype=jnp.int32), n_used - 1) * blk
    blk_e = jnp.sum((blk_first[:, None] >= pend[None, :]).astype(jnp.int32), axis=1)
    blk_e = jnp.minimum(blk_e, N_EXPERTS - 1)
    return pos, pstart + counts, padded - counts, blk_e, n_used.reshape(1)


def kernel(x, c, ctx, c_ctx, w_ada, b_ada, w_in, sgu_gain, w_sgu, b_sgu, mla_q_gain, w_uq, mla_kv_gain, w_ukv, w_conv, gqa_q_gain, gqa_k_gain, w_gate, b_gate, w_branch, w_out, w_group_router, b_group_router, w_expert_router, b_expert_router, w_expert_gate_up, w_expert_down, final_gain):
    n_batch, seq, d = x.shape
    n_ctx = ctx.shape[1]
    n_tok = n_ctx + seq
    t = n_batch * n_tok
    depth = w_in.shape[0]
    tm = TOKEN_TILE
    assert n_ctx % tm == 0 and seq % tm == 0 and seq % GRID_W == 0 and tm % SGU_CHUNK == 0
    dims = (n_batch, n_tok, n_ctx)

    hs = jnp.concatenate([ctx, x], axis=1).reshape(t, d)
    tabs = _tables(n_ctx, seq)
    rows = -(-(n_batch + 1) // 8) * 8
    cvec = jnp.concatenate([c, c_ctx[None, :], jnp.zeros((rows - n_batch - 1, d), F32)], axis=0)
    mods = _ada_call(cvec, w_ada, b_ada).reshape(depth, rows, 6, d)
    weights = _prep_weights(w_in, sgu_gain, w_sgu, b_sgu, mla_q_gain, w_uq, mla_kv_gain, w_ukv, w_conv,
                            gqa_q_gain, gqa_k_gain, w_gate, b_gate, w_branch, w_out, w_group_router,
                            b_group_router, w_expert_router, b_expert_router)
    gain = final_gain[None, :]

    for l in range(depth):
        last = l == depth - 1
        ya, qm, km, vm, p, cb, qg, kg, vg = _in_call(hs, mods, tabs, weights, l, dims)
        yb, yd = _attn_call(qm, km, vm, qg, kg, vg, dims)
        hs1, hn2, logits = _merge_call(hs, mods, ya, yb, yd, p, cb, weights, l, dims, latent_only=last)
        route, wts, cnt = _route_call(logits)
        t_moe = hs1.shape[0]
        cap = -(-2 * t_moe // EXPERT_BLK) * EXPERT_BLK + N_EXPERTS * EXPERT_BLK
        pos, pad_start, pad_len, blk_e, n_used = _dispatch_plan(route, cnt, t_moe // tm, cap)
        xs = _dispatch_call(pos, pad_start, pad_len, n_used * EXPERT_BLK, hn2, cap)
        y = _expert_call(blk_e, n_used, xs, w_expert_gate_up, w_expert_down, l)
        hs = _combine_call(pos, y, hs1, wts.T, mods, l, gain, dims, final=last)
    return hs.reshape(n_batch, seq, d)
```

```python
import functools
import math

import jax
import jax.numpy as jnp
from jax import lax
from jax.experimental import pallas as pl
from jax.experimental.pallas import tpu as pltpu

F32 = jnp.float32
BF16 = jnp.bfloat16

GRID_W = 64
ROPE_THETA = 10000.0
NORM_EPS = 1e-6
BRANCH_W = 256
SGU_CHUNK = 128
SGU_GROUPS = 4
MLA_HEADS = 4
MLA_NOPE = 64
MLA_ROPE = 32
MLA_V = 64
GQA_HEADS = 4
GQA_HEAD_DIM = 64
N_GROUPS = 4
EXPERTS_PER_GROUP = 8
N_EXPERTS = N_GROUPS * EXPERTS_PER_GROUP
D_EXPERT = 256
LANES = 128
ROW_SUB = 8
TOKEN_TILE = 256
EXPERT_BLK = 256
VMEM_LIMIT = 56 * 1024 * 1024
LOG2E = 1.4426950408889634

C_AU, C_AV, C_MQ, C_MKV, C_KPA, C_KPB, C_CB, C_CC, C_CX, C_GQ, C_GK, C_GV, C_END = (
    0, 256, 512, 768, 896, 1024, 1152, 1408, 1664, 1920, 2176, 2304, 2432)


def _cparams(sem):
    return pltpu.CompilerParams(dimension_semantics=sem, vmem_limit_bytes=VMEM_LIMIT)


def _dot(a, b):
    return jnp.dot(a, b, preferred_element_type=F32)


def _dot_nt(a, b):
    return lax.dot_general(a, b, (((1,), (1,)), ((), ())), preferred_element_type=F32)


def _split_bf16(v):
    hi = v.astype(BF16)
    lo = (v - hi.astype(F32)).astype(BF16)
    return hi, lo


def _gelu(v):
    return 0.5 * v * (1.0 + jnp.tanh(math.sqrt(2.0 / math.pi) * (v + 0.044715 * (v * v * v))))


def _sigmoid(v):
    return 0.5 * jnp.tanh(0.5 * v) + 0.5


def _rms(v):
    return v * lax.rsqrt(jnp.mean(v * v, axis=-1, keepdims=True) + NORM_EPS)


def _const_spec(shape):
    nd = len(shape)
    return pl.BlockSpec(shape, lambda *_: (0,) * nd)


def _layer_spec(stacked, layer):
    nd = stacked.ndim - 1
    return pl.BlockSpec((None,) + stacked.shape[1:], lambda *_: (layer,) + (0,) * nd)


def _ada_kernel(c_ref, w_ref, b_ref, o_ref):
    cv = c_ref[...]
    s = cv * _sigmoid(cv)
    s_hi, s_lo = _split_bf16(s)
    w_hi, w_lo = _split_bf16(w_ref[...])
    o_ref[...] = _dot(s_hi, w_hi) + _dot(s_lo, w_hi) + _dot(s_hi, w_lo) + b_ref[...]


def _ada_call(cvec, w_ada, b_ada):
    n_layers, d, d6 = w_ada.shape
    rows = cvec.shape[0]
    tn = 1536
    return pl.pallas_call(
        _ada_kernel,
        grid=(n_layers, d6 // tn),
        in_specs=[pl.BlockSpec((rows, d), lambda l, j: (0, 0)),
                  pl.BlockSpec((None, d, tn), lambda l, j: (l, 0, j)),
                  pl.BlockSpec((None, 1, tn), lambda l, j: (l, 0, j))],
        out_specs=pl.BlockSpec((None, rows, tn), lambda l, j: (l, 0, j)),
        out_shape=jax.ShapeDtypeStruct((n_layers, rows, d6), F32),
        compiler_params=_cparams(("arbitrary", "arbitrary")),
        name="ada_mod",
    )(cvec, w_ada, b_ada.reshape(n_layers, 1, d6))


def _hs_specs(hs_parts, dims, tile_map=lambda i: i):
    n_batch, n_tok, n_ctx = dims
    tm = TOKEN_TILE
    d = hs_parts[0].shape[1]
    if len(hs_parts) == 1:
        return [pl.BlockSpec((tm, d), lambda i, *_: (tile_map(i), 0))]
    tiles, ctx_tiles = n_tok // tm, n_ctx // tm

    def ctx_map(i, *_):
        g = tile_map(i)
        return ((g // tiles) * ctx_tiles + jnp.minimum(g % tiles, ctx_tiles - 1), 0)

    def lat_map(i, *_):
        g = tile_map(i)
        return ((g // tiles) * (tiles - ctx_tiles) + jnp.maximum(g % tiles - ctx_tiles, 0), 0)
    return [pl.BlockSpec((tm, d), ctx_map), pl.BlockSpec((tm, d), lat_map)]


def _load_hs(hs_refs, start, n_ctx):
    if len(hs_refs) == 1:
        return hs_refs[0][...]
    return jnp.where(start < n_ctx, hs_refs[0][...], hs_refs[1][...])


def _group_hs(kernel, n_parts):
    def wrapped(*refs):
        return kernel(refs[:n_parts], *refs[n_parts:])
    return wrapped


def _in_kernel(hs_refs, mod_ref, cosm_ref, sinm_ref, cosg_ref, sing_ref, w_in_ref, sgu_gain_ref,
               w_sgu_ref, b_sgu_ref, qgain_ref, wq_ref, kvgain_ref, wkv_ref, gqgain_ref, gkgain_ref,
               gmat_ref,
               ya_ref, qm_ref, km_ref, vm_ref, p_ref, cb_ref, qg_ref, kg_ref, vg_ref, *, n_tok, n_ctx):
    tm = hs_refs[0].shape[0]
    mod = mod_ref[...]
    hs = _load_hs(hs_refs, (pl.program_id(0) * tm) % n_tok, n_ctx)
    hn = _rms(hs) * (1.0 + mod[1:2, :]) + mod[0:1, :]
    z = _dot(hn.astype(BF16), w_in_ref[...])

    av = _rms(_gelu(z[:, C_AV:C_MQ])) * sgu_gain_ref[...]
    av = av.astype(BF16)
    grp = lax.broadcasted_iota(jnp.int32, (SGU_CHUNK, BRANCH_W), 1) // (BRANCH_W // SGU_GROUPS)
    chunks = []
    for ci in range(tm // SGU_CHUNK):
        vc = av[ci * SGU_CHUNK:(ci + 1) * SGU_CHUNK, :]
        m = b_sgu_ref[...]
        for g in range(SGU_GROUPS):
            m = m + jnp.where(grp == g, _dot(w_sgu_ref[g], vc), 0.0)
        chunks.append(m)
    mixed = jnp.concatenate(chunks, axis=0)
    ya_ref[...] = (_gelu(z[:, C_AU:C_AV]) * mixed).astype(BF16)

    cosm = cosm_ref[...]
    sinm = sinm_ref[...]
    cos4 = jnp.concatenate([cosm] * MLA_HEADS, axis=1)
    sin4 = jnp.concatenate([sinm] * MLA_HEADS, axis=1)
    nq = (_rms(z[:, C_MQ:C_MKV]) * qgain_ref[...]).astype(BF16)
    qab = _dot(nq, wq_ref[...])
    hw = MLA_HEADS * LANES
    q_scale = (MLA_NOPE + MLA_ROPE) ** -0.5 * LOG2E
    qm_ref[...] = ((qab[:, :hw] * cos4 + qab[:, hw:] * sin4) * q_scale).astype(BF16)
    nkv = (_rms(z[:, C_MKV:C_KPA]) * kvgain_ref[...]).astype(BF16)
    kv = _dot(nkv, wkv_ref[...])
    kpe = z[:, C_KPA:C_KPB] * cosm + z[:, C_KPB:C_CB] * sinm
    km_ref[...] = (kv[:, :hw] + jnp.concatenate([kpe] * MLA_HEADS, axis=1)).astype(BF16)
    lane = lax.broadcasted_iota(jnp.int32, (tm, hw), 1)
    ones_col = jnp.where(lane % (2 * LANES) == LANES, 1.0, 0.0)
    vm_ref[...] = (kv[:, hw:] + ones_col).astype(BF16)

    p_ref[...] = (z[:, C_CC:C_CX] * z[:, C_CX:C_GQ]).astype(BF16)
    cb_ref[...] = z[:, C_CB:C_CC].astype(BF16)

    cosg = cosg_ref[...]
    sing = sing_ref[...]
    gmat = gmat_ref[...]

    def head_norm_rope(v, gain, cos_t, sin_t, scale):
        w = v.shape[1]
        sq_hi, sq_lo = _split_bf16(v * v)
        ms = _dot(sq_hi, gmat[:w, :w]) + _dot(sq_lo, gmat[:w, :w])
        y = v * lax.rsqrt(ms + NORM_EPS) * gain
        half = GQA_HEAD_DIM // 2
        ln = lax.broadcasted_iota(jnp.int32, v.shape, 1) % GQA_HEAD_DIM
        partner = jnp.where(ln < half, pltpu.roll(y, w - half, axis=1), pltpu.roll(y, half, axis=1))
        return ((y * cos_t + partner * sin_t) * scale).astype(BF16)

    cos2 = jnp.concatenate([cosg, cosg], axis=1)
    sin2 = jnp.concatenate([sing, sing], axis=1)
    qg_ref[...] = head_norm_rope(z[:, C_GQ:C_GK], gqgain_ref[...], cos2, sin2,
                                 GQA_HEAD_DIM ** -0.5 * LOG2E)
    kg_ref[...] = head_norm_rope(z[:, C_GK:C_GV], gkgain_ref[...], cosg, sing, 1.0)
    lane2 = lax.broadcasted_iota(jnp.int32, (tm, LANES), 1)
    vg_ref[...] = jnp.concatenate([z[:, C_GV:C_END], jnp.where(lane2 == 0, 1.0, 0.0)], axis=1).astype(BF16)


def _mod_spec(mods, layer, dims, tile_map=lambda i: i):
    n_batch, n_tok, n_ctx = dims

    def index_map(i, *_):
        start = tile_map(i) * TOKEN_TILE
        return (layer, jnp.where(start % n_tok < n_ctx, n_batch, start // n_tok), 0, 0)
    return pl.BlockSpec((None, None) + mods.shape[2:], index_map)


def _in_call(hs_parts, mods, tabs, weights, layer, dims):
    n_batch, n_tok, n_ctx = dims
    t = n_batch * n_tok
    tm = TOKEN_TILE
    tiles_per_seq = n_tok // tm
    row = lambda w: pl.BlockSpec((tm, w), lambda i: (i, 0))
    tab = pl.BlockSpec((tm, LANES), lambda i: (i % tiles_per_seq, 0))
    hw = MLA_HEADS * LANES
    outs = [(BRANCH_W, "ya"), (hw, "qm"), (hw, "km"), (hw, "vm"), (BRANCH_W, "p"), (BRANCH_W, "cb"),
            (BRANCH_W, "qg"), (LANES, "kg"), (2 * LANES, "vg")]
    names = ("w_in", "sgu_gain", "w_sgu", "b_sgu", "q_gain", "wq", "kv_gain", "wkv", "gq_gain", "gk_gain")
    params = [weights[k] for k in names]
    return pl.pallas_call(
        _group_hs(functools.partial(_in_kernel, n_tok=n_tok, n_ctx=n_ctx), len(hs_parts)),
        grid=(t // tm,),
        in_specs=_hs_specs(hs_parts, dims) + [_mod_spec(mods, layer, dims), tab, tab, tab, tab]
                 + [_layer_spec(a, layer) for a in params] + [_const_spec((BRANCH_W, BRANCH_W))],
        out_specs=[row(w) for w, _ in outs],
        out_shape=[jax.ShapeDtypeStruct((t, w), BF16) for w, _ in outs],
        compiler_params=_cparams(("arbitrary",)),
        name="in_proj",
    )(*hs_parts, mods, tabs["cosm"], tabs["sinm"], tabs["cosg"], tabs["sing"], *params, tabs["gmat"])


def _attend(q, k_ref, v_ref, nk, k_cols, v_cols):
    s = _dot_nt(q, k_ref[0:nk, k_cols])
    m = jnp.max(s, axis=-1, keepdims=True)
    r = _dot(jnp.exp2(s - m).astype(BF16), v_ref[0:nk, v_cols])
    return r[:, :LANES] * (1.0 / r[:, LANES:LANES + 1])


def _attn_kernel(qm_ref, km_ref, vm_ref, qg_ref, kg_ref, vg_ref, ob_ref, od_ref, *, ctx_tiles, n_ctx):
    def run_sample(s, nk):
        qm, km, vm, qg, kg, vg = (r.at[s] for r in (qm_ref, km_ref, vm_ref, qg_ref, kg_ref, vg_ref))
        lane = lax.broadcasted_iota(jnp.int32, (qm.shape[0], LANES), 1)
        outs = []
        for pair in range(MLA_HEADS // 2):
            v_cols = slice(pair * 2 * LANES, (pair + 1) * 2 * LANES)
            res = []
            for hh in range(2):
                cols = slice((pair * 2 + hh) * LANES, (pair * 2 + hh + 1) * LANES)
                res.append(_attend(qm[:, cols], km, vm, nk, cols, v_cols))
            outs.append(jnp.where(lane < MLA_V, res[0], res[1]))
        ob_ref[s] = jnp.concatenate(outs, axis=1).astype(BF16)

        every = slice(None)
        outs = []
        for grp in range(GQA_HEADS // 2):
            q = qg[:, grp * LANES:(grp + 1) * LANES]
            zero = jnp.zeros_like(q)
            r_lo = _attend(jnp.where(lane < GQA_HEAD_DIM, q, zero), kg, vg, nk, every, every)
            r_hi = _attend(jnp.where(lane >= GQA_HEAD_DIM, q, zero), kg, vg, nk, every, every)
            outs.append(jnp.where(lane < GQA_HEAD_DIM, r_lo, r_hi))
        od_ref[s] = jnp.concatenate(outs, axis=1).astype(BF16)

    def run(nk):
        for s in range(qm_ref.shape[0]):
            run_sample(s, nk)

    j = pl.program_id(1)
    pl.when(j < ctx_tiles)(lambda: run(n_ctx))
    pl.when(j >= ctx_tiles)(lambda: run(km_ref.shape[1]))


ATTN_SAMPLES = 2


def _attn_call(qm, km, vm, qg, kg, vg, dims):
    n_batch, n_tok, n_ctx = dims
    tq = TOKEN_TILE
    ns = ATTN_SAMPLES if n_batch % ATTN_SAMPLES == 0 else 1
    per_sample = lambda a: a.reshape(n_batch, n_tok, a.shape[1])
    qspec = lambda w: pl.BlockSpec((ns, tq, w), lambda b, j: (b, j, 0))
    kspec = lambda w: pl.BlockSpec((ns, n_tok, w), lambda b, j: (b, 0, 0))
    ob, od = pl.pallas_call(
        functools.partial(_attn_kernel, ctx_tiles=n_ctx // tq, n_ctx=n_ctx),
        grid=(n_batch // ns, n_tok // tq),
        in_specs=[qspec(qm.shape[1]), kspec(km.shape[1]), kspec(vm.shape[1]),
                  qspec(qg.shape[1]), kspec(kg.shape[1]), kspec(vg.shape[1])],
        out_specs=[qspec(BRANCH_W), qspec(BRANCH_W)],
        out_shape=[jax.ShapeDtypeStruct((n_batch, n_tok, BRANCH_W), BF16)] * 2,
        compiler_params=_cparams(("arbitrary", "arbitrary")),
        name="attn",
    )(*(per_sample(a) for a in (qm, km, vm, qg, kg, vg)))
    return ob.reshape(-1, BRANCH_W), od.reshape(-1, BRANCH_W)


def _merge_kernel(hs_refs, mod_ref, ya_ref, yb_ref, yd_ref, p_ref, pprev_ref, pnext_ref, cb_ref,
                  wconv_ref, wgate_ref, bgate_ref, wbranch_ref, wout_ref, wr_hi_ref, wr_lo_ref, br_ref,
                  hs1_ref, hn2_ref, logit_ref, *, n_tok, n_ctx, tile_map):
    tm = hs_refs[0].shape[0]
    mod = mod_ref[...]
    start = (tile_map(pl.program_id(0)) * tm) % n_tok
    hs = _load_hs(hs_refs, start, n_ctx)
    hn = (_rms(hs) * (1.0 + mod[1:2, :]) + mod[0:1, :]).astype(BF16)

    prev_ok = jnp.logical_and(start != 0, start != n_ctx)
    next_ok = jnp.logical_and(start + tm != n_ctx, start + tm != n_tok)
    p = p_ref[...].astype(F32)
    halo_prev = jnp.where(prev_ok, pprev_ref[...].astype(F32)[15:16, :], 0.0)
    halo_next = jnp.where(next_ok, pnext_ref[...].astype(F32)[0:1, :], 0.0)
    rowi = lax.broadcasted_iota(jnp.int32, p.shape, 0)
    prev = jnp.where(rowi == 0, halo_prev, pltpu.roll(p, 1, axis=0))
    nxt = jnp.where(rowi == tm - 1, halo_next, pltpu.roll(p, tm - 1, axis=0))
    wc = wconv_ref[...]
    yc = cb_ref[...].astype(F32) * (wc[0:1, :] * prev + wc[1:2, :] * p + wc[2:3, :] * nxt)

    branches = (ya_ref[...], yb_ref[...], yc.astype(BF16), yd_ref[...])
    merged = None
    for i, y in enumerate(branches):
        half_gate = _dot(hn, wgate_ref[i]) + bgate_ref[i:i + 1, :]
        term = (jnp.tanh(half_gate) + 1.0) * _dot(y, wbranch_ref[i])
        merged = term if merged is None else merged + term
    hs1 = hs + _dot(merged.astype(BF16), wout_ref[...]) * mod[2:3, :]
    hs1_ref[...] = hs1

    hn2 = _rms(hs1) * (1.0 + mod[4:5, :]) + mod[3:4, :]
    _rows_to_tiles(hn2_ref, hn2)

    x_hi, x_lo = _split_bf16(hn2)
    logit_ref[...] = (_dot(x_hi, wr_hi_ref[...]) + _dot(x_lo, wr_hi_ref[...]) + _dot(x_hi, wr_lo_ref[...])
                      + br_ref[...])


ROUTE_SUB = 256


def _route_kernel(logit_ref, route_ref, wts_ref, cnt_ref, carry_ref):
    tm = logit_ref.shape[0]
    lt = logit_ref[...].T
    sub = lax.broadcasted_iota(jnp.int32, (EXPERTS_PER_GROUP, tm), 0)
    neg = -jnp.inf
    top = lambda v: jnp.max(v, axis=0, keepdims=True)
    first = lambda hit: jnp.min(jnp.where(hit, sub, EXPERTS_PER_GROUP), axis=0, keepdims=True)
    gl = jnp.where(sub < N_GROUPS, lt[0:EXPERTS_PER_GROUP, :], neg)
    ge = jnp.exp(gl - top(gl))
    pg = ge / jnp.sum(ge, axis=0, keepdims=True)
    pg_top = top(pg)
    g_sel = first(pg == pg_top)
    le = lt[EXPERTS_PER_GROUP:2 * EXPERTS_PER_GROUP, :]
    for g in range(1, N_GROUPS):
        le = jnp.where(g_sel == g, lt[(g + 1) * EXPERTS_PER_GROUP:(g + 2) * EXPERTS_PER_GROUP, :], le)
    ee = jnp.exp(le - top(le))
    pe = ee / jnp.sum(ee, axis=0, keepdims=True)
    p1 = top(pe)
    i1 = first(pe == p1)
    pe2 = jnp.where(sub != i1, pe, neg)
    p2 = top(pe2)
    i2 = first(jnp.logical_and(sub != i1, pe2 == p2))
    denom = p1 + p2
    e1 = g_sel * EXPERTS_PER_GROUP + i1
    e2 = g_sel * EXPERTS_PER_GROUP + i2
    wts_ref[...] = jnp.where(sub == 0, pg_top * p1 / denom, jnp.where(sub == 1, pg_top * p2 / denom, 0.0))

    @pl.when(pl.program_id(0) == 0)
    def _():
        carry_ref[...] = jnp.zeros_like(carry_ref)

    ts = ROUTE_SUB
    erow = lax.broadcasted_iota(jnp.int32, (N_EXPERTS, ts), 0)
    ri = lax.broadcasted_iota(jnp.int32, (ts, ts + LANES), 0)
    ci = lax.broadcasted_iota(jnp.int32, (ts, ts + LANES), 1)
    before = jnp.where(jnp.logical_or(ri < ci, ci >= ts), 1.0, 0.0).astype(BF16)
    carry = carry_ref[...]
    r1s, r2s = [], []
    for s in range(tm // ts):
        oh1 = jnp.where(erow == e1[:, s * ts:(s + 1) * ts], 1.0, 0.0)
        oh2 = jnp.where(erow == e2[:, s * ts:(s + 1) * ts], 1.0, 0.0)
        pref = _dot(jnp.concatenate([oh1, oh2], axis=0).astype(BF16), before)
        c1 = pref[:N_EXPERTS, ts:]
        c2 = pref[N_EXPERTS:, ts:]
        r1s.append(jnp.sum(oh1 * (carry[:, 0:1] + pref[:N_EXPERTS, :ts]), axis=0, keepdims=True))
        r2s.append(jnp.sum(oh2 * ((carry + c1)[:, 0:1] + pref[N_EXPERTS:, :ts]), axis=0, keepdims=True))
        carry = carry + c1 + c2
    r1 = jnp.concatenate(r1s, axis=1)
    r2 = jnp.concatenate(r2s, axis=1)
    carry_ref[...] = carry
    cnt_ref[...] = carry.astype(jnp.int32)
    route_ref[...] = jnp.where(sub == 0, e1, jnp.where(sub == 1, e2,
                               jnp.where(sub == 2, r1.astype(jnp.int32),
                                         jnp.where(sub == 3, r2.astype(jnp.int32), 0))))


def _route_call(logits):
    t = logits.shape[0]
    tr = 1024 if t % 1024 == 0 else ROUTE_SUB
    return pl.pallas_call(
        _route_kernel,
        grid=(t // tr,),
        in_specs=[pl.BlockSpec((tr, LANES), lambda i: (i, 0))],
        out_specs=[pl.BlockSpec((8, tr), lambda i: (0, i)), pl.BlockSpec((8, tr), lambda i: (0, i)),
                   pl.BlockSpec((N_EXPERTS, LANES), lambda i: (0, 0))],
        out_shape=[jax.ShapeDtypeStruct((8, t), jnp.int32), jax.ShapeDtypeStruct((8, t), F32),
                   jax.ShapeDtypeStruct((N_EXPERTS, LANES), jnp.int32)],
        scratch_shapes=[pltpu.VMEM((N_EXPERTS, LANES), F32)],
        compiler_params=_cparams(("arbitrary",)),
        name="route",
    )(logits)


def _token_tiles(dims, latent_only):
    n_batch, n_tok, n_ctx = dims
    tiles, ctx_tiles = n_tok // TOKEN_TILE, n_ctx // TOKEN_TILE
    if not latent_only:
        return n_batch * tiles, lambda i: i
    lat = tiles - ctx_tiles
    return n_batch * lat, lambda i: (i // lat) * tiles + ctx_tiles + i % lat


def _merge_call(hs_parts, mods, ya, yb, yd, p, cb, weights, layer, dims, latent_only):
    n_batch, n_tok, n_ctx = dims
    t, d = n_batch * n_tok, hs_parts[0].shape[1]
    tm = TOKEN_TILE
    sub = 16
    n_sub = t // sub
    n_steps, tile_map = _token_tiles(dims, latent_only)
    t_out = n_steps * tm
    row_in = lambda w: pl.BlockSpec((tm, w), lambda i: (tile_map(i), 0))
    row = lambda w: pl.BlockSpec((tm, w), lambda i: (i, 0))
    names = ("w_conv", "w_gate", "b_gate", "w_branch", "w_out", "wr_hi", "wr_lo", "br")
    params = [weights[k] for k in names]
    return pl.pallas_call(
        _group_hs(functools.partial(_merge_kernel, n_tok=n_tok, n_ctx=n_ctx, tile_map=tile_map), len(hs_parts)),
        grid=(n_steps,),
        in_specs=_hs_specs(hs_parts, dims, tile_map) + [
                  _mod_spec(mods, layer, dims, tile_map),
                  row_in(BRANCH_W), row_in(BRANCH_W), row_in(BRANCH_W), row_in(BRANCH_W),
                  pl.BlockSpec((sub, BRANCH_W),
                               lambda i: (jnp.maximum(tile_map(i) * (tm // sub) - 1, 0), 0)),
                  pl.BlockSpec((sub, BRANCH_W),
                               lambda i: (jnp.minimum((tile_map(i) + 1) * (tm // sub), n_sub - 1), 0)),
                  row_in(BRANCH_W)] + [_layer_spec(a, layer) for a in params],
        out_specs=[row(d), pl.BlockSpec((tm * ROW_SUB, LANES), lambda i: (i, 0)), row(LANES)],
        out_shape=[jax.ShapeDtypeStruct((t_out, d), F32), jax.ShapeDtypeStruct((t_out * ROW_SUB, LANES), F32),
                   jax.ShapeDtypeStruct((t_out, LANES), F32)],
        compiler_params=_cparams(("arbitrary",)),
        name="merge_route",
    )(*hs_parts, mods, ya, yb, yd, p, p, p, cb, *params)


PAD_BITS = tuple(1 << b for b in reversed(range(EXPERT_BLK.bit_length() - 1)))


def _rows_to_tiles(ref, v):
    for ck in range(ROW_SUB):
        ref[pl.ds(ck, v.shape[0], stride=ROW_SUB), :] = v[:, ck * LANES:(ck + 1) * LANES]


def _tiles_to_rows(ref, first, n):
    return jnp.concatenate([ref[pl.ds(first * ROW_SUB + ck, n, stride=ROW_SUB), :] for ck in range(ROW_SUB)],
                           axis=1)


def _tile_of(ref, row, n=1):
    return ref.at[pl.ds(pl.multiple_of(row * ROW_SUB, ROW_SUB), n * ROW_SUB)]


def _tiles_wait(like, sem):
    pltpu.make_async_copy(like, like, sem).wait()


def _dispatch_kernel(pos_ref, pad_start_ref, pad_len_ref, tail_ref, x_ref, xs_hbm, stage, zeros, sems, pad_sem):
    i = pl.program_id(0)
    n = pl.num_programs(0)
    tm = x_ref.shape[0] // ROW_SUB
    slot = i % 2

    def pad_copies(e, fn):
        start = pad_start_ref[e]
        ln = pad_len_ref[e]
        for b in PAD_BITS:
            @pl.when((ln & b) != 0)
            def _():
                off = ln & ~(2 * b - 1)
                fn(pltpu.make_async_copy(_tile_of(zeros, 0, b), _tile_of(xs_hbm, start + off, b), pad_sem))

    def for_each_expert(fn):
        def body(e, carry):
            pad_copies(e, fn)
            return carry
        lax.fori_loop(0, N_EXPERTS, body, 0)

        def tail(j, carry):
            fn(pltpu.make_async_copy(zeros, _tile_of(xs_hbm, j * PAD_BITS[0], PAD_BITS[0]), pad_sem))
            return carry
        lax.fori_loop(tail_ref[0] // PAD_BITS[0], xs_hbm.shape[0] // (ROW_SUB * PAD_BITS[0]), tail, 0)

    @pl.when(i == 0)
    def _():
        zeros[...] = jnp.zeros_like(zeros)
        for_each_expert(lambda cp: cp.start())

    @pl.when(i >= 2)
    def _():
        _tiles_wait(stage, sems.at[slot])

    stage[slot] = x_ref[...]

    def push(r, carry):
        for k in range(2):
            dst = pos_ref[i * (2 * tm) + k * tm + r]
            pltpu.make_async_copy(_tile_of(stage.at[slot], r), _tile_of(xs_hbm, dst),
                                  sems.at[slot]).start(priority=k)
        return carry
    lax.fori_loop(0, tm, push, 0, unroll=8)

    @pl.when(i == n - 1)
    def _():
        _tiles_wait(stage, sems.at[slot])

        @pl.when(n >= 2)
        def _():
            _tiles_wait(stage, sems.at[1 - slot])
        for_each_expert(lambda cp: cp.wait())


def _dispatch_call(pos, pad_start, pad_len, tail_start, x_tiles, cap):
    tm = TOKEN_TILE
    grid_spec = pltpu.PrefetchScalarGridSpec(
        num_scalar_prefetch=4,
        grid=(x_tiles.shape[0] // (tm * ROW_SUB),),
        in_specs=[pl.BlockSpec((tm * ROW_SUB, LANES), lambda i, *_: (i, 0))],
        out_specs=pl.BlockSpec(memory_space=pl.ANY),
        scratch_shapes=[pltpu.VMEM((2, tm * ROW_SUB, LANES), F32), pltpu.VMEM((PAD_BITS[0] * ROW_SUB, LANES), F32),
                        pltpu.SemaphoreType.DMA((2,)), pltpu.SemaphoreType.DMA(())],
    )
    return pl.pallas_call(
        _dispatch_kernel,
        grid_spec=grid_spec,
        out_shape=jax.ShapeDtypeStruct((cap * ROW_SUB, LANES), F32),
        compiler_params=_cparams(("arbitrary",)),
        name="dispatch",
    )(pos, pad_start, pad_len, tail_start, x_tiles)


def _expert_kernel(blk_e_ref, n_used_ref, x_ref, wgu_ref, wdn_ref, y_ref, wgu_bf, wdn_bf):
    i = pl.program_id(0)

    @pl.when(jnp.logical_or(i == 0, blk_e_ref[i] != blk_e_ref[jnp.maximum(i - 1, 0)]))
    def _():
        wgu_bf[...] = wgu_ref[...].astype(BF16)
        wdn_bf[...] = wdn_ref[...].astype(BF16)

    @pl.when(i < n_used_ref[0])
    def _():
        xe = _tiles_to_rows(x_ref, 0, EXPERT_BLK).astype(BF16)
        gu = _dot(xe, wgu_bf[...])
        g = gu[:, :D_EXPERT]
        act = (g * _sigmoid(g) * gu[:, D_EXPERT:]).astype(BF16)
        _rows_to_tiles(y_ref, _dot(act, wdn_bf[...]))

    @pl.when(i >= n_used_ref[0])
    def _():
        y_ref[...] = jnp.zeros_like(y_ref)


def _expert_call(blk_e, n_used, xs, w_gu, w_dn, layer):
    d = w_gu.shape[2]
    blk = EXPERT_BLK
    used = lambda i, be, nu: (jnp.minimum(i, nu[0] - 1), 0)
    grid_spec = pltpu.PrefetchScalarGridSpec(
        num_scalar_prefetch=2,
        grid=(xs.shape[0] // (blk * ROW_SUB),),
        in_specs=[pl.BlockSpec((blk * ROW_SUB, LANES), used),
                  pl.BlockSpec((None, None, d, 2 * D_EXPERT), lambda i, be, nu: (layer, be[i], 0, 0)),
                  pl.BlockSpec((None, None, D_EXPERT, d), lambda i, be, nu: (layer, be[i], 0, 0))],
        out_specs=pl.BlockSpec((blk * ROW_SUB, LANES), lambda i, be, nu: (i, 0)),
        scratch_shapes=[pltpu.VMEM((d, 2 * D_EXPERT), BF16), pltpu.VMEM((D_EXPERT, d), BF16)],
    )
    return pl.pallas_call(
        _expert_kernel,
        grid_spec=grid_spec,
        out_shape=jax.ShapeDtypeStruct(xs.shape, F32),
        compiler_params=_cparams(("arbitrary",)),
        name="experts",
    )(blk_e, n_used, xs, w_gu, w_dn)


def _combine_kernel(pos_ref, y_hbm, hs1_ref, wts_ref, mod_ref, gain_ref, o_ref, gbuf, sems, *,
                    final):
    i = pl.program_id(0)
    n = pl.num_programs(0)
    tm = hs1_ref.shape[0]
    slot = i % 2

    def start(step, s):
        base = step * (2 * tm)

        def body(r, carry):
            for k in range(2):
                src = pos_ref[base + k * tm + r]
                pltpu.make_async_copy(_tile_of(y_hbm, src), _tile_of(gbuf.at[s], k * tm + r),
                                      sems.at[s]).start(priority=k)
            return carry
        lax.fori_loop(0, tm, body, 0, unroll=8)

    @pl.when(i == 0)
    def _():
        start(0, 0)

    @pl.when(i + 1 < n)
    def _():
        start(i + 1, 1 - slot)

    _tiles_wait(gbuf.at[slot], sems.at[slot])
    wts = wts_ref[...]
    g = gbuf.at[slot]
    f = wts[:, 0:1] * _tiles_to_rows(g, 0, tm) + wts[:, 1:2] * _tiles_to_rows(g, tm, tm)
    out = hs1_ref[...] + f * mod_ref[...][5:6, :]
    if final:
        out = _rms(out) * gain_ref[...]
    o_ref[...] = out


def _combine_call(pos, y, hs1, wts, mods, layer, gain, dims, final):
    t, d = hs1.shape
    tm = TOKEN_TILE
    n_steps, tile_map = _token_tiles(dims, final)
    assert n_steps * tm == t
    grid_spec = pltpu.PrefetchScalarGridSpec(
        num_scalar_prefetch=1,
        grid=(n_steps,),
        in_specs=[pl.BlockSpec(memory_space=pl.ANY),
                  pl.BlockSpec((tm, d), lambda i, pos: (i, 0)),
                  pl.BlockSpec((tm, wts.shape[1]), lambda i, pos: (i, 0)),
                  _mod_spec(mods, layer, dims, tile_map),
                  pl.BlockSpec((1, d), lambda i, pos: (0, 0))],
        out_specs=pl.BlockSpec((tm, d), lambda i, pos: (i, 0)),
        scratch_shapes=[pltpu.VMEM((2, 2 * tm * ROW_SUB, LANES), F32), pltpu.SemaphoreType.DMA((2,))],
    )
    return pl.pallas_call(
        functools.partial(_combine_kernel, final=final),
        grid_spec=grid_spec,
        out_shape=jax.ShapeDtypeStruct((n_steps * tm, d), F32),
        compiler_params=_cparams(("arbitrary",)),
        name="combine_final" if final else "combine",
    )(pos, y, hs1, wts, mods, gain)


def _rope_angles(n_ctx, rows, rot_dim):
    n_freq = rot_dim // 4
    freqs = ROPE_THETA ** (-jnp.arange(n_freq, dtype=F32) / n_freq)
    row = jnp.repeat(jnp.arange(rows, dtype=F32), GRID_W)
    col = (jnp.arange(rows * GRID_W) % GRID_W).astype(F32)
    ang = jnp.concatenate([row[:, None] * freqs, col[:, None] * freqs], axis=-1)
    ang = jnp.concatenate([jnp.zeros((n_ctx, rot_dim // 2), F32), ang], axis=0)
    return jnp.cos(ang), jnp.sin(ang)


def _tables(n_ctx, seq):
    n = n_ctx + seq
    cm, sm = _rope_angles(n_ctx, seq // GRID_W, MLA_ROPE)
    cg, sg = _rope_angles(n_ctx, seq // GRID_W, GQA_HEAD_DIM)
    one, zero = jnp.ones((n, MLA_NOPE), F32), jnp.zeros((n, MLA_NOPE), F32)
    pad = jnp.zeros((n, LANES - MLA_NOPE - MLA_ROPE), F32)
    gi = jnp.arange(BRANCH_W) // GQA_HEAD_DIM
    return {
        "cosm": jnp.concatenate([one, cm, cm, pad], axis=1),
        "sinm": jnp.concatenate([zero, -sm, sm, pad], axis=1),
        "cosg": jnp.concatenate([cg, cg, cg, cg], axis=1),
        "sing": jnp.concatenate([-sg, sg, -sg, sg], axis=1),
        "gmat": jnp.where(gi[:, None] == gi[None, :], 1.0 / GQA_HEAD_DIM, 0.0).astype(BF16),
    }


def _prep_weights(w_in, sgu_gain, w_sgu, b_sgu, mla_q_gain, w_uq, mla_kv_gain, w_ukv, w_conv,
                  gqa_q_gain, gqa_k_gain, w_gate, b_gate, w_branch, w_out, w_group_router,
                  b_group_router, w_expert_router, b_expert_router):
    nl, d, _ = w_in.shape
    zc = lambda n: jnp.zeros((nl, d, n), F32)
    half = MLA_ROPE // 2
    kpe = w_in[..., 896:928]
    gq = w_in[..., 1696:1952].reshape(nl, d, GQA_HEADS, GQA_HEAD_DIM)[:, :, (0, 2, 1, 3), :].reshape(nl, d, BRANCH_W)
    w_in_ext = jnp.concatenate([
        w_in[..., 0:896],
        zc(MLA_NOPE), kpe, zc(LANES - MLA_NOPE - MLA_ROPE),
        zc(MLA_NOPE), kpe[..., half:], kpe[..., :half], zc(LANES - MLA_NOPE - MLA_ROPE),
        w_in[..., 928:1696], gq, w_in[..., 1952:2208]], axis=-1).astype(BF16)

    dq = MLA_NOPE + MLA_ROPE
    uq = w_uq.reshape(nl, -1, MLA_HEADS, dq)
    zq = jnp.zeros(uq.shape[:3] + (LANES - dq,), F32)
    wq_a = jnp.concatenate([uq, zq], axis=-1)
    wq_b = jnp.concatenate([jnp.zeros_like(uq[..., :MLA_NOPE]), uq[..., MLA_NOPE + half:],
                            uq[..., MLA_NOPE:MLA_NOPE + half], zq], axis=-1)
    wq = jnp.concatenate([wq_a.reshape(nl, -1, MLA_HEADS * LANES),
                          wq_b.reshape(nl, -1, MLA_HEADS * LANES)], axis=-1).astype(BF16)

    ukv = w_ukv.reshape(nl, -1, MLA_HEADS, MLA_NOPE + MLA_V)
    k_lay = jnp.concatenate([ukv[..., :MLA_NOPE], jnp.zeros_like(ukv[..., :LANES - MLA_NOPE])], axis=-1)
    v_h = ukv[..., MLA_NOPE:]
    zv = jnp.zeros(v_h.shape[:2] + (LANES,), F32)
    v_lay = jnp.concatenate([v_h[:, :, 0], v_h[:, :, 1], zv, v_h[:, :, 2], v_h[:, :, 3], zv], axis=-1)
    wkv = jnp.concatenate([k_lay.reshape(nl, -1, MLA_HEADS * LANES), v_lay], axis=-1).astype(BF16)

    wb = w_branch
    wb_d = wb[:, 3].reshape(nl, GQA_HEADS, GQA_HEAD_DIM, d)[:, (0, 2, 1, 3)].reshape(nl, BRANCH_W, d)
    w_branch_p = (0.5 * jnp.concatenate([wb[:, :3], wb_d[:, None]], axis=1)).astype(BF16)

    gap = EXPERTS_PER_GROUP - N_GROUPS
    rest = LANES - EXPERTS_PER_GROUP - N_EXPERTS
    wr = jnp.concatenate([w_group_router, jnp.zeros((nl, d, gap), F32), w_expert_router,
                          jnp.zeros((nl, d, rest), F32)], axis=-1)
    wr_hi = wr.astype(BF16)
    wr_lo = (wr - wr_hi.astype(F32)).astype(BF16)
    br = jnp.concatenate([b_group_router, jnp.zeros((nl, gap), F32), b_expert_router,
                          jnp.zeros((nl, rest), F32)], axis=-1)[:, None, :]

    tile = lambda g, reps: jnp.concatenate([g] * reps, axis=-1)[:, None, :]
    return {
        "w_in": w_in_ext,
        "sgu_gain": sgu_gain[:, None, :],
        "w_sgu": w_sgu.astype(BF16),
        "b_sgu": jnp.repeat(b_sgu.transpose(0, 2, 1), BRANCH_W // SGU_GROUPS, axis=-1),
        "q_gain": mla_q_gain[:, None, :], "wq": wq,
        "kv_gain": mla_kv_gain[:, None, :], "wkv": wkv,
        "gq_gain": tile(gqa_q_gain, 4), "gk_gain": tile(gqa_k_gain, 2),
        "w_conv": w_conv,
        "w_gate": (0.5 * w_gate).astype(BF16), "b_gate": 0.5 * b_gate,
        "w_branch": w_branch_p, "w_out": w_out.astype(BF16),
        "wr_hi": wr_hi, "wr_lo": wr_lo, "br": br,
    }


def _dispatch_plan(route, cnt, n_tiles, cap):
    blk = EXPERT_BLK
    tm = route.shape[1] // n_tiles
    counts = cnt[:, 0]
    padded = (counts + blk - 1) // blk * blk
    pend = jnp.cumsum(padded)
    pstart = pend - padded
    eid, rank = route[0:2], route[2:4]
    onehot = eid[:, :, None] == jnp.arange(N_EXPERTS, dtype=jnp.int32)
    dest = jnp.sum(jnp.where(onehot, pstart, 0), axis=-1) + rank
    pos = dest.reshape(2, n_tiles, tm).transpose(1, 0, 2).reshape(-1)
    n_used = pend[-1] // blk
    blk_first = jnp.minimum(jnp.arange(cap // blk, dtype=jnp.int32), n_used - 1) * blk
    blk_e = jnp.sum((blk_first[:, None] >= pend[None, :]).astype(jnp.int32), axis=1)
    blk_e = jnp.minimum(blk_e, N_EXPERTS - 1)
    return pos, pstart + counts, padded - counts, blk_e, n_used.reshape(1)


def kernel(x, c, ctx, c_ctx, w_ada, b_ada, w_in, sgu_gain, w_sgu, b_sgu, mla_q_gain, w_uq, mla_kv_gain, w_ukv, w_conv, gqa_q_gain, gqa_k_gain, w_gate, b_gate, w_branch, w_out, w_group_router, b_group_router, w_expert_router, b_expert_router, w_expert_gate_up, w_expert_down, final_gain):
    n_batch, seq, d = x.shape
    n_ctx = ctx.shape[1]
    n_tok = n_ctx + seq
    t = n_batch * n_tok
    depth = w_in.shape[0]
    tm = TOKEN_TILE
    assert n_ctx % tm == 0 and seq % tm == 0 and seq % GRID_W == 0 and tm % SGU_CHUNK == 0
    dims = (n_batch, n_tok, n_ctx)

    hs = [ctx.reshape(n_batch * n_ctx, d), x.reshape(n_batch * seq, d)]
    tabs = _tables(n_ctx, seq)
    rows = -(-(n_batch + 1) // 8) * 8
    cvec = jnp.concatenate([c, c_ctx[None, :], jnp.zeros((rows - n_batch - 1, d), F32)], axis=0)
    mods = _ada_call(cvec, w_ada, b_ada).reshape(depth, rows, 6, d)
    weights = _prep_weights(w_in, sgu_gain, w_sgu, b_sgu, mla_q_gain, w_uq, mla_kv_gain, w_ukv, w_conv,
                            gqa_q_gain, gqa_k_gain, w_gate, b_gate, w_branch, w_out, w_group_router,
                            b_group_router, w_expert_router, b_expert_router)
    gain = final_gain[None, :]

    for l in range(depth):
        last = l == depth - 1
        ya, qm, km, vm, p, cb, qg, kg, vg = _in_call(hs, mods, tabs, weights, l, dims)
        yb, yd = _attn_call(qm, km, vm, qg, kg, vg, dims)
        hs1, hn2, logits = _merge_call(hs, mods, ya, yb, yd, p, cb, weights, l, dims, latent_only=last)
        route, wts, cnt = _route_call(logits)
        t_moe = hs1.shape[0]
        cap = -(-2 * t_moe // EXPERT_BLK) * EXPERT_BLK + N_EXPERTS * EXPERT_BLK
        pos, pad_start, pad_len, blk_e, n_used = _dispatch_plan(route, cnt, t_moe // tm, cap)
        xs = _dispatch_call(pos, pad_start, pad_len, n_used * EXPERT_BLK, hn2, cap)
        y = _expert_call(blk_e, n_used, xs, w_expert_gate_up, w_expert_down, l)
        hs = [_combine_call(pos, y, hs1, wts.T, mods, l, gain, dims, final=last)]
    return hs[0].reshape(n_batch, seq, d)
```

```python
import functools
import math

import jax
import jax.numpy as jnp
from jax import lax
from jax.experimental import pallas as pl
from jax.experimental.pallas import tpu as pltpu

F32 = jnp.float32
BF16 = jnp.bfloat16

GRID_W = 64
ROPE_THETA = 10000.0
NORM_EPS = 1e-6
BRANCH_W = 256
SGU_CHUNK = 128
SGU_GROUPS = 4
MLA_HEADS = 4
MLA_NOPE = 64
MLA_ROPE = 32
MLA_V = 64
GQA_HEADS = 4
GQA_HEAD_DIM = 64
N_GROUPS = 4
EXPERTS_PER_GROUP = 8
N_EXPERTS = N_GROUPS * EXPERTS_PER_GROUP
D_EXPERT = 256
LANES = 128
ROW_SUB = 8
TOKEN_TILE = 256
EXPERT_BLK = 256
VMEM_LIMIT = 56 * 1024 * 1024
LOG2E = 1.4426950408889634

C_AU, C_AV, C_MQ, C_MKV, C_KPA, C_KPB, C_CB, C_CC, C_CX, C_GQ, C_GK, C_GV, C_END = (
    0, 256, 512, 768, 896, 1024, 1152, 1408, 1664, 1920, 2176, 2304, 2432)


def _cparams(sem):
    return pltpu.CompilerParams(dimension_semantics=sem, vmem_limit_bytes=VMEM_LIMIT)


def _dot(a, b):
    return jnp.dot(a, b, preferred_element_type=F32)


def _dot_nt(a, b):
    return lax.dot_general(a, b, (((1,), (1,)), ((), ())), preferred_element_type=F32)


def _split_bf16(v):
    hi = v.astype(BF16)
    lo = (v - hi.astype(F32)).astype(BF16)
    return hi, lo


def _gelu(v):
    return 0.5 * v * (1.0 + jnp.tanh(math.sqrt(2.0 / math.pi) * (v + 0.044715 * (v * v * v))))


def _sigmoid(v):
    return 0.5 * jnp.tanh(0.5 * v) + 0.5


def _rms(v):
    return v * lax.rsqrt(jnp.mean(v * v, axis=-1, keepdims=True) + NORM_EPS)


def _const_spec(shape):
    nd = len(shape)
    return pl.BlockSpec(shape, lambda *_: (0,) * nd)


def _layer_spec(stacked, layer):
    nd = stacked.ndim - 1
    return pl.BlockSpec((None,) + stacked.shape[1:], lambda *_: (layer,) + (0,) * nd)


def _ada_kernel(c_ref, w_ref, b_ref, o_ref):
    cv = c_ref[...]
    s = cv * _sigmoid(cv)
    s_hi, s_lo = _split_bf16(s)
    w_hi, w_lo = _split_bf16(w_ref[...])
    o_ref[...] = _dot(s_hi, w_hi) + _dot(s_lo, w_hi) + _dot(s_hi, w_lo) + b_ref[...]


def _ada_call(cvec, w_ada, b_ada):
    n_layers, d, d6 = w_ada.shape
    rows = cvec.shape[0]
    tn = 1536
    return pl.pallas_call(
        _ada_kernel,
        grid=(n_layers, d6 // tn),
        in_specs=[pl.BlockSpec((rows, d), lambda l, j: (0, 0)),
                  pl.BlockSpec((None, d, tn), lambda l, j: (l, 0, j)),
                  pl.BlockSpec((None, 1, tn), lambda l, j: (l, 0, j))],
        out_specs=pl.BlockSpec((None, rows, tn), lambda l, j: (l, 0, j)),
        out_shape=jax.ShapeDtypeStruct((n_layers, rows, d6), F32),
        compiler_params=_cparams(("arbitrary", "arbitrary")),
        name="ada_mod",
    )(cvec, w_ada, b_ada.reshape(n_layers, 1, d6))


def _hs_specs(hs_parts, dims, tile_map=lambda i: i):
    n_batch, n_tok, n_ctx = dims
    tm = TOKEN_TILE
    d = hs_parts[0].shape[1]
    if len(hs_parts) == 1:
        return [pl.BlockSpec((tm, d), lambda i, *_: (tile_map(i), 0))]
    tiles, ctx_tiles = n_tok // tm, n_ctx // tm

    def ctx_map(i, *_):
        g = tile_map(i)
        return ((g // tiles) * ctx_tiles + jnp.minimum(g % tiles, ctx_tiles - 1), 0)

    def lat_map(i, *_):
        g = tile_map(i)
        return ((g // tiles) * (tiles - ctx_tiles) + jnp.maximum(g % tiles - ctx_tiles, 0), 0)
    return [pl.BlockSpec((tm, d), ctx_map), pl.BlockSpec((tm, d), lat_map)]


def _load_hs(hs_refs, start, n_ctx):
    if len(hs_refs) == 1:
        return hs_refs[0][...]
    return jnp.where(start < n_ctx, hs_refs[0][...], hs_refs[1][...])


def _group_hs(kernel, n_parts):
    def wrapped(*refs):
        return kernel(refs[:n_parts], *refs[n_parts:])
    return wrapped


def _in_kernel(hs_refs, mod_ref, cosm_ref, sinm_ref, cosg_ref, sing_ref, w_in_ref, sgu_gain_ref,
               w_sgu_ref, b_sgu_ref, qgain_ref, wq_ref, kvgain_ref, wkv_ref, gqgain_ref, gkgain_ref,
               gmat_ref,
               ya_ref, qm_ref, km_ref, vm_ref, p_ref, cb_ref, qg_ref, kg_ref, vg_ref, *, n_tok, n_ctx):
    tm = hs_refs[0].shape[0]
    mod = mod_ref[...]
    hs = _load_hs(hs_refs, (pl.program_id(0) * tm) % n_tok, n_ctx)
    hn = _rms(hs) * (1.0 + mod[1:2, :]) + mod[0:1, :]
    z = _dot(hn.astype(BF16), w_in_ref[...])

    av = _rms(_gelu(z[:, C_AV:C_MQ])) * sgu_gain_ref[...]
    av = av.astype(BF16)
    grp = lax.broadcasted_iota(jnp.int32, (SGU_CHUNK, BRANCH_W), 1) // (BRANCH_W // SGU_GROUPS)
    chunks = []
    for ci in range(tm // SGU_CHUNK):
        vc = av[ci * SGU_CHUNK:(ci + 1) * SGU_CHUNK, :]
        m = b_sgu_ref[...]
        for g in range(SGU_GROUPS):
            m = m + jnp.where(grp == g, _dot(w_sgu_ref[g], vc), 0.0)
        chunks.append(m)
    mixed = jnp.concatenate(chunks, axis=0)
    ya_ref[...] = (_gelu(z[:, C_AU:C_AV]) * mixed).astype(BF16)

    cosm = cosm_ref[...]
    sinm = sinm_ref[...]
    cos4 = jnp.concatenate([cosm] * MLA_HEADS, axis=1)
    sin4 = jnp.concatenate([sinm] * MLA_HEADS, axis=1)
    nq = (_rms(z[:, C_MQ:C_MKV]) * qgain_ref[...]).astype(BF16)
    qab = _dot(nq, wq_ref[...])
    hw = MLA_HEADS * LANES
    q_scale = (MLA_NOPE + MLA_ROPE) ** -0.5 * LOG2E
    qm_ref[...] = ((qab[:, :hw] * cos4 + qab[:, hw:] * sin4) * q_scale).astype(BF16)
    nkv = (_rms(z[:, C_MKV:C_KPA]) * kvgain_ref[...]).astype(BF16)
    kv = _dot(nkv, wkv_ref[...])
    kpe = z[:, C_KPA:C_KPB] * cosm + z[:, C_KPB:C_CB] * sinm
    km_ref[...] = (kv[:, :hw] + jnp.concatenate([kpe] * MLA_HEADS, axis=1)).astype(BF16)
    lane = lax.broadcasted_iota(jnp.int32, (tm, hw), 1)
    ones_col = jnp.where(lane % (2 * LANES) == LANES, 1.0, 0.0)
    vm_ref[...] = (kv[:, hw:] + ones_col).astype(BF16)

    p_ref[...] = (z[:, C_CC:C_CX] * z[:, C_CX:C_GQ]).astype(BF16)
    cb_ref[...] = z[:, C_CB:C_CC].astype(BF16)

    cosg = cosg_ref[...]
    sing = sing_ref[...]
    gmat = gmat_ref[...]

    def head_norm_rope(v, gain, cos_t, sin_t, scale):
        w = v.shape[1]
        sq_hi, sq_lo = _split_bf16(v * v)
        ms = _dot(sq_hi, gmat[:w, :w]) + _dot(sq_lo, gmat[:w, :w])
        y = v * lax.rsqrt(ms + NORM_EPS) * gain
        half = GQA_HEAD_DIM // 2
        ln = lax.broadcasted_iota(jnp.int32, v.shape, 1) % GQA_HEAD_DIM
        partner = jnp.where(ln < half, pltpu.roll(y, w - half, axis=1), pltpu.roll(y, half, axis=1))
        return ((y * cos_t + partner * sin_t) * scale).astype(BF16)

    cos2 = jnp.concatenate([cosg, cosg], axis=1)
    sin2 = jnp.concatenate([sing, sing], axis=1)
    qg_ref[...] = head_norm_rope(z[:, C_GQ:C_GK], gqgain_ref[...], cos2, sin2,
                                 GQA_HEAD_DIM ** -0.5 * LOG2E)
    kg_ref[...] = head_norm_rope(z[:, C_GK:C_GV], gkgain_ref[...], cosg, sing, 1.0)
    lane2 = lax.broadcasted_iota(jnp.int32, (tm, LANES), 1)
    vg_ref[...] = jnp.concatenate([z[:, C_GV:C_END], jnp.where(lane2 == 0, 1.0, 0.0)], axis=1).astype(BF16)


def _mod_spec(mods, layer, dims, tile_map=lambda i: i):
    n_batch, n_tok, n_ctx = dims

    def index_map(i, *_):
        start = tile_map(i) * TOKEN_TILE
        return (layer, jnp.where(start % n_tok < n_ctx, n_batch, start // n_tok), 0, 0)
    return pl.BlockSpec((None, None) + mods.shape[2:], index_map)


def _in_call(hs_parts, mods, tabs, weights, layer, dims):
    n_batch, n_tok, n_ctx = dims
    t = n_batch * n_tok
    tm = TOKEN_TILE
    tiles_per_seq = n_tok // tm
    row = lambda w: pl.BlockSpec((tm, w), lambda i: (i, 0))
    tab = pl.BlockSpec((tm, LANES), lambda i: (i % tiles_per_seq, 0))
    hw = MLA_HEADS * LANES
    outs = [(BRANCH_W, "ya"), (hw, "qm"), (hw, "km"), (hw, "vm"), (BRANCH_W, "p"), (BRANCH_W, "cb"),
            (BRANCH_W, "qg"), (LANES, "kg"), (2 * LANES, "vg")]
    names = ("w_in", "sgu_gain", "w_sgu", "b_sgu", "q_gain", "wq", "kv_gain", "wkv", "gq_gain", "gk_gain")
    params = [weights[k] for k in names]
    return pl.pallas_call(
        _group_hs(functools.partial(_in_kernel, n_tok=n_tok, n_ctx=n_ctx), len(hs_parts)),
        grid=(t // tm,),
        in_specs=_hs_specs(hs_parts, dims) + [_mod_spec(mods, layer, dims), tab, tab, tab, tab]
                 + [_layer_spec(a, layer) for a in params] + [_const_spec((BRANCH_W, BRANCH_W))],
        out_specs=[row(w) for w, _ in outs],
        out_shape=[jax.ShapeDtypeStruct((t, w), BF16) for w, _ in outs],
        compiler_params=_cparams(("arbitrary",)),
        name="in_proj",
    )(*hs_parts, mods, tabs["cosm"], tabs["sinm"], tabs["cosg"], tabs["sing"], *params, tabs["gmat"])


def _attend(q, k_ref, v_ref, nk, k_cols, v_cols):
    s = _dot_nt(q, k_ref[0:nk, k_cols])
    m = jnp.max(s, axis=-1, keepdims=True)
    r = _dot(jnp.exp2(s - m).astype(BF16), v_ref[0:nk, v_cols])
    return r[:, :LANES] * (1.0 / r[:, LANES:LANES + 1])


def _attn_kernel(qm_ref, km_ref, vm_ref, qg_ref, kg_ref, vg_ref, ob_ref, od_ref, *, ctx_tiles, n_ctx):
    def run_sample(s, nk):
        qm, km, vm, qg, kg, vg = (r.at[s] for r in (qm_ref, km_ref, vm_ref, qg_ref, kg_ref, vg_ref))
        lane = lax.broadcasted_iota(jnp.int32, (qm.shape[0], LANES), 1)
        outs = []
        for pair in range(MLA_HEADS // 2):
            v_cols = slice(pair * 2 * LANES, (pair + 1) * 2 * LANES)
            res = []
            for hh in range(2):
                cols = slice((pair * 2 + hh) * LANES, (pair * 2 + hh + 1) * LANES)
                res.append(_attend(qm[:, cols], km, vm, nk, cols, v_cols))
            outs.append(jnp.where(lane < MLA_V, res[0], res[1]))
        ob_ref[s] = jnp.concatenate(outs, axis=1).astype(BF16)

        every = slice(None)
        outs = []
        for grp in range(GQA_HEADS // 2):
            q = qg[:, grp * LANES:(grp + 1) * LANES]
            zero = jnp.zeros_like(q)
            r_lo = _attend(jnp.where(lane < GQA_HEAD_DIM, q, zero), kg, vg, nk, every, every)
            r_hi = _attend(jnp.where(lane >= GQA_HEAD_DIM, q, zero), kg, vg, nk, every, every)
            outs.append(jnp.where(lane < GQA_HEAD_DIM, r_lo, r_hi))
        od_ref[s] = jnp.concatenate(outs, axis=1).astype(BF16)

    def run(nk):
        for s in range(qm_ref.shape[0]):
            run_sample(s, nk)

    j = pl.program_id(1)
    pl.when(j < ctx_tiles)(lambda: run(n_ctx))
    pl.when(j >= ctx_tiles)(lambda: run(km_ref.shape[1]))


ATTN_SAMPLES = 2


def _attn_call(qm, km, vm, qg, kg, vg, dims):
    n_batch, n_tok, n_ctx = dims
    tq = TOKEN_TILE
    ns = ATTN_SAMPLES if n_batch % ATTN_SAMPLES == 0 else 1
    per_sample = lambda a: a.reshape(n_batch, n_tok, a.shape[1])
    qspec = lambda w: pl.BlockSpec((ns, tq, w), lambda b, j: (b, j, 0))
    kspec = lambda w: pl.BlockSpec((ns, n_tok, w), lambda b, j: (b, 0, 0))
    ob, od = pl.pallas_call(
        functools.partial(_attn_kernel, ctx_tiles=n_ctx // tq, n_ctx=n_ctx),
        grid=(n_batch // ns, n_tok // tq),
        in_specs=[qspec(qm.shape[1]), kspec(km.shape[1]), kspec(vm.shape[1]),
                  qspec(qg.shape[1]), kspec(kg.shape[1]), kspec(vg.shape[1])],
        out_specs=[qspec(BRANCH_W), qspec(BRANCH_W)],
        out_shape=[jax.ShapeDtypeStruct((n_batch, n_tok, BRANCH_W), BF16)] * 2,
        compiler_params=_cparams(("arbitrary", "arbitrary")),
        name="attn",
    )(*(per_sample(a) for a in (qm, km, vm, qg, kg, vg)))
    return ob.reshape(-1, BRANCH_W), od.reshape(-1, BRANCH_W)


def _merge_kernel(hs_refs, mod_ref, ya_ref, yb_ref, yd_ref, p_ref, pprev_ref, pnext_ref, cb_ref,
                  wconv_ref, wgate_ref, bgate_ref, wbranch_ref, wout_ref, wr_hi_ref, wr_lo_ref, br_ref,
                  hs1_ref, hn2_ref, logit_ref, *, n_tok, n_ctx, tile_map):
    tm = hs_refs[0].shape[0]
    mod = mod_ref[...]
    start = (tile_map(pl.program_id(0)) * tm) % n_tok
    hs = _load_hs(hs_refs, start, n_ctx)
    hn = (_rms(hs) * (1.0 + mod[1:2, :]) + mod[0:1, :]).astype(BF16)

    prev_ok = jnp.logical_and(start != 0, start != n_ctx)
    next_ok = jnp.logical_and(start + tm != n_ctx, start + tm != n_tok)
    p = p_ref[...].astype(F32)
    halo_prev = jnp.where(prev_ok, pprev_ref[...].astype(F32)[15:16, :], 0.0)
    halo_next = jnp.where(next_ok, pnext_ref[...].astype(F32)[0:1, :], 0.0)
    rowi = lax.broadcasted_iota(jnp.int32, p.shape, 0)
    prev = jnp.where(rowi == 0, halo_prev, pltpu.roll(p, 1, axis=0))
    nxt = jnp.where(rowi == tm - 1, halo_next, pltpu.roll(p, tm - 1, axis=0))
    wc = wconv_ref[...]
    yc = cb_ref[...].astype(F32) * (wc[0:1, :] * prev + wc[1:2, :] * p + wc[2:3, :] * nxt)

    branches = (ya_ref[...], yb_ref[...], yc.astype(BF16), yd_ref[...])
    merged = None
    for i, y in enumerate(branches):
        half_gate = _dot(hn, wgate_ref[i]) + bgate_ref[i:i + 1, :]
        term = (jnp.tanh(half_gate) + 1.0) * _dot(y, wbranch_ref[i])
        merged = term if merged is None else merged + term
    hs1 = hs + _dot(merged.astype(BF16), wout_ref[...]) * mod[2:3, :]
    hs1_ref[...] = hs1

    hn2 = _rms(hs1) * (1.0 + mod[4:5, :]) + mod[3:4, :]
    _rows_to_tiles(hn2_ref, hn2)

    x_hi, x_lo = _split_bf16(hn2)
    logit_ref[...] = (_dot(x_hi, wr_hi_ref[...]) + _dot(x_lo, wr_hi_ref[...]) + _dot(x_hi, wr_lo_ref[...])
                      + br_ref[...])


ROUTE_SUB = 256


def _route_kernel(logit_ref, route_ref, wts_ref, cnt_ref, carry_ref):
    tm = logit_ref.shape[0]
    lt = logit_ref[...].T
    sub = lax.broadcasted_iota(jnp.int32, (EXPERTS_PER_GROUP, tm), 0)
    neg = -jnp.inf
    top = lambda v: jnp.max(v, axis=0, keepdims=True)
    first = lambda hit: jnp.min(jnp.where(hit, sub, EXPERTS_PER_GROUP), axis=0, keepdims=True)
    gl = jnp.where(sub < N_GROUPS, lt[0:EXPERTS_PER_GROUP, :], neg)
    ge = jnp.exp(gl - top(gl))
    pg = ge / jnp.sum(ge, axis=0, keepdims=True)
    pg_top = top(pg)
    g_sel = first(pg == pg_top)
    le = lt[EXPERTS_PER_GROUP:2 * EXPERTS_PER_GROUP, :]
    for g in range(1, N_GROUPS):
        le = jnp.where(g_sel == g, lt[(g + 1) * EXPERTS_PER_GROUP:(g + 2) * EXPERTS_PER_GROUP, :], le)
    ee = jnp.exp(le - top(le))
    pe = ee / jnp.sum(ee, axis=0, keepdims=True)
    p1 = top(pe)
    i1 = first(pe == p1)
    pe2 = jnp.where(sub != i1, pe, neg)
    p2 = top(pe2)
    i2 = first(jnp.logical_and(sub != i1, pe2 == p2))
    denom = p1 + p2
    e1 = g_sel * EXPERTS_PER_GROUP + i1
    e2 = g_sel * EXPERTS_PER_GROUP + i2
    wts_ref[...] = jnp.where(sub == 0, pg_top * p1 / denom, jnp.where(sub == 1, pg_top * p2 / denom, 0.0))

    @pl.when(pl.program_id(0) == 0)
    def _():
        carry_ref[...] = jnp.zeros_like(carry_ref)

    ts = ROUTE_SUB
    erow = lax.broadcasted_iota(jnp.int32, (N_EXPERTS, ts), 0)
    ri = lax.broadcasted_iota(jnp.int32, (ts, ts + LANES), 0)
    ci = lax.broadcasted_iota(jnp.int32, (ts, ts + LANES), 1)
    before = jnp.where(jnp.logical_or(ri < ci, ci >= ts), 1.0, 0.0).astype(BF16)
    carry = carry_ref[...]
    r1s, r2s = [], []
    for s in range(tm // ts):
        oh1 = jnp.where(erow == e1[:, s * ts:(s + 1) * ts], 1.0, 0.0)
        oh2 = jnp.where(erow == e2[:, s * ts:(s + 1) * ts], 1.0, 0.0)
        pref = _dot(jnp.concatenate([oh1, oh2], axis=0).astype(BF16), before)
        c1 = pref[:N_EXPERTS, ts:]
        c2 = pref[N_EXPERTS:, ts:]
        r1s.append(jnp.sum(oh1 * (carry[:, 0:1] + pref[:N_EXPERTS, :ts]), axis=0, keepdims=True))
        r2s.append(jnp.sum(oh2 * ((carry + c1)[:, 0:1] + pref[N_EXPERTS:, :ts]), axis=0, keepdims=True))
        carry = carry + c1 + c2
    r1 = jnp.concatenate(r1s, axis=1)
    r2 = jnp.concatenate(r2s, axis=1)
    carry_ref[...] = carry
    cnt_ref[...] = carry.astype(jnp.int32)
    route_ref[...] = jnp.where(sub == 0, e1, jnp.where(sub == 1, e2,
                               jnp.where(sub == 2, r1.astype(jnp.int32),
                                         jnp.where(sub == 3, r2.astype(jnp.int32), 0))))


def _route_call(logits):
    t = logits.shape[0]
    tr = 1024 if t % 1024 == 0 else ROUTE_SUB
    return pl.pallas_call(
        _route_kernel,
        grid=(t // tr,),
        in_specs=[pl.BlockSpec((tr, LANES), lambda i: (i, 0))],
        out_specs=[pl.BlockSpec((8, tr), lambda i: (0, i)), pl.BlockSpec((8, tr), lambda i: (0, i)),
                   pl.BlockSpec((N_EXPERTS, LANES), lambda i: (0, 0))],
        out_shape=[jax.ShapeDtypeStruct((8, t), jnp.int32), jax.ShapeDtypeStruct((8, t), F32),
                   jax.ShapeDtypeStruct((N_EXPERTS, LANES), jnp.int32)],
        scratch_shapes=[pltpu.VMEM((N_EXPERTS, LANES), F32)],
        compiler_params=_cparams(("arbitrary",)),
        name="route",
    )(logits)


def _token_tiles(dims, latent_only):
    n_batch, n_tok, n_ctx = dims
    tiles, ctx_tiles = n_tok // TOKEN_TILE, n_ctx // TOKEN_TILE
    if not latent_only:
        return n_batch * tiles, lambda i: i
    lat = tiles - ctx_tiles
    return n_batch * lat, lambda i: (i // lat) * tiles + ctx_tiles + i % lat


def _merge_call(hs_parts, mods, ya, yb, yd, p, cb, weights, layer, dims, latent_only):
    n_batch, n_tok, n_ctx = dims
    t, d = n_batch * n_tok, hs_parts[0].shape[1]
    tm = TOKEN_TILE
    sub = 16
    n_sub = t // sub
    n_steps, tile_map = _token_tiles(dims, latent_only)
    t_out = n_steps * tm
    row_in = lambda w: pl.BlockSpec((tm, w), lambda i: (tile_map(i), 0))
    row = lambda w: pl.BlockSpec((tm, w), lambda i: (i, 0))
    names = ("w_conv", "w_gate", "b_gate", "w_branch", "w_out", "wr_hi", "wr_lo", "br")
    params = [weights[k] for k in names]
    return pl.pallas_call(
        _group_hs(functools.partial(_merge_kernel, n_tok=n_tok, n_ctx=n_ctx, tile_map=tile_map), len(hs_parts)),
        grid=(n_steps,),
        in_specs=_hs_specs(hs_parts, dims, tile_map) + [
                  _mod_spec(mods, layer, dims, tile_map),
                  row_in(BRANCH_W), row_in(BRANCH_W), row_in(BRANCH_W), row_in(BRANCH_W),
                  pl.BlockSpec((sub, BRANCH_W),
                               lambda i: (jnp.maximum(tile_map(i) * (tm // sub) - 1, 0), 0)),
                  pl.BlockSpec((sub, BRANCH_W),
                               lambda i: (jnp.minimum((tile_map(i) + 1) * (tm // sub), n_sub - 1), 0)),
                  row_in(BRANCH_W)] + [_layer_spec(a, layer) for a in params],
        out_specs=[row(d), pl.BlockSpec((tm * ROW_SUB, LANES), lambda i: (i, 0)), row(LANES)],
        out_shape=[jax.ShapeDtypeStruct((t_out, d), F32), jax.ShapeDtypeStruct((t_out * ROW_SUB, LANES), F32),
                   jax.ShapeDtypeStruct((t_out, LANES), F32)],
        compiler_params=_cparams(("arbitrary",)),
        name="merge_route",
    )(*hs_parts, mods, ya, yb, yd, p, p, p, cb, *params)


PAD_BITS = tuple(1 << b for b in reversed(range(EXPERT_BLK.bit_length() - 1)))


def _rows_to_tiles(ref, v):
    for ck in range(ROW_SUB):
        ref[pl.ds(ck, v.shape[0], stride=ROW_SUB), :] = v[:, ck * LANES:(ck + 1) * LANES]


def _tiles_to_rows(ref, first, n):
    return jnp.concatenate([ref[pl.ds(first * ROW_SUB + ck, n, stride=ROW_SUB), :] for ck in range(ROW_SUB)],
                           axis=1)


def _tile_of(ref, row, n=1):
    return ref.at[pl.ds(pl.multiple_of(row * ROW_SUB, ROW_SUB), n * ROW_SUB)]


def _tiles_wait(like, sem):
    pltpu.make_async_copy(like, like, sem).wait()


def _dispatch_kernel(pos_ref, pad_start_ref, pad_len_ref, tail_ref, x_ref, xs_hbm, stage, zeros, sems, pad_sem):
    i = pl.program_id(0)
    n = pl.num_programs(0)
    tm = x_ref.shape[0] // ROW_SUB
    slot = i % 2

    def pad_copies(e, fn):
        start = pad_start_ref[e]
        ln = pad_len_ref[e]
        for b in PAD_BITS:
            @pl.when((ln & b) != 0)
            def _():
                off = ln & ~(2 * b - 1)
                fn(pltpu.make_async_copy(_tile_of(zeros, 0, b), _tile_of(xs_hbm, start + off, b), pad_sem))

    def for_each_expert(fn):
        def body(e, carry):
            pad_copies(e, fn)
            return carry
        lax.fori_loop(0, N_EXPERTS, body, 0)

        def tail(j, carry):
            fn(pltpu.make_async_copy(zeros, _tile_of(xs_hbm, j * PAD_BITS[0], PAD_BITS[0]), pad_sem))
            return carry
        lax.fori_loop(tail_ref[0] // PAD_BITS[0], xs_hbm.shape[0] // (ROW_SUB * PAD_BITS[0]), tail, 0)

    @pl.when(i == 0)
    def _():
        zeros[...] = jnp.zeros_like(zeros)
        for_each_expert(lambda cp: cp.start())

    @pl.when(i >= 2)
    def _():
        _tiles_wait(stage, sems.at[slot])

    stage[slot] = x_ref[...]

    def push(r, carry):
        for k in range(2):
            dst = pos_ref[i * (2 * tm) + k * tm + r]
            pltpu.make_async_copy(_tile_of(stage.at[slot], r), _tile_of(xs_hbm, dst),
                                  sems.at[slot]).start(priority=k)
        return carry
    lax.fori_loop(0, tm, push, 0, unroll=8)

    @pl.when(i == n - 1)
    def _():
        _tiles_wait(stage, sems.at[slot])

        @pl.when(n >= 2)
        def _():
            _tiles_wait(stage, sems.at[1 - slot])
        for_each_expert(lambda cp: cp.wait())


def _dispatch_call(pos, pad_start, pad_len, tail_start, x_tiles, cap):
    tm = TOKEN_TILE
    grid_spec = pltpu.PrefetchScalarGridSpec(
        num_scalar_prefetch=4,
        grid=(x_tiles.shape[0] // (tm * ROW_SUB),),
        in_specs=[pl.BlockSpec((tm * ROW_SUB, LANES), lambda i, *_: (i, 0))],
        out_specs=pl.BlockSpec(memory_space=pl.ANY),
        scratch_shapes=[pltpu.VMEM((2, tm * ROW_SUB, LANES), F32), pltpu.VMEM((PAD_BITS[0] * ROW_SUB, LANES), F32),
                        pltpu.SemaphoreType.DMA((2,)), pltpu.SemaphoreType.DMA(())],
    )
    return pl.pallas_call(
        _dispatch_kernel,
        grid_spec=grid_spec,
        out_shape=jax.ShapeDtypeStruct((cap * ROW_SUB, LANES), F32),
        compiler_params=_cparams(("arbitrary",)),
        name="dispatch",
    )(pos, pad_start, pad_len, tail_start, x_tiles)


EXPERT_AHEAD = 3


def _expert_kernel(blk_e_ref, n_used_ref, x_hbm, wgu_ref, wdn_ref, y_ref, wgu_bf, wdn_bf, xbuf, sems):
    i = pl.program_id(0)
    n_used = n_used_ref[0]
    n_buf = xbuf.shape[0]
    rows = xbuf.shape[1]

    def fetch(j):
        first = j * rows if isinstance(j, int) else pl.multiple_of(j * rows, rows)
        return pltpu.make_async_copy(x_hbm.at[pl.ds(first, rows)], xbuf.at[j % n_buf], sems.at[j % n_buf])

    @pl.when(i == 0)
    def _():
        for j in range(EXPERT_AHEAD):
            pl.when(j < n_used)(lambda j=j: fetch(j).start())

    @pl.when(i + EXPERT_AHEAD < n_used)
    def _():
        fetch(i + EXPERT_AHEAD).start()

    @pl.when(jnp.logical_or(i == 0, blk_e_ref[i] != blk_e_ref[jnp.maximum(i - 1, 0)]))
    def _():
        wgu_bf[...] = wgu_ref[...].astype(BF16)
        wdn_bf[...] = wdn_ref[...].astype(BF16)

    @pl.when(i < n_used)
    def _():
        fetch(i).wait()
        xe = _tiles_to_rows(xbuf.at[i % n_buf], 0, EXPERT_BLK).astype(BF16)
        gu = _dot(xe, wgu_bf[...])
        g = gu[:, :D_EXPERT]
        act = (g * _sigmoid(g) * gu[:, D_EXPERT:]).astype(BF16)
        _rows_to_tiles(y_ref, _dot(act, wdn_bf[...]))

    @pl.when(i >= n_used)
    def _():
        y_ref[...] = jnp.zeros_like(y_ref)


def _expert_call(blk_e, n_used, xs, w_gu, w_dn, layer):
    d = w_gu.shape[2]
    blk = EXPERT_BLK
    grid_spec = pltpu.PrefetchScalarGridSpec(
        num_scalar_prefetch=2,
        grid=(xs.shape[0] // (blk * ROW_SUB),),
        in_specs=[pl.BlockSpec(memory_space=pl.ANY),
                  pl.BlockSpec((None, None, d, 2 * D_EXPERT), lambda i, be, nu: (layer, be[i], 0, 0)),
                  pl.BlockSpec((None, None, D_EXPERT, d), lambda i, be, nu: (layer, be[i], 0, 0))],
        out_specs=pl.BlockSpec((blk * ROW_SUB, LANES), lambda i, be, nu: (i, 0)),
        scratch_shapes=[pltpu.VMEM((d, 2 * D_EXPERT), BF16), pltpu.VMEM((D_EXPERT, d), BF16),
                        pltpu.VMEM((EXPERT_AHEAD + 1, blk * ROW_SUB, LANES), F32),
                        pltpu.SemaphoreType.DMA((EXPERT_AHEAD + 1,))],
    )
    return pl.pallas_call(
        _expert_kernel,
        grid_spec=grid_spec,
        out_shape=jax.ShapeDtypeStruct(xs.shape, F32),
        compiler_params=_cparams(("arbitrary",)),
        name="experts",
    )(blk_e, n_used, xs, w_gu, w_dn)


def _combine_kernel(pos_ref, y_hbm, hs1_ref, wts_ref, mod_ref, gain_ref, o_ref, gbuf, sems, *,
                    final):
    i = pl.program_id(0)
    n = pl.num_programs(0)
    tm = hs1_ref.shape[0]
    slot = i % 2

    def start(step, s):
        base = step * (2 * tm)

        def body(r, carry):
            for k in range(2):
                src = pos_ref[base + k * tm + r]
                pltpu.make_async_copy(_tile_of(y_hbm, src), _tile_of(gbuf.at[s], k * tm + r),
                                      sems.at[s]).start(priority=k)
            return carry
        lax.fori_loop(0, tm, body, 0, unroll=8)

    @pl.when(i == 0)
    def _():
        start(0, 0)

    @pl.when(i + 1 < n)
    def _():
        start(i + 1, 1 - slot)

    _tiles_wait(gbuf.at[slot], sems.at[slot])
    wts = wts_ref[...]
    g = gbuf.at[slot]
    f = wts[:, 0:1] * _tiles_to_rows(g, 0, tm) + wts[:, 1:2] * _tiles_to_rows(g, tm, tm)
    out = hs1_ref[...] + f * mod_ref[...][5:6, :]
    if final:
        out = _rms(out) * gain_ref[...]
    o_ref[...] = out


def _combine_call(pos, y, hs1, wts, mods, layer, gain, dims, final):
    t, d = hs1.shape
    tm = TOKEN_TILE
    n_steps, tile_map = _token_tiles(dims, final)
    assert n_steps * tm == t
    grid_spec = pltpu.PrefetchScalarGridSpec(
        num_scalar_prefetch=1,
        grid=(n_steps,),
        in_specs=[pl.BlockSpec(memory_space=pl.ANY),
                  pl.BlockSpec((tm, d), lambda i, pos: (i, 0)),
                  pl.BlockSpec((tm, wts.shape[1]), lambda i, pos: (i, 0)),
                  _mod_spec(mods, layer, dims, tile_map),
                  pl.BlockSpec((1, d), lambda i, pos: (0, 0))],
        out_specs=pl.BlockSpec((tm, d), lambda i, pos: (i, 0)),
        scratch_shapes=[pltpu.VMEM((2, 2 * tm * ROW_SUB, LANES), F32), pltpu.SemaphoreType.DMA((2,))],
    )
    return pl.pallas_call(
        functools.partial(_combine_kernel, final=final),
        grid_spec=grid_spec,
        out_shape=jax.ShapeDtypeStruct((n_steps * tm, d), F32),
        compiler_params=_cparams(("arbitrary",)),
        name="combine_final" if final else "combine",
    )(pos, y, hs1, wts, mods, gain)


def _rope_angles(n_ctx, rows, rot_dim):
    n_freq = rot_dim // 4
    freqs = ROPE_THETA ** (-jnp.arange(n_freq, dtype=F32) / n_freq)
    row = jnp.repeat(jnp.arange(rows, dtype=F32), GRID_W)
    col = (jnp.arange(rows * GRID_W) % GRID_W).astype(F32)
    ang = jnp.concatenate([row[:, None] * freqs, col[:, None] * freqs], axis=-1)
    ang = jnp.concatenate([jnp.zeros((n_ctx, rot_dim // 2), F32), ang], axis=0)
    return jnp.cos(ang), jnp.sin(ang)


def _tables(n_ctx, seq):
    n = n_ctx + seq
    cm, sm = _rope_angles(n_ctx, seq // GRID_W, MLA_ROPE)
    cg, sg = _rope_angles(n_ctx, seq // GRID_W, GQA_HEAD_DIM)
    one, zero = jnp.ones((n, MLA_NOPE), F32), jnp.zeros((n, MLA_NOPE), F32)
    pad = jnp.zeros((n, LANES - MLA_NOPE - MLA_ROPE), F32)
    gi = jnp.arange(BRANCH_W) // GQA_HEAD_DIM
    return {
        "cosm": jnp.concatenate([one, cm, cm, pad], axis=1),
        "sinm": jnp.concatenate([zero, -sm, sm, pad], axis=1),
        "cosg": jnp.concatenate([cg, cg, cg, cg], axis=1),
        "sing": jnp.concatenate([-sg, sg, -sg, sg], axis=1),
        "gmat": jnp.where(gi[:, None] == gi[None, :], 1.0 / GQA_HEAD_DIM, 0.0).astype(BF16),
    }


def _prep_weights(w_in, sgu_gain, w_sgu, b_sgu, mla_q_gain, w_uq, mla_kv_gain, w_ukv, w_conv,
                  gqa_q_gain, gqa_k_gain, w_gate, b_gate, w_branch, w_out, w_group_router,
                  b_group_router, w_expert_router, b_expert_router):
    nl, d, _ = w_in.shape
    zc = lambda n: jnp.zeros((nl, d, n), F32)
    half = MLA_ROPE // 2
    kpe = w_in[..., 896:928]
    gq = w_in[..., 1696:1952].reshape(nl, d, GQA_HEADS, GQA_HEAD_DIM)[:, :, (0, 2, 1, 3), :].reshape(nl, d, BRANCH_W)
    w_in_ext = jnp.concatenate([
        w_in[..., 0:896],
        zc(MLA_NOPE), kpe, zc(LANES - MLA_NOPE - MLA_ROPE),
        zc(MLA_NOPE), kpe[..., half:], kpe[..., :half], zc(LANES - MLA_NOPE - MLA_ROPE),
        w_in[..., 928:1696], gq, w_in[..., 1952:2208]], axis=-1).astype(BF16)

    dq = MLA_NOPE + MLA_ROPE
    uq = w_uq.reshape(nl, -1, MLA_HEADS, dq)
    zq = jnp.zeros(uq.shape[:3] + (LANES - dq,), F32)
    wq_a = jnp.concatenate([uq, zq], axis=-1)
    wq_b = jnp.concatenate([jnp.zeros_like(uq[..., :MLA_NOPE]), uq[..., MLA_NOPE + half:],
                            uq[..., MLA_NOPE:MLA_NOPE + half], zq], axis=-1)
    wq = jnp.concatenate([wq_a.reshape(nl, -1, MLA_HEADS * LANES),
                          wq_b.reshape(nl, -1, MLA_HEADS * LANES)], axis=-1).astype(BF16)

    ukv = w_ukv.reshape(nl, -1, MLA_HEADS, MLA_NOPE + MLA_V)
    k_lay = jnp.concatenate([ukv[..., :MLA_NOPE], jnp.zeros_like(ukv[..., :LANES - MLA_NOPE])], axis=-1)
    v_h = ukv[..., MLA_NOPE:]
    zv = jnp.zeros(v_h.shape[:2] + (LANES,), F32)
    v_lay = jnp.concatenate([v_h[:, :, 0], v_h[:, :, 1], zv, v_h[:, :, 2], v_h[:, :, 3], zv], axis=-1)
    wkv = jnp.concatenate([k_lay.reshape(nl, -1, MLA_HEADS * LANES), v_lay], axis=-1).astype(BF16)

    wb = w_branch
    wb_d = wb[:, 3].reshape(nl, GQA_HEADS, GQA_HEAD_DIM, d)[:, (0, 2, 1, 3)].reshape(nl, BRANCH_W, d)
    w_branch_p = (0.5 * jnp.concatenate([wb[:, :3], wb_d[:, None]], axis=1)).astype(BF16)

    gap = EXPERTS_PER_GROUP - N_GROUPS
    rest = LANES - EXPERTS_PER_GROUP - N_EXPERTS
    wr = jnp.concatenate([w_group_router, jnp.zeros((nl, d, gap), F32), w_expert_router,
                          jnp.zeros((nl, d, rest), F32)], axis=-1)
    wr_hi = wr.astype(BF16)
    wr_lo = (wr - wr_hi.astype(F32)).astype(BF16)
    br = jnp.concatenate([b_group_router, jnp.zeros((nl, gap), F32), b_expert_router,
                          jnp.zeros((nl, rest), F32)], axis=-1)[:, None, :]

    tile = lambda g, reps: jnp.concatenate([g] * reps, axis=-1)[:, None, :]
    return {
        "w_in": w_in_ext,
        "sgu_gain": sgu_gain[:, None, :],
        "w_sgu": w_sgu.astype(BF16),
        "b_sgu": jnp.repeat(b_sgu.transpose(0, 2, 1), BRANCH_W // SGU_GROUPS, axis=-1),
        "q_gain": mla_q_gain[:, None, :], "wq": wq,
        "kv_gain": mla_kv_gain[:, None, :], "wkv": wkv,
        "gq_gain": tile(gqa_q_gain, 4), "gk_gain": tile(gqa_k_gain, 2),
        "w_conv": w_conv,
        "w_gate": (0.5 * w_gate).astype(BF16), "b_gate": 0.5 * b_gate,
        "w_branch": w_branch_p, "w_out": w_out.astype(BF16),
        "wr_hi": wr_hi, "wr_lo": wr_lo, "br": br,
    }


def _dispatch_plan(route, cnt, n_tiles, cap):
    blk = EXPERT_BLK
    tm = route.shape[1] // n_tiles
    counts = cnt[:, 0]
    padded = (counts + blk - 1) // blk * blk
    pend = jnp.cumsum(padded)
    pstart = pend - padded
    eid, rank = route[0:2], route[2:4]
    onehot = eid[:, :, None] == jnp.arange(N_EXPERTS, dtype=jnp.int32)
    dest = jnp.sum(jnp.where(onehot, pstart, 0), axis=-1) + rank
    pos = dest.reshape(2, n_tiles, tm).transpose(1, 0, 2).reshape(-1)
    n_used = pend[-1] // blk
    blk_first = jnp.minimum(jnp.arange(cap // blk, dtype=jnp.int32), n_used - 1) * blk
    blk_e = jnp.sum((blk_first[:, None] >= pend[None, :]).astype(jnp.int32), axis=1)
    blk_e = jnp.minimum(blk_e, N_EXPERTS - 1)
    return pos, pstart + counts, padded - counts, blk_e, n_used.reshape(1)


def kernel(x, c, ctx, c_ctx, w_ada, b_ada, w_in, sgu_gain, w_sgu, b_sgu, mla_q_gain, w_uq, mla_kv_gain, w_ukv, w_conv, gqa_q_gain, gqa_k_gain, w_gate, b_gate, w_branch, w_out, w_group_router, b_group_router, w_expert_router, b_expert_router, w_expert_gate_up, w_expert_down, final_gain):
    n_batch, seq, d = x.shape
    n_ctx = ctx.shape[1]
    n_tok = n_ctx + seq
    t = n_batch * n_tok
    depth = w_in.shape[0]
    tm = TOKEN_TILE
    assert n_ctx % tm == 0 and seq % tm == 0 and seq % GRID_W == 0 and tm % SGU_CHUNK == 0
    dims = (n_batch, n_tok, n_ctx)

    hs = [ctx.reshape(n_batch * n_ctx, d), x.reshape(n_batch * seq, d)]
    tabs = _tables(n_ctx, seq)
    rows = -(-(n_batch + 1) // 8) * 8
    cvec = jnp.concatenate([c, c_ctx[None, :], jnp.zeros((rows - n_batch - 1, d), F32)], axis=0)
    mods = _ada_call(cvec, w_ada, b_ada).reshape(depth, rows, 6, d)
    weights = _prep_weights(w_in, sgu_gain, w_sgu, b_sgu, mla_q_gain, w_uq, mla_kv_gain, w_ukv, w_conv,
                            gqa_q_gain, gqa_k_gain, w_gate, b_gate, w_branch, w_out, w_group_router,
                            b_group_router, w_expert_router, b_expert_router)
    gain = final_gain[None, :]

    for l in range(depth):
        last = l == depth - 1
        ya, qm, km, vm, p, cb, qg, kg, vg = _in_call(hs, mods, tabs, weights, l, dims)
        yb, yd = _attn_call(qm, km, vm, qg, kg, vg, dims)
        hs1, hn2, logits = _merge_call(hs, mods, ya, yb, yd, p, cb, weights, l, dims, latent_only=last)
        route, wts, cnt = _route_call(logits)
        t_moe = hs1.shape[0]
        cap = -(-2 * t_moe // EXPERT_BLK) * EXPERT_BLK + N_EXPERTS * EXPERT_BLK
        pos, pad_start, pad_len, blk_e, n_used = _dispatch_plan(route, cnt, t_moe // tm, cap)
        xs = _dispatch_call(pos, pad_start, pad_len, n_used * EXPERT_BLK, hn2, cap)
        y = _expert_call(blk_e, n_used, xs, w_expert_gate_up, w_expert_down, l)
        hs = [_combine_call(pos, y, hs1, wts.T, mods, l, gain, dims, final=last)]
    return hs[0].reshape(n_batch, seq, d)
```

```python
import functools
import math

import jax
import jax.numpy as jnp
from jax import lax
from jax.experimental import pallas as pl
from jax.experimental.pallas import tpu as pltpu

F32 = jnp.float32
BF16 = jnp.bfloat16

GRID_W = 64
ROPE_THETA = 10000.0
NORM_EPS = 1e-6
BRANCH_W = 256
SGU_CHUNK = 128
SGU_GROUPS = 4
MLA_HEADS = 4
MLA_NOPE = 64
MLA_ROPE = 32
MLA_V = 64
GQA_HEADS = 4
GQA_HEAD_DIM = 64
N_GROUPS = 4
EXPERTS_PER_GROUP = 8
N_EXPERTS = N_GROUPS * EXPERTS_PER_GROUP
D_EXPERT = 256
LANES = 128
ROW_SUB = 8
TOKEN_TILE = 256
EXPERT_BLK = 256
VMEM_LIMIT = 56 * 1024 * 1024
LOG2E = 1.4426950408889634

C_AU, C_AV, C_MQ, C_MKV, C_KPA, C_KPB, C_CB, C_CC, C_CX, C_GQ, C_GK, C_GV, C_END = (
    0, 256, 512, 768, 896, 1024, 1152, 1408, 1664, 1920, 2176, 2304, 2432)


def _cparams(sem):
    return pltpu.CompilerParams(dimension_semantics=sem, vmem_limit_bytes=VMEM_LIMIT)


def _dot(a, b):
    return jnp.dot(a, b, preferred_element_type=F32)


def _dot_nt(a, b):
    return lax.dot_general(a, b, (((1,), (1,)), ((), ())), preferred_element_type=F32)


def _split_bf16(v):
    hi = v.astype(BF16)
    lo = (v - hi.astype(F32)).astype(BF16)
    return hi, lo


def _gelu(v):
    return 0.5 * v * (1.0 + jnp.tanh(math.sqrt(2.0 / math.pi) * (v + 0.044715 * (v * v * v))))


def _sigmoid(v):
    return 0.5 * jnp.tanh(0.5 * v) + 0.5


def _rms(v):
    return v * lax.rsqrt(jnp.mean(v * v, axis=-1, keepdims=True) + NORM_EPS)


def _const_spec(shape):
    nd = len(shape)
    return pl.BlockSpec(shape, lambda *_: (0,) * nd)


def _layer_spec(stacked, layer):
    nd = stacked.ndim - 1
    return pl.BlockSpec((None,) + stacked.shape[1:], lambda *_: (layer,) + (0,) * nd)


def _ada_kernel(c_ref, w_ref, b_ref, o_ref):
    cv = c_ref[...]
    s = cv * _sigmoid(cv)
    s_hi, s_lo = _split_bf16(s)
    w_hi, w_lo = _split_bf16(w_ref[...])
    o_ref[...] = _dot(s_hi, w_hi) + _dot(s_lo, w_hi) + _dot(s_hi, w_lo) + b_ref[...]


def _ada_call(cvec, w_ada, b_ada):
    n_layers, d, d6 = w_ada.shape
    rows = cvec.shape[0]
    tn = 1536
    return pl.pallas_call(
        _ada_kernel,
        grid=(n_layers, d6 // tn),
        in_specs=[pl.BlockSpec((rows, d), lambda l, j: (0, 0)),
                  pl.BlockSpec((None, d, tn), lambda l, j: (l, 0, j)),
                  pl.BlockSpec((None, 1, tn), lambda l, j: (l, 0, j))],
        out_specs=pl.BlockSpec((None, rows, tn), lambda l, j: (l, 0, j)),
        out_shape=jax.ShapeDtypeStruct((n_layers, rows, d6), F32),
        compiler_params=_cparams(("arbitrary", "arbitrary")),
        name="ada_mod",
    )(cvec, w_ada, b_ada.reshape(n_layers, 1, d6))


def _hs_specs(hs_parts, dims, tile_map=lambda i: i):
    n_batch, n_tok, n_ctx = dims
    tm = TOKEN_TILE
    d = hs_parts[0].shape[1]
    if len(hs_parts) == 1:
        return [pl.BlockSpec((tm, d), lambda i, *_: (tile_map(i), 0))]
    tiles, ctx_tiles = n_tok // tm, n_ctx // tm

    def ctx_map(i, *_):
        g = tile_map(i)
        return ((g // tiles) * ctx_tiles + jnp.minimum(g % tiles, ctx_tiles - 1), 0)

    def lat_map(i, *_):
        g = tile_map(i)
        return ((g // tiles) * (tiles - ctx_tiles) + jnp.maximum(g % tiles - ctx_tiles, 0), 0)
    return [pl.BlockSpec((tm, d), ctx_map), pl.BlockSpec((tm, d), lat_map)]


def _load_hs(hs_refs, start, n_ctx):
    if len(hs_refs) == 1:
        return hs_refs[0][...]
    return jnp.where(start < n_ctx, hs_refs[0][...], hs_refs[1][...])


def _group_hs(kernel, n_parts):
    def wrapped(*refs):
        return kernel(refs[:n_parts], *refs[n_parts:])
    return wrapped


def _in_kernel(hs_refs, mod_ref, cosm_ref, sinm_ref, cosg_ref, sing_ref, w_in_ref, sgu_gain_ref,
               w_sgu_ref, b_sgu_ref, qgain_ref, wq_ref, kvgain_ref, wkv_ref, gqgain_ref, gkgain_ref,
               gmat_ref,
               ya_ref, qm_ref, km_ref, vm_ref, p_ref, cb_ref, qg_ref, kg_ref, vg_ref, *, n_tok, n_ctx):
    tm = hs_refs[0].shape[0]
    mod = mod_ref[...]
    hs = _load_hs(hs_refs, (pl.program_id(0) * tm) % n_tok, n_ctx)
    hn = _rms(hs) * (1.0 + mod[1:2, :]) + mod[0:1, :]
    z = _dot(hn.astype(BF16), w_in_ref[...])

    av = _rms(_gelu(z[:, C_AV:C_MQ])) * sgu_gain_ref[...]
    av = av.astype(BF16)
    grp = lax.broadcasted_iota(jnp.int32, (SGU_CHUNK, BRANCH_W), 1) // (BRANCH_W // SGU_GROUPS)
    chunks = []
    for ci in range(tm // SGU_CHUNK):
        vc = av[ci * SGU_CHUNK:(ci + 1) * SGU_CHUNK, :]
        m = b_sgu_ref[...]
        for g in range(SGU_GROUPS):
            m = m + jnp.where(grp == g, _dot(w_sgu_ref[g], vc), 0.0)
        chunks.append(m)
    mixed = jnp.concatenate(chunks, axis=0)
    ya_ref[...] = (_gelu(z[:, C_AU:C_AV]) * mixed).astype(BF16)

    cosm = cosm_ref[...]
    sinm = sinm_ref[...]
    cos4 = jnp.concatenate([cosm] * MLA_HEADS, axis=1)
    sin4 = jnp.concatenate([sinm] * MLA_HEADS, axis=1)
    nq = (_rms(z[:, C_MQ:C_MKV]) * qgain_ref[...]).astype(BF16)
    qab = _dot(nq, wq_ref[...])
    hw = MLA_HEADS * LANES
    q_scale = (MLA_NOPE + MLA_ROPE) ** -0.5 * LOG2E
    qm_ref[...] = ((qab[:, :hw] * cos4 + qab[:, hw:] * sin4) * q_scale).astype(BF16)
    nkv = (_rms(z[:, C_MKV:C_KPA]) * kvgain_ref[...]).astype(BF16)
    kv = _dot(nkv, wkv_ref[...])
    kpe = z[:, C_KPA:C_KPB] * cosm + z[:, C_KPB:C_CB] * sinm
    km_ref[...] = (kv[:, :hw] + jnp.concatenate([kpe] * MLA_HEADS, axis=1)).astype(BF16)
    lane = lax.broadcasted_iota(jnp.int32, (tm, hw), 1)
    ones_col = jnp.where(lane % (2 * LANES) == LANES, 1.0, 0.0)
    vm_ref[...] = (kv[:, hw:] + ones_col).astype(BF16)

    p_ref[...] = (z[:, C_CC:C_CX] * z[:, C_CX:C_GQ]).astype(BF16)
    cb_ref[...] = z[:, C_CB:C_CC].astype(BF16)

    cosg = cosg_ref[...]
    sing = sing_ref[...]
    gmat = gmat_ref[...]

    def head_norm_rope(v, gain, cos_t, sin_t, scale):
        w = v.shape[1]
        sq_hi, sq_lo = _split_bf16(v * v)
        ms = _dot(sq_hi, gmat[:w, :w]) + _dot(sq_lo, gmat[:w, :w])
        y = v * lax.rsqrt(ms + NORM_EPS) * gain
        half = GQA_HEAD_DIM // 2
        ln = lax.broadcasted_iota(jnp.int32, v.shape, 1) % GQA_HEAD_DIM
        partner = jnp.where(ln < half, pltpu.roll(y, w - half, axis=1), pltpu.roll(y, half, axis=1))
        return ((y * cos_t + partner * sin_t) * scale).astype(BF16)

    cos2 = jnp.concatenate([cosg, cosg], axis=1)
    sin2 = jnp.concatenate([sing, sing], axis=1)
    qg_ref[...] = head_norm_rope(z[:, C_GQ:C_GK], gqgain_ref[...], cos2, sin2,
                                 GQA_HEAD_DIM ** -0.5 * LOG2E)
    kg_ref[...] = head_norm_rope(z[:, C_GK:C_GV], gkgain_ref[...], cosg, sing, 1.0)
    lane2 = lax.broadcasted_iota(jnp.int32, (tm, LANES), 1)
    vg_ref[...] = jnp.concatenate([z[:, C_GV:C_END], jnp.where(lane2 == 0, 1.0, 0.0)], axis=1).astype(BF16)


def _mod_spec(mods, layer, dims, tile_map=lambda i: i):
    n_batch, n_tok, n_ctx = dims

    def index_map(i, *_):
        start = tile_map(i) * TOKEN_TILE
        return (layer, jnp.where(start % n_tok < n_ctx, n_batch, start // n_tok), 0, 0)
    return pl.BlockSpec((None, None) + mods.shape[2:], index_map)


def _in_call(hs_parts, mods, tabs, weights, layer, dims):
    n_batch, n_tok, n_ctx = dims
    t = n_batch * n_tok
    tm = TOKEN_TILE
    tiles_per_seq = n_tok // tm
    row = lambda w: pl.BlockSpec((tm, w), lambda i: (i, 0))
    tab = pl.BlockSpec((tm, LANES), lambda i: (i % tiles_per_seq, 0))
    hw = MLA_HEADS * LANES
    outs = [(BRANCH_W, "ya"), (hw, "qm"), (hw, "km"), (hw, "vm"), (BRANCH_W, "p"), (BRANCH_W, "cb"),
            (BRANCH_W, "qg"), (LANES, "kg"), (2 * LANES, "vg")]
    names = ("w_in", "sgu_gain", "w_sgu", "b_sgu", "q_gain", "wq", "kv_gain", "wkv", "gq_gain", "gk_gain")
    params = [weights[k] for k in names]
    return pl.pallas_call(
        _group_hs(functools.partial(_in_kernel, n_tok=n_tok, n_ctx=n_ctx), len(hs_parts)),
        grid=(t // tm,),
        in_specs=_hs_specs(hs_parts, dims) + [_mod_spec(mods, layer, dims), tab, tab, tab, tab]
                 + [_layer_spec(a, layer) for a in params] + [_const_spec((BRANCH_W, BRANCH_W))],
        out_specs=[row(w) for w, _ in outs],
        out_shape=[jax.ShapeDtypeStruct((t, w), BF16) for w, _ in outs],
        compiler_params=_cparams(("arbitrary",)),
        name="in_proj",
    )(*hs_parts, mods, tabs["cosm"], tabs["sinm"], tabs["cosg"], tabs["sing"], *params, tabs["gmat"])


def _attend(q, k_ref, v_ref, nk, k_cols, v_cols):
    s = _dot_nt(q, k_ref[0:nk, k_cols])
    m = jnp.max(s, axis=-1, keepdims=True)
    r = _dot(jnp.exp2(s - m).astype(BF16), v_ref[0:nk, v_cols])
    return r[:, :LANES] * (1.0 / r[:, LANES:LANES + 1])


def _attn_kernel(qm_ref, km_ref, vm_ref, qg_ref, kg_ref, vg_ref, ob_ref, od_ref, *, ctx_tiles, n_ctx,
                 latent_only):
    def run_sample(s, nk):
        qm, km, vm, qg, kg, vg = (r.at[s] for r in (qm_ref, km_ref, vm_ref, qg_ref, kg_ref, vg_ref))
        lane = lax.broadcasted_iota(jnp.int32, (qm.shape[0], LANES), 1)
        outs = []
        for pair in range(MLA_HEADS // 2):
            v_cols = slice(pair * 2 * LANES, (pair + 1) * 2 * LANES)
            res = []
            for hh in range(2):
                cols = slice((pair * 2 + hh) * LANES, (pair * 2 + hh + 1) * LANES)
                res.append(_attend(qm[:, cols], km, vm, nk, cols, v_cols))
            outs.append(jnp.where(lane < MLA_V, res[0], res[1]))
        ob_ref[s] = jnp.concatenate(outs, axis=1).astype(BF16)

        every = slice(None)
        outs = []
        for grp in range(GQA_HEADS // 2):
            q = qg[:, grp * LANES:(grp + 1) * LANES]
            zero = jnp.zeros_like(q)
            r_lo = _attend(jnp.where(lane < GQA_HEAD_DIM, q, zero), kg, vg, nk, every, every)
            r_hi = _attend(jnp.where(lane >= GQA_HEAD_DIM, q, zero), kg, vg, nk, every, every)
            outs.append(jnp.where(lane < GQA_HEAD_DIM, r_lo, r_hi))
        od_ref[s] = jnp.concatenate(outs, axis=1).astype(BF16)

    def run(nk):
        for s in range(qm_ref.shape[0]):
            run_sample(s, nk)

    def skip():
        ob_ref[...] = jnp.zeros_like(ob_ref)
        od_ref[...] = jnp.zeros_like(od_ref)

    j = pl.program_id(1)
    pl.when(j < ctx_tiles)(skip if latent_only else (lambda: run(n_ctx)))
    pl.when(j >= ctx_tiles)(lambda: run(km_ref.shape[1]))


ATTN_SAMPLES = 2


def _attn_call(qm, km, vm, qg, kg, vg, dims, latent_only):
    n_batch, n_tok, n_ctx = dims
    tq = TOKEN_TILE
    ns = ATTN_SAMPLES if n_batch % ATTN_SAMPLES == 0 else 1
    per_sample = lambda a: a.reshape(n_batch, n_tok, a.shape[1])
    qspec = lambda w: pl.BlockSpec((ns, tq, w), lambda b, j: (b, j, 0))
    kspec = lambda w: pl.BlockSpec((ns, n_tok, w), lambda b, j: (b, 0, 0))
    ob, od = pl.pallas_call(
        functools.partial(_attn_kernel, ctx_tiles=n_ctx // tq, n_ctx=n_ctx, latent_only=latent_only),
        grid=(n_batch // ns, n_tok // tq),
        in_specs=[qspec(qm.shape[1]), kspec(km.shape[1]), kspec(vm.shape[1]),
                  qspec(qg.shape[1]), kspec(kg.shape[1]), kspec(vg.shape[1])],
        out_specs=[qspec(BRANCH_W), qspec(BRANCH_W)],
        out_shape=[jax.ShapeDtypeStruct((n_batch, n_tok, BRANCH_W), BF16)] * 2,
        compiler_params=_cparams(("arbitrary", "arbitrary")),
        name="attn",
    )(*(per_sample(a) for a in (qm, km, vm, qg, kg, vg)))
    return ob.reshape(-1, BRANCH_W), od.reshape(-1, BRANCH_W)


def _merge_kernel(hs_refs, mod_ref, ya_ref, yb_ref, yd_ref, p_ref, pprev_ref, pnext_ref, cb_ref,
                  wconv_ref, wgate_ref, bgate_ref, wbranch_ref, wout_ref, wr_hi_ref, wr_lo_ref, br_ref,
                  hs1_ref, hn2_ref, logit_ref, *, n_tok, n_ctx, tile_map):
    tm = hs_refs[0].shape[0]
    mod = mod_ref[...]
    start = (tile_map(pl.program_id(0)) * tm) % n_tok
    hs = _load_hs(hs_refs, start, n_ctx)
    hn = (_rms(hs) * (1.0 + mod[1:2, :]) + mod[0:1, :]).astype(BF16)

    prev_ok = jnp.logical_and(start != 0, start != n_ctx)
    next_ok = jnp.logical_and(start + tm != n_ctx, start + tm != n_tok)
    p = p_ref[...].astype(F32)
    halo_prev = jnp.where(prev_ok, pprev_ref[...].astype(F32)[15:16, :], 0.0)
    halo_next = jnp.where(next_ok, pnext_ref[...].astype(F32)[0:1, :], 0.0)
    rowi = lax.broadcasted_iota(jnp.int32, p.shape, 0)
    prev = jnp.where(rowi == 0, halo_prev, pltpu.roll(p, 1, axis=0))
    nxt = jnp.where(rowi == tm - 1, halo_next, pltpu.roll(p, tm - 1, axis=0))
    wc = wconv_ref[...]
    yc = cb_ref[...].astype(F32) * (wc[0:1, :] * prev + wc[1:2, :] * p + wc[2:3, :] * nxt)

    branches = (ya_ref[...], yb_ref[...], yc.astype(BF16), yd_ref[...])
    merged = None
    for i, y in enumerate(branches):
        half_gate = _dot(hn, wgate_ref[i]) + bgate_ref[i:i + 1, :]
        term = (jnp.tanh(half_gate) + 1.0) * _dot(y, wbranch_ref[i])
        merged = term if merged is None else merged + term
    hs1 = hs + _dot(merged.astype(BF16), wout_ref[...]) * mod[2:3, :]
    hs1_ref[...] = hs1

    hn2 = _rms(hs1) * (1.0 + mod[4:5, :]) + mod[3:4, :]
    _rows_to_tiles(hn2_ref, hn2)

    x_hi, x_lo = _split_bf16(hn2)
    logit_ref[...] = (_dot(x_hi, wr_hi_ref[...]) + _dot(x_lo, wr_hi_ref[...]) + _dot(x_hi, wr_lo_ref[...])
                      + br_ref[...])


ROUTE_SUB = 256


def _route_kernel(logit_ref, route_ref, wts_ref, cnt_ref, carry_ref):
    tm = logit_ref.shape[0]
    lt = logit_ref[...].T
    sub = lax.broadcasted_iota(jnp.int32, (EXPERTS_PER_GROUP, tm), 0)
    neg = -jnp.inf
    top = lambda v: jnp.max(v, axis=0, keepdims=True)
    first = lambda hit: jnp.min(jnp.where(hit, sub, EXPERTS_PER_GROUP), axis=0, keepdims=True)
    gl = jnp.where(sub < N_GROUPS, lt[0:EXPERTS_PER_GROUP, :], neg)
    ge = jnp.exp(gl - top(gl))
    pg = ge / jnp.sum(ge, axis=0, keepdims=True)
    pg_top = top(pg)
    g_sel = first(pg == pg_top)
    le = lt[EXPERTS_PER_GROUP:2 * EXPERTS_PER_GROUP, :]
    for g in range(1, N_GROUPS):
        le = jnp.where(g_sel == g, lt[(g + 1) * EXPERTS_PER_GROUP:(g + 2) * EXPERTS_PER_GROUP, :], le)
    ee = jnp.exp(le - top(le))
    pe = ee / jnp.sum(ee, axis=0, keepdims=True)
    p1 = top(pe)
    i1 = first(pe == p1)
    pe2 = jnp.where(sub != i1, pe, neg)
    p2 = top(pe2)
    i2 = first(jnp.logical_and(sub != i1, pe2 == p2))
    denom = p1 + p2
    e1 = g_sel * EXPERTS_PER_GROUP + i1
    e2 = g_sel * EXPERTS_PER_GROUP + i2
    wts_ref[...] = jnp.where(sub == 0, pg_top * p1 / denom, jnp.where(sub == 1, pg_top * p2 / denom, 0.0))

    @pl.when(pl.program_id(0) == 0)
    def _():
        carry_ref[...] = jnp.zeros_like(carry_ref)

    ts = ROUTE_SUB
    erow = lax.broadcasted_iota(jnp.int32, (N_EXPERTS, ts), 0)
    ri = lax.broadcasted_iota(jnp.int32, (ts, ts + LANES), 0)
    ci = lax.broadcasted_iota(jnp.int32, (ts, ts + LANES), 1)
    before = jnp.where(jnp.logical_or(ri < ci, ci >= ts), 1.0, 0.0).astype(BF16)
    carry = carry_ref[...]
    r1s, r2s = [], []
    for s in range(tm // ts):
        oh1 = jnp.where(erow == e1[:, s * ts:(s + 1) * ts], 1.0, 0.0)
        oh2 = jnp.where(erow == e2[:, s * ts:(s + 1) * ts], 1.0, 0.0)
        pref = _dot(jnp.concatenate([oh1, oh2], axis=0).astype(BF16), before)
        c1 = pref[:N_EXPERTS, ts:]
        c2 = pref[N_EXPERTS:, ts:]
        r1s.append(jnp.sum(oh1 * (carry[:, 0:1] + pref[:N_EXPERTS, :ts]), axis=0, keepdims=True))
        r2s.append(jnp.sum(oh2 * ((carry + c1)[:, 0:1] + pref[N_EXPERTS:, :ts]), axis=0, keepdims=True))
        carry = carry + c1 + c2
    r1 = jnp.concatenate(r1s, axis=1)
    r2 = jnp.concatenate(r2s, axis=1)
    carry_ref[...] = carry
    cnt_ref[...] = carry.astype(jnp.int32)
    route_ref[...] = jnp.where(sub == 0, e1, jnp.where(sub == 1, e2,
                               jnp.where(sub == 2, r1.astype(jnp.int32),
                                         jnp.where(sub == 3, r2.astype(jnp.int32), 0))))


def _route_call(logits):
    t = logits.shape[0]
    tr = 1024 if t % 1024 == 0 else ROUTE_SUB
    return pl.pallas_call(
        _route_kernel,
        grid=(t // tr,),
        in_specs=[pl.BlockSpec((tr, LANES), lambda i: (i, 0))],
        out_specs=[pl.BlockSpec((8, tr), lambda i: (0, i)), pl.BlockSpec((8, tr), lambda i: (0, i)),
                   pl.BlockSpec((N_EXPERTS, LANES), lambda i: (0, 0))],
        out_shape=[jax.ShapeDtypeStruct((8, t), jnp.int32), jax.ShapeDtypeStruct((8, t), F32),
                   jax.ShapeDtypeStruct((N_EXPERTS, LANES), jnp.int32)],
        scratch_shapes=[pltpu.VMEM((N_EXPERTS, LANES), F32)],
        compiler_params=_cparams(("arbitrary",)),
        name="route",
    )(logits)


def _token_tiles(dims, latent_only):
    n_batch, n_tok, n_ctx = dims
    tiles, ctx_tiles = n_tok // TOKEN_TILE, n_ctx // TOKEN_TILE
    if not latent_only:
        return n_batch * tiles, lambda i: i
    lat = tiles - ctx_tiles
    return n_batch * lat, lambda i: (i // lat) * tiles + ctx_tiles + i % lat


def _merge_call(hs_parts, mods, ya, yb, yd, p, cb, weights, layer, dims, latent_only):
    n_batch, n_tok, n_ctx = dims
    t, d = n_batch * n_tok, hs_parts[0].shape[1]
    tm = TOKEN_TILE
    sub = 16
    n_sub = t // sub
    n_steps, tile_map = _token_tiles(dims, latent_only)
    t_out = n_steps * tm
    row_in = lambda w: pl.BlockSpec((tm, w), lambda i: (tile_map(i), 0))
    row = lambda w: pl.BlockSpec((tm, w), lambda i: (i, 0))
    names = ("w_conv", "w_gate", "b_gate", "w_branch", "w_out", "wr_hi", "wr_lo", "br")
    params = [weights[k] for k in names]
    return pl.pallas_call(
        _group_hs(functools.partial(_merge_kernel, n_tok=n_tok, n_ctx=n_ctx, tile_map=tile_map), len(hs_parts)),
        grid=(n_steps,),
        in_specs=_hs_specs(hs_parts, dims, tile_map) + [
                  _mod_spec(mods, layer, dims, tile_map),
                  row_in(BRANCH_W), row_in(BRANCH_W), row_in(BRANCH_W), row_in(BRANCH_W),
                  pl.BlockSpec((sub, BRANCH_W),
                               lambda i: (jnp.maximum(tile_map(i) * (tm // sub) - 1, 0), 0)),
                  pl.BlockSpec((sub, BRANCH_W),
                               lambda i: (jnp.minimum((tile_map(i) + 1) * (tm // sub), n_sub - 1), 0)),
                  row_in(BRANCH_W)] + [_layer_spec(a, layer) for a in params],
        out_specs=[row(d), pl.BlockSpec((tm * ROW_SUB, LANES), lambda i: (i, 0)), row(LANES)],
        out_shape=[jax.ShapeDtypeStruct((t_out, d), F32), jax.ShapeDtypeStruct((t_out * ROW_SUB, LANES), F32),
                   jax.ShapeDtypeStruct((t_out, LANES), F32)],
        compiler_params=_cparams(("arbitrary",)),
        name="merge_route",
    )(*hs_parts, mods, ya, yb, yd, p, p, p, cb, *params)


PAD_BITS = tuple(1 << b for b in reversed(range(EXPERT_BLK.bit_length() - 1)))


def _rows_to_tiles(ref, v):
    for ck in range(ROW_SUB):
        ref[pl.ds(ck, v.shape[0], stride=ROW_SUB), :] = v[:, ck * LANES:(ck + 1) * LANES]


def _tiles_to_rows(ref, first, n):
    return jnp.concatenate([ref[pl.ds(first * ROW_SUB + ck, n, stride=ROW_SUB), :] for ck in range(ROW_SUB)],
                           axis=1)


def _tile_of(ref, row, n=1):
    return ref.at[pl.ds(pl.multiple_of(row * ROW_SUB, ROW_SUB), n * ROW_SUB)]


def _tiles_wait(like, sem):
    pltpu.make_async_copy(like, like, sem).wait()


def _dispatch_kernel(pos_ref, pad_start_ref, pad_len_ref, tail_ref, x_ref, xs_hbm, stage, zeros, sems, pad_sem):
    i = pl.program_id(0)
    n = pl.num_programs(0)
    tm = x_ref.shape[0] // ROW_SUB
    slot = i % 2

    def pad_copies(e, fn):
        start = pad_start_ref[e]
        ln = pad_len_ref[e]
        for b in PAD_BITS:
            @pl.when((ln & b) != 0)
            def _():
                off = ln & ~(2 * b - 1)
                fn(pltpu.make_async_copy(_tile_of(zeros, 0, b), _tile_of(xs_hbm, start + off, b), pad_sem))

    def for_each_expert(fn):
        def body(e, carry):
            pad_copies(e, fn)
            return carry
        lax.fori_loop(0, N_EXPERTS, body, 0)

        def tail(j, carry):
            fn(pltpu.make_async_copy(zeros, _tile_of(xs_hbm, j * PAD_BITS[0], PAD_BITS[0]), pad_sem))
            return carry
        lax.fori_loop(tail_ref[0] // PAD_BITS[0], xs_hbm.shape[0] // (ROW_SUB * PAD_BITS[0]), tail, 0)

    @pl.when(i == 0)
    def _():
        zeros[...] = jnp.zeros_like(zeros)
        for_each_expert(lambda cp: cp.start())

    @pl.when(i >= 2)
    def _():
        _tiles_wait(stage, sems.at[slot])

    stage[slot] = x_ref[...]

    def push(r, carry):
        for k in range(2):
            dst = pos_ref[i * (2 * tm) + k * tm + r]
            pltpu.make_async_copy(_tile_of(stage.at[slot], r), _tile_of(xs_hbm, dst),
                                  sems.at[slot]).start(priority=k)
        return carry
    lax.fori_loop(0, tm, push, 0, unroll=8)

    @pl.when(i == n - 1)
    def _():
        _tiles_wait(stage, sems.at[slot])

        @pl.when(n >= 2)
        def _():
            _tiles_wait(stage, sems.at[1 - slot])
        for_each_expert(lambda cp: cp.wait())


def _dispatch_call(pos, pad_start, pad_len, tail_start, x_tiles, cap):
    tm = TOKEN_TILE
    grid_spec = pltpu.PrefetchScalarGridSpec(
        num_scalar_prefetch=4,
        grid=(x_tiles.shape[0] // (tm * ROW_SUB),),
        in_specs=[pl.BlockSpec((tm * ROW_SUB, LANES), lambda i, *_: (i, 0))],
        out_specs=pl.BlockSpec(memory_space=pl.ANY),
        scratch_shapes=[pltpu.VMEM((2, tm * ROW_SUB, LANES), F32), pltpu.VMEM((PAD_BITS[0] * ROW_SUB, LANES), F32),
                        pltpu.SemaphoreType.DMA((2,)), pltpu.SemaphoreType.DMA(())],
    )
    return pl.pallas_call(
        _dispatch_kernel,
        grid_spec=grid_spec,
        out_shape=jax.ShapeDtypeStruct((cap * ROW_SUB, LANES), F32),
        compiler_params=_cparams(("arbitrary",)),
        name="dispatch",
    )(pos, pad_start, pad_len, tail_start, x_tiles)


EXPERT_AHEAD = 3


def _expert_kernel(blk_e_ref, n_used_ref, x_hbm, wgu_ref, wdn_ref, y_ref, wgu_bf, wdn_bf, xbuf, sems):
    i = pl.program_id(0)
    n_used = n_used_ref[0]
    n_buf = xbuf.shape[0]
    rows = xbuf.shape[1]

    def fetch(j):
        first = j * rows if isinstance(j, int) else pl.multiple_of(j * rows, rows)
        return pltpu.make_async_copy(x_hbm.at[pl.ds(first, rows)], xbuf.at[j % n_buf], sems.at[j % n_buf])

    @pl.when(i == 0)
    def _():
        for j in range(EXPERT_AHEAD):
            pl.when(j < n_used)(lambda j=j: fetch(j).start())

    @pl.when(i + EXPERT_AHEAD < n_used)
    def _():
        fetch(i + EXPERT_AHEAD).start()

    @pl.when(jnp.logical_or(i == 0, blk_e_ref[i] != blk_e_ref[jnp.maximum(i - 1, 0)]))
    def _():
        wgu_bf[...] = wgu_ref[...].astype(BF16)
        wdn_bf[...] = wdn_ref[...].astype(BF16)

    @pl.when(i < n_used)
    def _():
        fetch(i).wait()
        xe = _tiles_to_rows(xbuf.at[i % n_buf], 0, EXPERT_BLK).astype(BF16)
        gu = _dot(xe, wgu_bf[...])
        g = gu[:, :D_EXPERT]
        act = (g * _sigmoid(g) * gu[:, D_EXPERT:]).astype(BF16)
        _rows_to_tiles(y_ref, _dot(act, wdn_bf[...]))

    @pl.when(i >= n_used)
    def _():
        y_ref[...] = jnp.zeros_like(y_ref)


def _expert_call(blk_e, n_used, xs, w_gu, w_dn, layer):
    d = w_gu.shape[2]
    blk = EXPERT_BLK
    grid_spec = pltpu.PrefetchScalarGridSpec(
        num_scalar_prefetch=2,
        grid=(xs.shape[0] // (blk * ROW_SUB),),
        in_specs=[pl.BlockSpec(memory_space=pl.ANY),
                  pl.BlockSpec((None, None, d, 2 * D_EXPERT), lambda i, be, nu: (layer, be[i], 0, 0)),
                  pl.BlockSpec((None, None, D_EXPERT, d), lambda i, be, nu: (layer, be[i], 0, 0))],
        out_specs=pl.BlockSpec((blk * ROW_SUB, LANES), lambda i, be, nu: (i, 0)),
        scratch_shapes=[pltpu.VMEM((d, 2 * D_EXPERT), BF16), pltpu.VMEM((D_EXPERT, d), BF16),
                        pltpu.VMEM((EXPERT_AHEAD + 1, blk * ROW_SUB, LANES), F32),
                        pltpu.SemaphoreType.DMA((EXPERT_AHEAD + 1,))],
    )
    return pl.pallas_call(
        _expert_kernel,
        grid_spec=grid_spec,
        out_shape=jax.ShapeDtypeStruct(xs.shape, F32),
        compiler_params=_cparams(("arbitrary",)),
        name="experts",
    )(blk_e, n_used, xs, w_gu, w_dn)


COMBINE_BUFS = 3


def _combine_kernel(pos_ref, y_hbm, hs1_ref, wts_ref, mod_ref, gain_ref, o_ref, gbuf, sems, *,
                    final):
    i = pl.program_id(0)
    n = pl.num_programs(0)
    tm = hs1_ref.shape[0]
    n_buf = gbuf.shape[0]
    slot = i % n_buf

    def start(step):
        base = step * (2 * tm)
        s = step % n_buf

        def body(r, carry):
            for k in range(2):
                src = pos_ref[base + k * tm + r]
                pltpu.make_async_copy(_tile_of(y_hbm, src), _tile_of(gbuf.at[s], k * tm + r),
                                      sems.at[s]).start(priority=k)
            return carry
        lax.fori_loop(0, tm, body, 0, unroll=8)

    @pl.when(i == 0)
    def _():
        for step in range(n_buf - 1):
            pl.when(step < n)(lambda step=step: start(step))

    @pl.when(i + n_buf - 1 < n)
    def _():
        start(i + n_buf - 1)

    _tiles_wait(gbuf.at[slot], sems.at[slot])
    wts = wts_ref[...]
    g = gbuf.at[slot]
    f = wts[:, 0:1] * _tiles_to_rows(g, 0, tm) + wts[:, 1:2] * _tiles_to_rows(g, tm, tm)
    out = hs1_ref[...] + f * mod_ref[...][5:6, :]
    if final:
        out = _rms(out) * gain_ref[...]
    o_ref[...] = out


def _combine_call(pos, y, hs1, wts, mods, layer, gain, dims, final):
    t, d = hs1.shape
    tm = TOKEN_TILE
    n_steps, tile_map = _token_tiles(dims, final)
    assert n_steps * tm == t
    grid_spec = pltpu.PrefetchScalarGridSpec(
        num_scalar_prefetch=1,
        grid=(n_steps,),
        in_specs=[pl.BlockSpec(memory_space=pl.ANY),
                  pl.BlockSpec((tm, d), lambda i, pos: (i, 0)),
                  pl.BlockSpec((tm, wts.shape[1]), lambda i, pos: (i, 0)),
                  _mod_spec(mods, layer, dims, tile_map),
                  pl.BlockSpec((1, d), lambda i, pos: (0, 0))],
        out_specs=pl.BlockSpec((tm, d), lambda i, pos: (i, 0)),
        scratch_shapes=[pltpu.VMEM((COMBINE_BUFS, 2 * tm * ROW_SUB, LANES), F32),
                        pltpu.SemaphoreType.DMA((COMBINE_BUFS,))],
    )
    return pl.pallas_call(
        functools.partial(_combine_kernel, final=final),
        grid_spec=grid_spec,
        out_shape=jax.ShapeDtypeStruct((n_steps * tm, d), F32),
        compiler_params=_cparams(("arbitrary",)),
        name="combine_final" if final else "combine",
    )(pos, y, hs1, wts, mods, gain)


def _rope_angles(n_ctx, rows, rot_dim):
    n_freq = rot_dim // 4
    freqs = ROPE_THETA ** (-jnp.arange(n_freq, dtype=F32) / n_freq)
    row = jnp.repeat(jnp.arange(rows, dtype=F32), GRID_W)
    col = (jnp.arange(rows * GRID_W) % GRID_W).astype(F32)
    ang = jnp.concatenate([row[:, None] * freqs, col[:, None] * freqs], axis=-1)
    ang = jnp.concatenate([jnp.zeros((n_ctx, rot_dim // 2), F32), ang], axis=0)
    return jnp.cos(ang), jnp.sin(ang)


def _tables(n_ctx, seq):
    n = n_ctx + seq
    cm, sm = _rope_angles(n_ctx, seq // GRID_W, MLA_ROPE)
    cg, sg = _rope_angles(n_ctx, seq // GRID_W, GQA_HEAD_DIM)
    one, zero = jnp.ones((n, MLA_NOPE), F32), jnp.zeros((n, MLA_NOPE), F32)
    pad = jnp.zeros((n, LANES - MLA_NOPE - MLA_ROPE), F32)
    gi = jnp.arange(BRANCH_W) // GQA_HEAD_DIM
    return {
        "cosm": jnp.concatenate([one, cm, cm, pad], axis=1),
        "sinm": jnp.concatenate([zero, -sm, sm, pad], axis=1),
        "cosg": jnp.concatenate([cg, cg, cg, cg], axis=1),
        "sing": jnp.concatenate([-sg, sg, -sg, sg], axis=1),
        "gmat": jnp.where(gi[:, None] == gi[None, :], 1.0 / GQA_HEAD_DIM, 0.0).astype(BF16),
    }


def _prep_weights(w_in, sgu_gain, w_sgu, b_sgu, mla_q_gain, w_uq, mla_kv_gain, w_ukv, w_conv,
                  gqa_q_gain, gqa_k_gain, w_gate, b_gate, w_branch, w_out, w_group_router,
                  b_group_router, w_expert_router, b_expert_router):
    nl, d, _ = w_in.shape
    zc = lambda n: jnp.zeros((nl, d, n), F32)
    half = MLA_ROPE // 2
    kpe = w_in[..., 896:928]
    gq = w_in[..., 1696:1952].reshape(nl, d, GQA_HEADS, GQA_HEAD_DIM)[:, :, (0, 2, 1, 3), :].reshape(nl, d, BRANCH_W)
    w_in_ext = jnp.concatenate([
        w_in[..., 0:896],
        zc(MLA_NOPE), kpe, zc(LANES - MLA_NOPE - MLA_ROPE),
        zc(MLA_NOPE), kpe[..., half:], kpe[..., :half], zc(LANES - MLA_NOPE - MLA_ROPE),
        w_in[..., 928:1696], gq, w_in[..., 1952:2208]], axis=-1).astype(BF16)

    dq = MLA_NOPE + MLA_ROPE
    uq = w_uq.reshape(nl, -1, MLA_HEADS, dq)
    zq = jnp.zeros(uq.shape[:3] + (LANES - dq,), F32)
    wq_a = jnp.concatenate([uq, zq], axis=-1)
    wq_b = jnp.concatenate([jnp.zeros_like(uq[..., :MLA_NOPE]), uq[..., MLA_NOPE + half:],
                            uq[..., MLA_NOPE:MLA_NOPE + half], zq], axis=-1)
    wq = jnp.concatenate([wq_a.reshape(nl, -1, MLA_HEADS * LANES),
                          wq_b.reshape(nl, -1, MLA_HEADS * LANES)], axis=-1).astype(BF16)

    ukv = w_ukv.reshape(nl, -1, MLA_HEADS, MLA_NOPE + MLA_V)
    k_lay = jnp.concatenate([ukv[..., :MLA_NOPE], jnp.zeros_like(ukv[..., :LANES - MLA_NOPE])], axis=-1)
    v_h = ukv[..., MLA_NOPE:]
    zv = jnp.zeros(v_h.shape[:2] + (LANES,), F32)
    v_lay = jnp.concatenate([v_h[:, :, 0], v_h[:, :, 1], zv, v_h[:, :, 2], v_h[:, :, 3], zv], axis=-1)
    wkv = jnp.concatenate([k_lay.reshape(nl, -1, MLA_HEADS * LANES), v_lay], axis=-1).astype(BF16)

    wb = w_branch
    wb_d = wb[:, 3].reshape(nl, GQA_HEADS, GQA_HEAD_DIM, d)[:, (0, 2, 1, 3)].reshape(nl, BRANCH_W, d)
    w_branch_p = (0.5 * jnp.concatenate([wb[:, :3], wb_d[:, None]], axis=1)).astype(BF16)

    gap = EXPERTS_PER_GROUP - N_GROUPS
    rest = LANES - EXPERTS_PER_GROUP - N_EXPERTS
    wr = jnp.concatenate([w_group_router, jnp.zeros((nl, d, gap), F32), w_expert_router,
                          jnp.zeros((nl, d, rest), F32)], axis=-1)
    wr_hi = wr.astype(BF16)
    wr_lo = (wr - wr_hi.astype(F32)).astype(BF16)
    br = jnp.concatenate([b_group_router, jnp.zeros((nl, gap), F32), b_expert_router,
                          jnp.zeros((nl, rest), F32)], axis=-1)[:, None, :]

    tile = lambda g, reps: jnp.concatenate([g] * reps, axis=-1)[:, None, :]
    return {
        "w_in": w_in_ext,
        "sgu_gain": sgu_gain[:, None, :],
        "w_sgu": w_sgu.astype(BF16),
        "b_sgu": jnp.repeat(b_sgu.transpose(0, 2, 1), BRANCH_W // SGU_GROUPS, axis=-1),
        "q_gain": mla_q_gain[:, None, :], "wq": wq,
        "kv_gain": mla_kv_gain[:, None, :], "wkv": wkv,
        "gq_gain": tile(gqa_q_gain, 4), "gk_gain": tile(gqa_k_gain, 2),
        "w_conv": w_conv,
        "w_gate": (0.5 * w_gate).astype(BF16), "b_gate": 0.5 * b_gate,
        "w_branch": w_branch_p, "w_out": w_out.astype(BF16),
        "wr_hi": wr_hi, "wr_lo": wr_lo, "br": br,
    }


def _dispatch_plan(route, cnt, n_tiles, cap):
    blk = EXPERT_BLK
    tm = route.shape[1] // n_tiles
    counts = cnt[:, 0]
    padded = (counts + blk - 1) // blk * blk
    pend = jnp.cumsum(padded)
    pstart = pend - padded
    eid, rank = route[0:2], route[2:4]
    onehot = eid[:, :, None] == jnp.arange(N_EXPERTS, dtype=jnp.int32)
    dest = jnp.sum(jnp.where(onehot, pstart, 0), axis=-1) + rank
    pos = dest.reshape(2, n_tiles, tm).transpose(1, 0, 2).reshape(-1)
    n_used = pend[-1] // blk
    blk_first = jnp.minimum(jnp.arange(cap // blk, dtype=jnp.int32), n_used - 1) * blk
    blk_e = jnp.sum((blk_first[:, None] >= pend[None, :]).astype(jnp.int32), axis=1)
    blk_e = jnp.minimum(blk_e, N_EXPERTS - 1)
    return pos, pstart + counts, padded - counts, blk_e, n_used.reshape(1)


def kernel(x, c, ctx, c_ctx, w_ada, b_ada, w_in, sgu_gain, w_sgu, b_sgu, mla_q_gain, w_uq, mla_kv_gain, w_ukv, w_conv, gqa_q_gain, gqa_k_gain, w_gate, b_gate, w_branch, w_out, w_group_router, b_group_router, w_expert_router, b_expert_router, w_expert_gate_up, w_expert_down, final_gain):
    n_batch, seq, d = x.shape
    n_ctx = ctx.shape[1]
    n_tok = n_ctx + seq
    t = n_batch * n_tok
    depth = w_in.shape[0]
    tm = TOKEN_TILE
    assert n_ctx % tm == 0 and seq % tm == 0 and seq % GRID_W == 0 and tm % SGU_CHUNK == 0
    dims = (n_batch, n_tok, n_ctx)

    hs = [ctx.reshape(n_batch * n_ctx, d), x.reshape(n_batch * seq, d)]
    tabs = _tables(n_ctx, seq)
    rows = -(-(n_batch + 1) // 8) * 8
    cvec = jnp.concatenate([c, c_ctx[None, :], jnp.zeros((rows - n_batch - 1, d), F32)], axis=0)
    mods = _ada_call(cvec, w_ada, b_ada).reshape(depth, rows, 6, d)
    weights = _prep_weights(w_in, sgu_gain, w_sgu, b_sgu, mla_q_gain, w_uq, mla_kv_gain, w_ukv, w_conv,
                            gqa_q_gain, gqa_k_gain, w_gate, b_gate, w_branch, w_out, w_group_router,
                            b_group_router, w_expert_router, b_expert_router)
    gain = final_gain[None, :]

    for l in range(depth):
        last = l == depth - 1
        ya, qm, km, vm, p, cb, qg, kg, vg = _in_call(hs, mods, tabs, weights, l, dims)
        yb, yd = _attn_call(qm, km, vm, qg, kg, vg, dims, latent_only=last)
        hs1, hn2, logits = _merge_call(hs, mods, ya, yb, yd, p, cb, weights, l, dims, latent_only=last)
        route, wts, cnt = _route_call(logits)
        t_moe = hs1.shape[0]
        cap = -(-2 * t_moe // EXPERT_BLK) * EXPERT_BLK + N_EXPERTS * EXPERT_BLK
        pos, pad_start, pad_len, blk_e, n_used = _dispatch_plan(route, cnt, t_moe // tm, cap)
        xs = _dispatch_call(pos, pad_start, pad_len, n_used * EXPERT_BLK, hn2, cap)
        y = _expert_call(blk_e, n_used, xs, w_expert_gate_up, w_expert_down, l)
        hs = [_combine_call(pos, y, hs1, wts.T, mods, l, gain, dims, final=last)]
    return hs[0].reshape(n_batch, seq, d)
```

```python
import functools
import math

import jax
import jax.numpy as jnp
from jax import lax
from jax.experimental import pallas as pl
from jax.experimental.pallas import tpu as pltpu

F32 = jnp.float32
BF16 = jnp.bfloat16

GRID_W = 64
ROPE_THETA = 10000.0
NORM_EPS = 1e-6
BRANCH_W = 256
SGU_CHUNK = 128
SGU_GROUPS = 4
MLA_HEADS = 4
MLA_NOPE = 64
MLA_ROPE = 32
MLA_V = 64
GQA_HEADS = 4
GQA_HEAD_DIM = 64
N_GROUPS = 4
EXPERTS_PER_GROUP = 8
N_EXPERTS = N_GROUPS * EXPERTS_PER_GROUP
D_EXPERT = 256
LANES = 128
ROW_SUB = 8
BF16_SUBLANES = 16
ADA_COLS = 1536
ROUTE_STEP = 1024
TOKEN_TILE = 256
EXPERT_BLK = 256
VMEM_LIMIT = 56 * 1024 * 1024
LOG2E = 1.4426950408889634

C_AU, C_AV, C_MQ, C_MKV, C_KPA, C_KPB, C_CB, C_CC, C_CX, C_GQ, C_GK, C_GV, C_END = (
    0, 256, 512, 768, 896, 1024, 1152, 1408, 1664, 1920, 2176, 2304, 2432)


def _cparams(sem):
    return pltpu.CompilerParams(dimension_semantics=sem, vmem_limit_bytes=VMEM_LIMIT)


def _dot(a, b):
    return jnp.dot(a, b, preferred_element_type=F32)


def _dot_nt(a, b):
    return lax.dot_general(a, b, (((1,), (1,)), ((), ())), preferred_element_type=F32)


def _split_bf16(v):
    hi = v.astype(BF16)
    lo = (v - hi.astype(F32)).astype(BF16)
    return hi, lo


def _gelu(v):
    return 0.5 * v * (1.0 + jnp.tanh(math.sqrt(2.0 / math.pi) * (v + 0.044715 * (v * v * v))))


def _sigmoid(v):
    return 0.5 * jnp.tanh(0.5 * v) + 0.5


def _rms(v):
    return v * lax.rsqrt(jnp.mean(v * v, axis=-1, keepdims=True) + NORM_EPS)


def _const_spec(shape):
    nd = len(shape)
    return pl.BlockSpec(shape, lambda *_: (0,) * nd)


def _layer_spec(stacked, layer):
    nd = stacked.ndim - 1
    return pl.BlockSpec((None,) + stacked.shape[1:], lambda *_: (layer,) + (0,) * nd)


def _ada_kernel(c_ref, w_ref, b_ref, o_ref):
    cv = c_ref[...]
    s = cv * _sigmoid(cv)
    s_hi, s_lo = _split_bf16(s)
    w_hi, w_lo = _split_bf16(w_ref[...])
    o_ref[...] = _dot(s_hi, w_hi) + _dot(s_lo, w_hi) + _dot(s_hi, w_lo) + b_ref[...]


def _ada_call(cvec, w_ada, b_ada):
    n_layers, d, d6 = w_ada.shape
    rows = cvec.shape[0]
    tn = ADA_COLS
    return pl.pallas_call(
        _ada_kernel,
        grid=(n_layers, d6 // tn),
        in_specs=[pl.BlockSpec((rows, d), lambda l, j: (0, 0)),
                  pl.BlockSpec((None, d, tn), lambda l, j: (l, 0, j)),
                  pl.BlockSpec((None, 1, tn), lambda l, j: (l, 0, j))],
        out_specs=pl.BlockSpec((None, rows, tn), lambda l, j: (l, 0, j)),
        out_shape=jax.ShapeDtypeStruct((n_layers, rows, d6), F32),
        compiler_params=_cparams(("arbitrary", "arbitrary")),
        name="ada_mod",
    )(cvec, w_ada, b_ada.reshape(n_layers, 1, d6))


def _hs_specs(hs_parts, dims, tile_map=lambda i: i):
    n_batch, n_tok, n_ctx = dims
    tm = TOKEN_TILE
    d = hs_parts[0].shape[1]
    if len(hs_parts) == 1:
        return [pl.BlockSpec((tm, d), lambda i, *_: (tile_map(i), 0))]
    tiles, ctx_tiles = n_tok // tm, n_ctx // tm

    def ctx_map(i, *_):
        g = tile_map(i)
        return ((g // tiles) * ctx_tiles + jnp.minimum(g % tiles, ctx_tiles - 1), 0)

    def lat_map(i, *_):
        g = tile_map(i)
        return ((g // tiles) * (tiles - ctx_tiles) + jnp.maximum(g % tiles - ctx_tiles, 0), 0)
    return [pl.BlockSpec((tm, d), ctx_map), pl.BlockSpec((tm, d), lat_map)]


def _load_hs(hs_refs, start, n_ctx):
    if len(hs_refs) == 1:
        return hs_refs[0][...]
    return jnp.where(start < n_ctx, hs_refs[0][...], hs_refs[1][...])


def _group_hs(kernel, n_parts):
    def wrapped(*refs):
        return kernel(refs[:n_parts], *refs[n_parts:])
    return wrapped


def _in_kernel(hs_refs, mod_ref, cosm_ref, sinm_ref, cosg_ref, sing_ref, w_in_ref, sgu_gain_ref,
               w_sgu_ref, b_sgu_ref, qgain_ref, wq_ref, kvgain_ref, wkv_ref, gqgain_ref, gkgain_ref,
               gmat_ref,
               ya_ref, qm_ref, km_ref, vm_ref, p_ref, cb_ref, qg_ref, kg_ref, vg_ref, *, n_tok, n_ctx):
    tm = hs_refs[0].shape[0]
    mod = mod_ref[...]
    hs = _load_hs(hs_refs, (pl.program_id(0) * tm) % n_tok, n_ctx)
    hn = _rms(hs) * (1.0 + mod[1:2, :]) + mod[0:1, :]
    z = _dot(hn.astype(BF16), w_in_ref[...])

    av = _rms(_gelu(z[:, C_AV:C_MQ])) * sgu_gain_ref[...]
    av = av.astype(BF16)
    grp = lax.broadcasted_iota(jnp.int32, (SGU_CHUNK, BRANCH_W), 1) // (BRANCH_W // SGU_GROUPS)
    chunks = []
    for ci in range(tm // SGU_CHUNK):
        vc = av[ci * SGU_CHUNK:(ci + 1) * SGU_CHUNK, :]
        m = b_sgu_ref[...]
        for g in range(SGU_GROUPS):
            m = m + jnp.where(grp == g, _dot(w_sgu_ref[g], vc), 0.0)
        chunks.append(m)
    mixed = jnp.concatenate(chunks, axis=0)
    ya_ref[...] = (_gelu(z[:, C_AU:C_AV]) * mixed).astype(BF16)

    cosm = cosm_ref[...]
    sinm = sinm_ref[...]
    cos4 = jnp.concatenate([cosm] * MLA_HEADS, axis=1)
    sin4 = jnp.concatenate([sinm] * MLA_HEADS, axis=1)
    nq = (_rms(z[:, C_MQ:C_MKV]) * qgain_ref[...]).astype(BF16)
    qab = _dot(nq, wq_ref[...])
    hw = MLA_HEADS * LANES
    q_scale = (MLA_NOPE + MLA_ROPE) ** -0.5 * LOG2E
    qm_ref[...] = ((qab[:, :hw] * cos4 + qab[:, hw:] * sin4) * q_scale).astype(BF16)
    nkv = (_rms(z[:, C_MKV:C_KPA]) * kvgain_ref[...]).astype(BF16)
    kv = _dot(nkv, wkv_ref[...])
    kpe = z[:, C_KPA:C_KPB] * cosm + z[:, C_KPB:C_CB] * sinm
    km_ref[...] = (kv[:, :hw] + jnp.concatenate([kpe] * MLA_HEADS, axis=1)).astype(BF16)
    lane = lax.broadcasted_iota(jnp.int32, (tm, hw), 1)
    ones_col = jnp.where(lane % (2 * LANES) == LANES, 1.0, 0.0)
    vm_ref[...] = (kv[:, hw:] + ones_col).astype(BF16)

    p_ref[...] = (z[:, C_CC:C_CX] * z[:, C_CX:C_GQ]).astype(BF16)
    cb_ref[...] = z[:, C_CB:C_CC].astype(BF16)

    cosg = cosg_ref[...]
    sing = sing_ref[...]
    gmat = gmat_ref[...]

    def head_norm_rope(v, gain, cos_t, sin_t, scale):
        w = v.shape[1]
        sq_hi, sq_lo = _split_bf16(v * v)
        ms = _dot(sq_hi, gmat[:w, :w]) + _dot(sq_lo, gmat[:w, :w])
        y = v * lax.rsqrt(ms + NORM_EPS) * gain
        half = GQA_HEAD_DIM // 2
        ln = lax.broadcasted_iota(jnp.int32, v.shape, 1) % GQA_HEAD_DIM
        partner = jnp.where(ln < half, pltpu.roll(y, w - half, axis=1), pltpu.roll(y, half, axis=1))
        return ((y * cos_t + partner * sin_t) * scale).astype(BF16)

    cos2 = jnp.concatenate([cosg, cosg], axis=1)
    sin2 = jnp.concatenate([sing, sing], axis=1)
    qg_ref[...] = head_norm_rope(z[:, C_GQ:C_GK], gqgain_ref[...], cos2, sin2,
                                 GQA_HEAD_DIM ** -0.5 * LOG2E)
    kg_ref[...] = head_norm_rope(z[:, C_GK:C_GV], gkgain_ref[...], cosg, sing, 1.0)
    lane2 = lax.broadcasted_iota(jnp.int32, (tm, LANES), 1)
    vg_ref[...] = jnp.concatenate([z[:, C_GV:C_END], jnp.where(lane2 == 0, 1.0, 0.0)], axis=1).astype(BF16)


def _mod_spec(mods, layer, dims, tile_map=lambda i: i):
    n_batch, n_tok, n_ctx = dims

    def index_map(i, *_):
        start = tile_map(i) * TOKEN_TILE
        return (layer, jnp.where(start % n_tok < n_ctx, n_batch, start // n_tok), 0, 0)
    return pl.BlockSpec((None, None) + mods.shape[2:], index_map)


def _in_call(hs_parts, mods, tabs, weights, layer, dims):
    n_batch, n_tok, n_ctx = dims
    t = n_batch * n_tok
    tm = TOKEN_TILE
    tiles_per_seq = n_tok // tm
    row = lambda w: pl.BlockSpec((tm, w), lambda i: (i, 0))
    tab = pl.BlockSpec((tm, LANES), lambda i: (i % tiles_per_seq, 0))
    hw = MLA_HEADS * LANES
    outs = [(BRANCH_W, "ya"), (hw, "qm"), (hw, "km"), (hw, "vm"), (BRANCH_W, "p"), (BRANCH_W, "cb"),
            (BRANCH_W, "qg"), (LANES, "kg"), (2 * LANES, "vg")]
    names = ("w_in", "sgu_gain", "w_sgu", "b_sgu", "q_gain", "wq", "kv_gain", "wkv", "gq_gain", "gk_gain")
    params = [weights[k] for k in names]
    return pl.pallas_call(
        _group_hs(functools.partial(_in_kernel, n_tok=n_tok, n_ctx=n_ctx), len(hs_parts)),
        grid=(t // tm,),
        in_specs=_hs_specs(hs_parts, dims) + [_mod_spec(mods, layer, dims), tab, tab, tab, tab]
                 + [_layer_spec(a, layer) for a in params] + [_const_spec((BRANCH_W, BRANCH_W))],
        out_specs=[row(w) for w, _ in outs],
        out_shape=[jax.ShapeDtypeStruct((t, w), BF16) for w, _ in outs],
        compiler_params=_cparams(("arbitrary",)),
        name="in_proj",
    )(*hs_parts, mods, tabs["cosm"], tabs["sinm"], tabs["cosg"], tabs["sing"], *params, tabs["gmat"])


def _attend(q, k_ref, v_ref, nk, k_cols, v_cols):
    s = _dot_nt(q, k_ref[0:nk, k_cols])
    m = jnp.max(s, axis=-1, keepdims=True)
    r = _dot(jnp.exp2(s - m).astype(BF16), v_ref[0:nk, v_cols])
    return r[:, :LANES] * (1.0 / r[:, LANES:LANES + 1])


def _attn_kernel(qm_ref, km_ref, vm_ref, qg_ref, kg_ref, vg_ref, ob_ref, od_ref, *, ctx_tiles, n_ctx,
                 latent_only):
    def run_sample(s, nk):
        qm, km, vm, qg, kg, vg = (r.at[s] for r in (qm_ref, km_ref, vm_ref, qg_ref, kg_ref, vg_ref))
        lane = lax.broadcasted_iota(jnp.int32, (qm.shape[0], LANES), 1)
        outs = []
        for pair in range(MLA_HEADS // 2):
            v_cols = slice(pair * 2 * LANES, (pair + 1) * 2 * LANES)
            res = []
            for hh in range(2):
                cols = slice((pair * 2 + hh) * LANES, (pair * 2 + hh + 1) * LANES)
                res.append(_attend(qm[:, cols], km, vm, nk, cols, v_cols))
            outs.append(jnp.where(lane < MLA_V, res[0], res[1]))
        ob_ref[s] = jnp.concatenate(outs, axis=1).astype(BF16)

        every = slice(None)
        outs = []
        for grp in range(GQA_HEADS // 2):
            q = qg[:, grp * LANES:(grp + 1) * LANES]
            zero = jnp.zeros_like(q)
            r_lo = _attend(jnp.where(lane < GQA_HEAD_DIM, q, zero), kg, vg, nk, every, every)
            r_hi = _attend(jnp.where(lane >= GQA_HEAD_DIM, q, zero), kg, vg, nk, every, every)
            outs.append(jnp.where(lane < GQA_HEAD_DIM, r_lo, r_hi))
        od_ref[s] = jnp.concatenate(outs, axis=1).astype(BF16)

    def run(nk):
        for s in range(qm_ref.shape[0]):
            run_sample(s, nk)

    def skip():
        ob_ref[...] = jnp.zeros_like(ob_ref)
        od_ref[...] = jnp.zeros_like(od_ref)

    j = pl.program_id(1)
    pl.when(j < ctx_tiles)(skip if latent_only else (lambda: run(n_ctx)))
    pl.when(j >= ctx_tiles)(lambda: run(km_ref.shape[1]))


ATTN_SAMPLES = 2


def _attn_call(qm, km, vm, qg, kg, vg, dims, latent_only):
    n_batch, n_tok, n_ctx = dims
    tq = TOKEN_TILE
    ns = ATTN_SAMPLES if n_batch % ATTN_SAMPLES == 0 else 1
    per_sample = lambda a: a.reshape(n_batch, n_tok, a.shape[1])
    qspec = lambda w: pl.BlockSpec((ns, tq, w), lambda b, j: (b, j, 0))
    kspec = lambda w: pl.BlockSpec((ns, n_tok, w), lambda b, j: (b, 0, 0))
    ob, od = pl.pallas_call(
        functools.partial(_attn_kernel, ctx_tiles=n_ctx // tq, n_ctx=n_ctx, latent_only=latent_only),
        grid=(n_batch // ns, n_tok // tq),
        in_specs=[qspec(qm.shape[1]), kspec(km.shape[1]), kspec(vm.shape[1]),
                  qspec(qg.shape[1]), kspec(kg.shape[1]), kspec(vg.shape[1])],
        out_specs=[qspec(BRANCH_W), qspec(BRANCH_W)],
        out_shape=[jax.ShapeDtypeStruct((n_batch, n_tok, BRANCH_W), BF16)] * 2,
        compiler_params=_cparams(("arbitrary", "arbitrary")),
        name="attn",
    )(*(per_sample(a) for a in (qm, km, vm, qg, kg, vg)))
    return ob.reshape(-1, BRANCH_W), od.reshape(-1, BRANCH_W)


def _merge_kernel(hs_refs, mod_ref, ya_ref, yb_ref, yd_ref, p_ref, pprev_ref, pnext_ref, cb_ref,
                  wconv_ref, wgate_ref, bgate_ref, wbranch_ref, wout_ref, wr_hi_ref, wr_lo_ref, br_ref,
                  hs1_ref, hn2_ref, logit_ref, *, n_tok, n_ctx, tile_map):
    tm = hs_refs[0].shape[0]
    mod = mod_ref[...]
    start = (tile_map(pl.program_id(0)) * tm) % n_tok
    hs = _load_hs(hs_refs, start, n_ctx)
    hn = (_rms(hs) * (1.0 + mod[1:2, :]) + mod[0:1, :]).astype(BF16)

    prev_ok = jnp.logical_and(start != 0, start != n_ctx)
    next_ok = jnp.logical_and(start + tm != n_ctx, start + tm != n_tok)
    p = p_ref[...].astype(F32)
    halo_prev = jnp.where(prev_ok, pprev_ref[...].astype(F32)[15:16, :], 0.0)
    halo_next = jnp.where(next_ok, pnext_ref[...].astype(F32)[0:1, :], 0.0)
    rowi = lax.broadcasted_iota(jnp.int32, p.shape, 0)
    prev = jnp.where(rowi == 0, halo_prev, pltpu.roll(p, 1, axis=0))
    nxt = jnp.where(rowi == tm - 1, halo_next, pltpu.roll(p, tm - 1, axis=0))
    wc = wconv_ref[...]
    yc = cb_ref[...].astype(F32) * (wc[0:1, :] * prev + wc[1:2, :] * p + wc[2:3, :] * nxt)

    branches = (ya_ref[...], yb_ref[...], yc.astype(BF16), yd_ref[...])
    merged = None
    for i, y in enumerate(branches):
        half_gate = _dot(hn, wgate_ref[i]) + bgate_ref[i:i + 1, :]
        term = (jnp.tanh(half_gate) + 1.0) * _dot(y, wbranch_ref[i])
        merged = term if merged is None else merged + term
    hs1 = hs + _dot(merged.astype(BF16), wout_ref[...]) * mod[2:3, :]
    hs1_ref[...] = hs1

    hn2 = _rms(hs1) * (1.0 + mod[4:5, :]) + mod[3:4, :]
    _rows_to_tiles(hn2_ref, hn2)

    x_hi, x_lo = _split_bf16(hn2)
    logit_ref[...] = (_dot(x_hi, wr_hi_ref[...]) + _dot(x_lo, wr_hi_ref[...]) + _dot(x_hi, wr_lo_ref[...])
                      + br_ref[...])


ROUTE_SUB = 256


def _route_kernel(logit_ref, route_ref, wts_ref, cnt_ref, carry_ref):
    tm = logit_ref.shape[0]
    lt = logit_ref[...].T
    sub = lax.broadcasted_iota(jnp.int32, (EXPERTS_PER_GROUP, tm), 0)
    neg = -jnp.inf
    top = lambda v: jnp.max(v, axis=0, keepdims=True)
    first = lambda hit: jnp.min(jnp.where(hit, sub, EXPERTS_PER_GROUP), axis=0, keepdims=True)
    gl = jnp.where(sub < N_GROUPS, lt[0:EXPERTS_PER_GROUP, :], neg)
    ge = jnp.exp(gl - top(gl))
    pg = ge / jnp.sum(ge, axis=0, keepdims=True)
    pg_top = top(pg)
    g_sel = first(pg == pg_top)
    le = lt[EXPERTS_PER_GROUP:2 * EXPERTS_PER_GROUP, :]
    for g in range(1, N_GROUPS):
        le = jnp.where(g_sel == g, lt[(g + 1) * EXPERTS_PER_GROUP:(g + 2) * EXPERTS_PER_GROUP, :], le)
    ee = jnp.exp(le - top(le))
    pe = ee / jnp.sum(ee, axis=0, keepdims=True)
    p1 = top(pe)
    i1 = first(pe == p1)
    pe2 = jnp.where(sub != i1, pe, neg)
    p2 = top(pe2)
    i2 = first(jnp.logical_and(sub != i1, pe2 == p2))
    denom = p1 + p2
    e1 = g_sel * EXPERTS_PER_GROUP + i1
    e2 = g_sel * EXPERTS_PER_GROUP + i2
    wts_ref[...] = jnp.where(sub == 0, pg_top * p1 / denom, jnp.where(sub == 1, pg_top * p2 / denom, 0.0))

    @pl.when(pl.program_id(0) == 0)
    def _():
        carry_ref[...] = jnp.zeros_like(carry_ref)

    ts = ROUTE_SUB
    erow = lax.broadcasted_iota(jnp.int32, (N_EXPERTS, ts), 0)
    ri = lax.broadcasted_iota(jnp.int32, (ts, ts + LANES), 0)
    ci = lax.broadcasted_iota(jnp.int32, (ts, ts + LANES), 1)
    before = jnp.where(jnp.logical_or(ri < ci, ci >= ts), 1.0, 0.0).astype(BF16)
    carry = carry_ref[...]
    r1s, r2s = [], []
    for s in range(tm // ts):
        oh1 = jnp.where(erow == e1[:, s * ts:(s + 1) * ts], 1.0, 0.0)
        oh2 = jnp.where(erow == e2[:, s * ts:(s + 1) * ts], 1.0, 0.0)
        pref = _dot(jnp.concatenate([oh1, oh2], axis=0).astype(BF16), before)
        c1 = pref[:N_EXPERTS, ts:]
        c2 = pref[N_EXPERTS:, ts:]
        r1s.append(jnp.sum(oh1 * (carry[:, 0:1] + pref[:N_EXPERTS, :ts]), axis=0, keepdims=True))
        r2s.append(jnp.sum(oh2 * ((carry + c1)[:, 0:1] + pref[N_EXPERTS:, :ts]), axis=0, keepdims=True))
        carry = carry + c1 + c2
    r1 = jnp.concatenate(r1s, axis=1)
    r2 = jnp.concatenate(r2s, axis=1)
    carry_ref[...] = carry
    cnt_ref[...] = carry.astype(jnp.int32)
    route_ref[...] = jnp.where(sub == 0, e1, jnp.where(sub == 1, e2,
                               jnp.where(sub == 2, r1.astype(jnp.int32),
                                         jnp.where(sub == 3, r2.astype(jnp.int32), 0))))


def _route_call(logits):
    t = logits.shape[0]
    tr = ROUTE_STEP if t % ROUTE_STEP == 0 else ROUTE_SUB
    return pl.pallas_call(
        _route_kernel,
        grid=(t // tr,),
        in_specs=[pl.BlockSpec((tr, LANES), lambda i: (i, 0))],
        out_specs=[pl.BlockSpec((8, tr), lambda i: (0, i)), pl.BlockSpec((8, tr), lambda i: (0, i)),
                   pl.BlockSpec((N_EXPERTS, LANES), lambda i: (0, 0))],
        out_shape=[jax.ShapeDtypeStruct((8, t), jnp.int32), jax.ShapeDtypeStruct((8, t), F32),
                   jax.ShapeDtypeStruct((N_EXPERTS, LANES), jnp.int32)],
        scratch_shapes=[pltpu.VMEM((N_EXPERTS, LANES), F32)],
        compiler_params=_cparams(("arbitrary",)),
        name="route",
    )(logits)


def _token_tiles(dims, latent_only):
    n_batch, n_tok, n_ctx = dims
    tiles, ctx_tiles = n_tok // TOKEN_TILE, n_ctx // TOKEN_TILE
    if not latent_only:
        return n_batch * tiles, lambda i: i
    lat = tiles - ctx_tiles
    return n_batch * lat, lambda i: (i // lat) * tiles + ctx_tiles + i % lat


def _merge_call(hs_parts, mods, ya, yb, yd, p, cb, weights, layer, dims, latent_only):
    n_batch, n_tok, n_ctx = dims
    t, d = n_batch * n_tok, hs_parts[0].shape[1]
    tm = TOKEN_TILE
    sub = BF16_SUBLANES
    n_sub = t // sub
    n_steps, tile_map = _token_tiles(dims, latent_only)
    t_out = n_steps * tm
    row_in = lambda w: pl.BlockSpec((tm, w), lambda i: (tile_map(i), 0))
    row = lambda w: pl.BlockSpec((tm, w), lambda i: (i, 0))
    names = ("w_conv", "w_gate", "b_gate", "w_branch", "w_out", "wr_hi", "wr_lo", "br")
    params = [weights[k] for k in names]
    return pl.pallas_call(
        _group_hs(functools.partial(_merge_kernel, n_tok=n_tok, n_ctx=n_ctx, tile_map=tile_map), len(hs_parts)),
        grid=(n_steps,),
        in_specs=_hs_specs(hs_parts, dims, tile_map) + [
                  _mod_spec(mods, layer, dims, tile_map),
                  row_in(BRANCH_W), row_in(BRANCH_W), row_in(BRANCH_W), row_in(BRANCH_W),
                  pl.BlockSpec((sub, BRANCH_W),
                               lambda i: (jnp.maximum(tile_map(i) * (tm // sub) - 1, 0), 0)),
                  pl.BlockSpec((sub, BRANCH_W),
                               lambda i: (jnp.minimum((tile_map(i) + 1) * (tm // sub), n_sub - 1), 0)),
                  row_in(BRANCH_W)] + [_layer_spec(a, layer) for a in params],
        out_specs=[row(d), pl.BlockSpec((tm * ROW_SUB, LANES), lambda i: (i, 0)), row(LANES)],
        out_shape=[jax.ShapeDtypeStruct((t_out, d), F32), jax.ShapeDtypeStruct((t_out * ROW_SUB, LANES), F32),
                   jax.ShapeDtypeStruct((t_out, LANES), F32)],
        compiler_params=_cparams(("arbitrary",)),
        name="merge_route",
    )(*hs_parts, mods, ya, yb, yd, p, p, p, cb, *params)


PAD_BITS = tuple(1 << b for b in reversed(range(EXPERT_BLK.bit_length() - 1)))


def _rows_to_tiles(ref, v):
    for ck in range(ROW_SUB):
        ref[pl.ds(ck, v.shape[0], stride=ROW_SUB), :] = v[:, ck * LANES:(ck + 1) * LANES]


def _tiles_to_rows(ref, first, n):
    return jnp.concatenate([ref[pl.ds(first * ROW_SUB + ck, n, stride=ROW_SUB), :] for ck in range(ROW_SUB)],
                           axis=1)


def _tile_of(ref, row, n=1):
    return ref.at[pl.ds(pl.multiple_of(row * ROW_SUB, ROW_SUB), n * ROW_SUB)]


def _tiles_wait(like, sem):
    pltpu.make_async_copy(like, like, sem).wait()


def _dispatch_kernel(pos_ref, pad_start_ref, pad_len_ref, tail_ref, x_ref, xs_hbm, stage, zeros, sems, pad_sem):
    i = pl.program_id(0)
    n = pl.num_programs(0)
    tm = x_ref.shape[0] // ROW_SUB
    slot = i % 2

    def pad_copies(e, fn):
        start = pad_start_ref[e]
        ln = pad_len_ref[e]
        for b in PAD_BITS:
            @pl.when((ln & b) != 0)
            def _():
                off = ln & ~(2 * b - 1)
                fn(pltpu.make_async_copy(_tile_of(zeros, 0, b), _tile_of(xs_hbm, start + off, b), pad_sem))

    def for_each_expert(fn):
        def body(e, carry):
            pad_copies(e, fn)
            return carry
        lax.fori_loop(0, N_EXPERTS, body, 0)

        def tail(j, carry):
            fn(pltpu.make_async_copy(zeros, _tile_of(xs_hbm, j * PAD_BITS[0], PAD_BITS[0]), pad_sem))
            return carry
        lax.fori_loop(tail_ref[0] // PAD_BITS[0], xs_hbm.shape[0] // (ROW_SUB * PAD_BITS[0]), tail, 0)

    @pl.when(i == 0)
    def _():
        zeros[...] = jnp.zeros_like(zeros)
        for_each_expert(lambda cp: cp.start())

    @pl.when(i >= 2)
    def _():
        _tiles_wait(stage, sems.at[slot])

    stage[slot] = x_ref[...]

    def push(r, carry):
        for k in range(2):
            dst = pos_ref[i * (2 * tm) + k * tm + r]
            pltpu.make_async_copy(_tile_of(stage.at[slot], r), _tile_of(xs_hbm, dst),
                                  sems.at[slot]).start(priority=k)
        return carry
    lax.fori_loop(0, tm, push, 0, unroll=8)

    @pl.when(i == n - 1)
    def _():
        _tiles_wait(stage, sems.at[slot])

        @pl.when(n >= 2)
        def _():
            _tiles_wait(stage, sems.at[1 - slot])
        for_each_expert(lambda cp: cp.wait())


def _dispatch_call(pos, pad_start, pad_len, tail_start, x_tiles, cap):
    tm = TOKEN_TILE
    grid_spec = pltpu.PrefetchScalarGridSpec(
        num_scalar_prefetch=4,
        grid=(x_tiles.shape[0] // (tm * ROW_SUB),),
        in_specs=[pl.BlockSpec((tm * ROW_SUB, LANES), lambda i, *_: (i, 0))],
        out_specs=pl.BlockSpec(memory_space=pl.ANY),
        scratch_shapes=[pltpu.VMEM((2, tm * ROW_SUB, LANES), F32), pltpu.VMEM((PAD_BITS[0] * ROW_SUB, LANES), F32),
                        pltpu.SemaphoreType.DMA((2,)), pltpu.SemaphoreType.DMA(())],
    )
    return pl.pallas_call(
        _dispatch_kernel,
        grid_spec=grid_spec,
        out_shape=jax.ShapeDtypeStruct((cap * ROW_SUB, LANES), F32),
        compiler_params=_cparams(("arbitrary",)),
        name="dispatch",
    )(pos, pad_start, pad_len, tail_start, x_tiles)


EXPERT_AHEAD = 3


def _expert_kernel(blk_e_ref, n_used_ref, x_hbm, wgu_ref, wdn_ref, y_ref, wgu_bf, wdn_bf, xbuf, sems):
    i = pl.program_id(0)
    n_used = n_used_ref[0]
    n_buf = xbuf.shape[0]
    rows = xbuf.shape[1]

    def fetch(j):
        first = j * rows if isinstance(j, int) else pl.multiple_of(j * rows, rows)
        return pltpu.make_async_copy(x_hbm.at[pl.ds(first, rows)], xbuf.at[j % n_buf], sems.at[j % n_buf])

    @pl.when(i == 0)
    def _():
        for j in range(EXPERT_AHEAD):
            pl.when(j < n_used)(lambda j=j: fetch(j).start())

    @pl.when(i + EXPERT_AHEAD < n_used)
    def _():
        fetch(i + EXPERT_AHEAD).start()

    @pl.when(jnp.logical_or(i == 0, blk_e_ref[i] != blk_e_ref[jnp.maximum(i - 1, 0)]))
    def _():
        wgu_bf[...] = wgu_ref[...].astype(BF16)
        wdn_bf[...] = wdn_ref[...].astype(BF16)

    @pl.when(i < n_used)
    def _():
        fetch(i).wait()
        xe = _tiles_to_rows(xbuf.at[i % n_buf], 0, EXPERT_BLK).astype(BF16)
        gu = _dot(xe, wgu_bf[...])
        g = gu[:, :D_EXPERT]
        act = (g * _sigmoid(g) * gu[:, D_EXPERT:]).astype(BF16)
        _rows_to_tiles(y_ref, _dot(act, wdn_bf[...]))

    @pl.when(i >= n_used)
    def _():
        y_ref[...] = jnp.zeros_like(y_ref)


def _expert_call(blk_e, n_used, xs, w_gu, w_dn, layer):
    d = w_gu.shape[2]
    blk = EXPERT_BLK
    grid_spec = pltpu.PrefetchScalarGridSpec(
        num_scalar_prefetch=2,
        grid=(xs.shape[0] // (blk * ROW_SUB),),
        in_specs=[pl.BlockSpec(memory_space=pl.ANY),
                  pl.BlockSpec((None, None, d, 2 * D_EXPERT), lambda i, be, nu: (layer, be[i], 0, 0)),
                  pl.BlockSpec((None, None, D_EXPERT, d), lambda i, be, nu: (layer, be[i], 0, 0))],
        out_specs=pl.BlockSpec((blk * ROW_SUB, LANES), lambda i, be, nu: (i, 0)),
        scratch_shapes=[pltpu.VMEM((d, 2 * D_EXPERT), BF16), pltpu.VMEM((D_EXPERT, d), BF16),
                        pltpu.VMEM((EXPERT_AHEAD + 1, blk * ROW_SUB, LANES), F32),
                        pltpu.SemaphoreType.DMA((EXPERT_AHEAD + 1,))],
    )
    return pl.pallas_call(
        _expert_kernel,
        grid_spec=grid_spec,
        out_shape=jax.ShapeDtypeStruct(xs.shape, F32),
        compiler_params=_cparams(("arbitrary",)),
        name="experts",
    )(blk_e, n_used, xs, w_gu, w_dn)


COMBINE_BUFS = 2


def _combine_kernel(pos_ref, y_hbm, hs1_ref, wts_ref, mod_ref, gain_ref, o_ref, gbuf, sems, *,
                    final):
    i = pl.program_id(0)
    n = pl.num_programs(0)
    tm = hs1_ref.shape[0]
    n_buf = gbuf.shape[0]
    slot = i % n_buf

    def start(step):
        base = step * (2 * tm)
        s = step % n_buf

        def body(r, carry):
            for k in range(2):
                src = pos_ref[base + k * tm + r]
                pltpu.make_async_copy(_tile_of(y_hbm, src), _tile_of(gbuf.at[s], k * tm + r),
                                      sems.at[s]).start(priority=k)
            return carry
        lax.fori_loop(0, tm, body, 0, unroll=8)

    @pl.when(i == 0)
    def _():
        for step in range(n_buf - 1):
            pl.when(step < n)(lambda step=step: start(step))

    @pl.when(i + n_buf - 1 < n)
    def _():
        start(i + n_buf - 1)

    _tiles_wait(gbuf.at[slot], sems.at[slot])
    wts = wts_ref[...]
    g = gbuf.at[slot]
    f = wts[:, 0:1] * _tiles_to_rows(g, 0, tm) + wts[:, 1:2] * _tiles_to_rows(g, tm, tm)
    out = hs1_ref[...] + f * mod_ref[...][5:6, :]
    if final:
        out = _rms(out) * gain_ref[...]
    o_ref[...] = out


def _combine_call(pos, y, hs1, wts, mods, layer, gain, dims, final):
    t, d = hs1.shape
    tm = TOKEN_TILE
    n_steps, tile_map = _token_tiles(dims, final)
    assert n_steps * tm == t
    grid_spec = pltpu.PrefetchScalarGridSpec(
        num_scalar_prefetch=1,
        grid=(n_steps,),
        in_specs=[pl.BlockSpec(memory_space=pl.ANY),
                  pl.BlockSpec((tm, d), lambda i, pos: (i, 0)),
                  pl.BlockSpec((tm, wts.shape[1]), lambda i, pos: (i, 0)),
                  _mod_spec(mods, layer, dims, tile_map),
                  pl.BlockSpec((1, d), lambda i, pos: (0, 0))],
        out_specs=pl.BlockSpec((tm, d), lambda i, pos: (i, 0)),
        scratch_shapes=[pltpu.VMEM((COMBINE_BUFS, 2 * tm * ROW_SUB, LANES), F32),
                        pltpu.SemaphoreType.DMA((COMBINE_BUFS,))],
    )
    return pl.pallas_call(
        functools.partial(_combine_kernel, final=final),
        grid_spec=grid_spec,
        out_shape=jax.ShapeDtypeStruct((n_steps * tm, d), F32),
        compiler_params=_cparams(("arbitrary",)),
        name="combine_final" if final else "combine",
    )(pos, y, hs1, wts, mods, gain)


def _rope_angles(n_ctx, rows, rot_dim):
    n_freq = rot_dim // 4
    freqs = ROPE_THETA ** (-jnp.arange(n_freq, dtype=F32) / n_freq)
    row = jnp.repeat(jnp.arange(rows, dtype=F32), GRID_W)
    col = (jnp.arange(rows * GRID_W) % GRID_W).astype(F32)
    ang = jnp.concatenate([row[:, None] * freqs, col[:, None] * freqs], axis=-1)
    ang = jnp.concatenate([jnp.zeros((n_ctx, rot_dim // 2), F32), ang], axis=0)
    return jnp.cos(ang), jnp.sin(ang)


def _tables(n_ctx, seq):
    n = n_ctx + seq
    cm, sm = _rope_angles(n_ctx, seq // GRID_W, MLA_ROPE)
    cg, sg = _rope_angles(n_ctx, seq // GRID_W, GQA_HEAD_DIM)
    one, zero = jnp.ones((n, MLA_NOPE), F32), jnp.zeros((n, MLA_NOPE), F32)
    pad = jnp.zeros((n, LANES - MLA_NOPE - MLA_ROPE), F32)
    gi = jnp.arange(BRANCH_W) // GQA_HEAD_DIM
    return {
        "cosm": jnp.concatenate([one, cm, cm, pad], axis=1),
        "sinm": jnp.concatenate([zero, -sm, sm, pad], axis=1),
        "cosg": jnp.concatenate([cg, cg, cg, cg], axis=1),
        "sing": jnp.concatenate([-sg, sg, -sg, sg], axis=1),
        "gmat": jnp.where(gi[:, None] == gi[None, :], 1.0 / GQA_HEAD_DIM, 0.0).astype(BF16),
    }


def _prep_weights(w_in, sgu_gain, w_sgu, b_sgu, mla_q_gain, w_uq, mla_kv_gain, w_ukv, w_conv,
                  gqa_q_gain, gqa_k_gain, w_gate, b_gate, w_branch, w_out, w_group_router,
                  b_group_router, w_expert_router, b_expert_router):
    nl, d, _ = w_in.shape
    zc = lambda n: jnp.zeros((nl, d, n), F32)
    half = MLA_ROPE // 2
    kpe = w_in[..., 896:928]
    gq = w_in[..., 1696:1952].reshape(nl, d, GQA_HEADS, GQA_HEAD_DIM)[:, :, (0, 2, 1, 3), :].reshape(nl, d, BRANCH_W)
    w_in_ext = jnp.concatenate([
        w_in[..., 0:896],
        zc(MLA_NOPE), kpe, zc(LANES - MLA_NOPE - MLA_ROPE),
        zc(MLA_NOPE), kpe[..., half:], kpe[..., :half], zc(LANES - MLA_NOPE - MLA_ROPE),
        w_in[..., 928:1696], gq, w_in[..., 1952:2208]], axis=-1).astype(BF16)

    dq = MLA_NOPE + MLA_ROPE
    uq = w_uq.reshape(nl, -1, MLA_HEADS, dq)
    zq = jnp.zeros(uq.shape[:3] + (LANES - dq,), F32)
    wq_a = jnp.concatenate([uq, zq], axis=-1)
    wq_b = jnp.concatenate([jnp.zeros_like(uq[..., :MLA_NOPE]), uq[..., MLA_NOPE + half:],
                            uq[..., MLA_NOPE:MLA_NOPE + half], zq], axis=-1)
    wq = jnp.concatenate([wq_a.reshape(nl, -1, MLA_HEADS * LANES),
                          wq_b.reshape(nl, -1, MLA_HEADS * LANES)], axis=-1).astype(BF16)

    ukv = w_ukv.reshape(nl, -1, MLA_HEADS, MLA_NOPE + MLA_V)
    k_lay = jnp.concatenate([ukv[..., :MLA_NOPE], jnp.zeros_like(ukv[..., :LANES - MLA_NOPE])], axis=-1)
    v_h = ukv[..., MLA_NOPE:]
    zv = jnp.zeros(v_h.shape[:2] + (LANES,), F32)
    v_lay = jnp.concatenate([v_h[:, :, 0], v_h[:, :, 1], zv, v_h[:, :, 2], v_h[:, :, 3], zv], axis=-1)
    wkv = jnp.concatenate([k_lay.reshape(nl, -1, MLA_HEADS * LANES), v_lay], axis=-1).astype(BF16)

    wb = w_branch
    wb_d = wb[:, 3].reshape(nl, GQA_HEADS, GQA_HEAD_DIM, d)[:, (0, 2, 1, 3)].reshape(nl, BRANCH_W, d)
    w_branch_p = (0.5 * jnp.concatenate([wb[:, :3], wb_d[:, None]], axis=1)).astype(BF16)

    gap = EXPERTS_PER_GROUP - N_GROUPS
    rest = LANES - EXPERTS_PER_GROUP - N_EXPERTS
    wr = jnp.concatenate([w_group_router, jnp.zeros((nl, d, gap), F32), w_expert_router,
                          jnp.zeros((nl, d, rest), F32)], axis=-1)
    wr_hi = wr.astype(BF16)
    wr_lo = (wr - wr_hi.astype(F32)).astype(BF16)
    br = jnp.concatenate([b_group_router, jnp.zeros((nl, gap), F32), b_expert_router,
                          jnp.zeros((nl, rest), F32)], axis=-1)[:, None, :]

    tile = lambda g, reps: jnp.concatenate([g] * reps, axis=-1)[:, None, :]
    return {
        "w_in": w_in_ext,
        "sgu_gain": sgu_gain[:, None, :],
        "w_sgu": w_sgu.astype(BF16),
        "b_sgu": jnp.repeat(b_sgu.transpose(0, 2, 1), BRANCH_W // SGU_GROUPS, axis=-1),
        "q_gain": mla_q_gain[:, None, :], "wq": wq,
        "kv_gain": mla_kv_gain[:, None, :], "wkv": wkv,
        "gq_gain": tile(gqa_q_gain, 4), "gk_gain": tile(gqa_k_gain, 2),
        "w_conv": w_conv,
        "w_gate": (0.5 * w_gate).astype(BF16), "b_gate": 0.5 * b_gate,
        "w_branch": w_branch_p, "w_out": w_out.astype(BF16),
        "wr_hi": wr_hi, "wr_lo": wr_lo, "br": br,
    }


def _dispatch_plan(route, cnt, n_tiles, cap):
    blk = EXPERT_BLK
    tm = route.shape[1] // n_tiles
    counts = cnt[:, 0]
    padded = (counts + blk - 1) // blk * blk
    pend = jnp.cumsum(padded)
    pstart = pend - padded
    eid, rank = route[0:2], route[2:4]
    onehot = eid[:, :, None] == jnp.arange(N_EXPERTS, dtype=jnp.int32)
    dest = jnp.sum(jnp.where(onehot, pstart, 0), axis=-1) + rank
    pos = dest.reshape(2, n_tiles, tm).transpose(1, 0, 2).reshape(-1)
    n_used = pend[-1] // blk
    blk_first = jnp.minimum(jnp.arange(cap // blk, dtype=jnp.int32), n_used - 1) * blk
    blk_e = jnp.sum((blk_first[:, None] >= pend[None, :]).astype(jnp.int32), axis=1)
    blk_e = jnp.minimum(blk_e, N_EXPERTS - 1)
    return pos, pstart + counts, padded - counts, blk_e, n_used.reshape(1)


def kernel(x, c, ctx, c_ctx, w_ada, b_ada, w_in, sgu_gain, w_sgu, b_sgu, mla_q_gain, w_uq, mla_kv_gain, w_ukv, w_conv, gqa_q_gain, gqa_k_gain, w_gate, b_gate, w_branch, w_out, w_group_router, b_group_router, w_expert_router, b_expert_router, w_expert_gate_up, w_expert_down, final_gain):
    n_batch, seq, d = x.shape
    n_ctx = ctx.shape[1]
    n_tok = n_ctx + seq
    t = n_batch * n_tok
    depth = w_in.shape[0]
    tm = TOKEN_TILE
    assert n_ctx % tm == 0 and seq % tm == 0 and seq % GRID_W == 0 and tm % SGU_CHUNK == 0
    dims = (n_batch, n_tok, n_ctx)

    hs = [ctx.reshape(n_batch * n_ctx, d), x.reshape(n_batch * seq, d)]
    tabs = _tables(n_ctx, seq)
    rows = -(-(n_batch + 1) // 8) * 8
    cvec = jnp.concatenate([c, c_ctx[None, :], jnp.zeros((rows - n_batch - 1, d), F32)], axis=0)
    mods = _ada_call(cvec, w_ada, b_ada).reshape(depth, rows, 6, d)
    weights = _prep_weights(w_in, sgu_gain, w_sgu, b_sgu, mla_q_gain, w_uq, mla_kv_gain, w_ukv, w_conv,
                            gqa_q_gain, gqa_k_gain, w_gate, b_gate, w_branch, w_out, w_group_router,
                            b_group_router, w_expert_router, b_expert_router)
    gain = final_gain[None, :]

    for l in range(depth):
        last = l == depth - 1
        ya, qm, km, vm, p, cb, qg, kg, vg = _in_call(hs, mods, tabs, weights, l, dims)
        yb, yd = _attn_call(qm, km, vm, qg, kg, vg, dims, latent_only=last)
        hs1, hn2, logits = _merge_call(hs, mods, ya, yb, yd, p, cb, weights, l, dims, latent_only=last)
        route, wts, cnt = _route_call(logits)
        t_moe = hs1.shape[0]
        cap = -(-2 * t_moe // EXPERT_BLK) * EXPERT_BLK + N_EXPERTS * EXPERT_BLK
        pos, pad_start, pad_len, blk_e, n_used = _dispatch_plan(route, cnt, t_moe // tm, cap)
        xs = _dispatch_call(pos, pad_start, pad_len, n_used * EXPERT_BLK, hn2, cap)
        y = _expert_call(blk_e, n_used, xs, w_expert_gate_up, w_expert_down, l)
        hs = [_combine_call(pos, y, hs1, wts.T, mods, l, gain, dims, final=last)]
    return hs[0].reshape(n_batch, seq, d)
```
